```python
import jax, jax.numpy as jnp
from jax import lax
import numpy as np

D_MODEL = 2048
BATCH = 4
SEQ = 4096
DEPTH = 1

D_FF = 5504
A_WIDTH = 2048
A_GROUPS = 8
A_CHUNK = 128
N_HEADS = 16
HEAD_DIM = 128
N_KV_GROUPS = 2
HEADS_PER_GROUP = N_HEADS // N_KV_GROUPS
CMP_BLOCK = 32
CMP_STRIDE = 16
CMP_HIDDEN = 256
SEL_BLOCK = 64
SEL_TOP = 16
WINDOW = 512
Q_BLOCK = 64
ROPE_THETA = 500000.0
ROPE_DIM = HEAD_DIM // 4
EPS = 1e-6
NEG = -1e30
FORCE_BONUS = 1e4
B_WIDTH = N_HEADS * HEAD_DIM
KV_WIDTH = N_KV_GROUPS * HEAD_DIM
IN_WIDTH = 2 * A_WIDTH + B_WIDTH + 6 * KV_WIDTH + 3 * N_HEADS + 2 * D_MODEL

kernel_name = "hybrid_gmlp_nsa_macaron_layer"


def rms_norm(x, g):
    xf = x.astype(jnp.float32)
    y = xf * lax.rsqrt(jnp.mean(xf * xf, axis=-1, keepdims=True) + EPS)
    return (y * g.astype(jnp.float32)).astype(x.dtype)


def swiglu(x, w_gate, w_up, w_down):
    return (jax.nn.silu(x @ w_gate) * (x @ w_up)) @ w_down


def partial_rope(x, pos):
    half = ROPE_DIM // 2
    inv = ROPE_THETA ** (-2.0 * jnp.arange(half, dtype=jnp.float32) / ROPE_DIM)
    ang = pos.astype(jnp.float32)[:, None, None] * inv
    cos, sin = jnp.cos(ang), jnp.sin(ang)
    xf = x.astype(jnp.float32)
    x1, x2 = xf[..., :half], xf[..., half:ROPE_DIM]
    out = jnp.concatenate([x1 * cos - x2 * sin, x1 * sin + x2 * cos, xf[..., ROPE_DIM:]], axis=-1)
    return out.astype(x.dtype)


def masked_softmax(s, mask):
    p = jax.nn.softmax(jnp.where(mask, s.astype(jnp.float32), NEG), axis=-1)
    return jnp.where(mask, p, 0.0)


def chunked_gmlp(u, v, v_norm, w_s, b_s):
    B, S, _ = v.shape
    nc = S // A_CHUNK
    v = rms_norm(v, v_norm)
    vc = v.reshape(B, nc, A_CHUNK, A_GROUPS, A_WIDTH // A_GROUPS)
    w = jnp.tril(w_s)
    mixed = jnp.einsum('gts,bcsgd->bctgd', w, vc) + b_s.T[:, :, None]
    return u * mixed.reshape(B, S, A_WIDTH)


def compress_blocks(z, pe, w1, w2):
    B, S, G, dh = z.shape
    nb = (S - CMP_BLOCK) // CMP_STRIDE + 1
    idx = jnp.arange(nb)[:, None] * CMP_STRIDE + jnp.arange(CMP_BLOCK)[None, :]
    blk = z[:, idx] + pe[:, None, :]
    blk = blk.transpose(0, 1, 3, 2, 4).reshape(B, nb, G, CMP_BLOCK * dh)
    return jax.nn.gelu(blk @ w1) @ w2


def native_sparse_attention(q, kc_raw, vc_raw, ks_raw, vs_raw, kw_raw, vw_raw, gate_raw,
                            q_norm, k_cmp_norm, k_slc_norm, k_win_norm,
                            cmp_k_pe, cmp_k_w1, cmp_k_w2, cmp_v_pe, cmp_v_w1, cmp_v_w2):
    B, S, _ = q.shape
    G, HPG, dh = N_KV_GROUPS, HEADS_PER_GROUP, HEAD_DIM
    scale = HEAD_DIM ** -0.5
    t = jnp.arange(S)

    def heads(z, n):
        return z.reshape(B, S, n, dh)

    qh = partial_rope(rms_norm(heads(q, N_HEADS), q_norm), t)
    qg = qh.reshape(B, S, G, HPG, dh).transpose(0, 2, 3, 1, 4)

    kc = compress_blocks(heads(kc_raw, G), cmp_k_pe, cmp_k_w1, cmp_k_w2)
    nb = kc.shape[1]
    cmp_start = jnp.arange(nb) * CMP_STRIDE
    cmp_end = cmp_start + CMP_BLOCK - 1
    kc = partial_rope(rms_norm(kc, k_cmp_norm), cmp_end).transpose(0, 2, 1, 3)
    vc = compress_blocks(heads(vc_raw, G), cmp_v_pe, cmp_v_w1, cmp_v_w2).transpose(0, 2, 1, 3)
    s_c = jnp.einsum('bghsd,bgnd->bghsn', qg, kc) * scale
    p_c = masked_softmax(s_c, cmp_end[None, :] <= t[:, None])
    o_c = jnp.einsum('bghsn,bgnd->bghsd', p_c.astype(vc.dtype), vc)

    n_sel = S // SEL_BLOCK
    n_top = min(SEL_TOP, n_sel)
    sel_start = jnp.arange(n_sel) * SEL_BLOCK
    overlap = ((cmp_start[:, None] < sel_start[None, :] + SEL_BLOCK)
               & (cmp_start[:, None] + CMP_BLOCK > sel_start[None, :])).astype(jnp.float32)
    imp = jnp.einsum('bgsn,nj->bgsj', p_c.sum(axis=2), overlap)
    j = jnp.arange(n_sel)[None, :]
    cur = (t // SEL_BLOCK)[:, None]
    forced = ((j == 0) | (j == cur) | (j == cur - 1)).astype(jnp.float32)
    score = jnp.where(sel_start[None, :] <= t[:, None], imp + FORCE_BONUS * forced, NEG)
    _, sel_idx = lax.top_k(score, n_top)

    ks = partial_rope(rms_norm(heads(ks_raw, G), k_slc_norm), t).transpose(0, 2, 1, 3)
    vs = heads(vs_raw, G).transpose(0, 2, 1, 3)
    ks_blocks = ks.reshape(B, G, n_sel, SEL_BLOCK, dh)
    vs_blocks = vs.reshape(B, G, n_sel, SEL_BLOCK, dh)
    kw = partial_rope(rms_norm(heads(kw_raw, G), k_win_norm), t).transpose(0, 2, 1, 3)
    vw = heads(vw_raw, G).transpose(0, 2, 1, 3)
    pad = ((0, 0), (0, 0), (WINDOW, 0), (0, 0))
    kw_pad, vw_pad = jnp.pad(kw, pad), jnp.pad(vw, pad)

    nq = S // Q_BLOCK
    q_blocks = jnp.moveaxis(qg.reshape(B, G, HPG, nq, Q_BLOCK, dh), 3, 0)
    idx_blocks = jnp.moveaxis(sel_idx.reshape(B, G, nq, Q_BLOCK, n_top), 2, 0)
    bi = jnp.arange(B)[:, None, None, None]
    gi = jnp.arange(G)[None, :, None, None]

    def query_block(args):
        i, q_i, idx_i = args
        t_i = i * Q_BLOCK + jnp.arange(Q_BLOCK)
        k_sel = ks_blocks[bi, gi, idx_i]
        v_sel = vs_blocks[bi, gi, idx_i]
        pos_sel = idx_i[..., None] * SEL_BLOCK + jnp.arange(SEL_BLOCK)
        mask_s = (pos_sel <= t_i[None, None, :, None, None]).reshape(B, G, 1, Q_BLOCK, n_top * SEL_BLOCK)
        s_s = jnp.einsum('bghqd,bgqnkd->bghqnk', q_i, k_sel).reshape(B, G, HPG, Q_BLOCK, n_top * SEL_BLOCK) * scale
        p_s = masked_softmax(s_s, mask_s).reshape(B, G, HPG, Q_BLOCK, n_top, SEL_BLOCK)
        o_s = jnp.einsum('bghqnk,bgqnkd->bghqd', p_s.astype(v_sel.dtype), v_sel)
        k_win = lax.dynamic_slice_in_dim(kw_pad, i * Q_BLOCK, WINDOW + Q_BLOCK, axis=2)
        v_win = lax.dynamic_slice_in_dim(vw_pad, i * Q_BLOCK, WINDOW + Q_BLOCK, axis=2)
        pos_w = i * Q_BLOCK - WINDOW + jnp.arange(WINDOW + Q_BLOCK)
        diff = t_i[:, None] - pos_w[None, :]
        mask_w = (diff >= 0) & (diff < WINDOW) & (pos_w[None, :] >= 0)
        s_w = jnp.einsum('bghqd,bgkd->bghqk', q_i, k_win) * scale
        p_w = masked_softmax(s_w, mask_w)
        o_w = jnp.einsum('bghqk,bgkd->bghqd', p_w.astype(v_win.dtype), v_win)
        return o_s, o_w

    o_s, o_w = lax.map(query_block, (jnp.arange(nq), q_blocks, idx_blocks))
    o_s = jnp.moveaxis(o_s, 0, 3).reshape(B, G, HPG, S, dh)
    o_w = jnp.moveaxis(o_w, 0, 3).reshape(B, G, HPG, S, dh)

    g = jax.nn.sigmoid(gate_raw.reshape(B, S, 3, N_HEADS)).transpose(0, 2, 3, 1)
    g = g.reshape(B, 3, G, HPG, S, 1)
    o = g[:, 0] * o_c + g[:, 1] * o_s + g[:, 2] * o_w
    return o.transpose(0, 3, 1, 2, 4).reshape(B, S, B_WIDTH)


def hybrid_layer(x, ffn1_norm, ffn1_w_gate, ffn1_w_up, ffn1_w_down, mix_norm, w_in,
                 a_v_norm, a_w_s, a_b_s, q_norm, k_cmp_norm, k_slc_norm, k_win_norm,
                 cmp_k_pe, cmp_k_w1, cmp_k_w2, cmp_v_pe, cmp_v_w1, cmp_v_w2,
                 w_branch_a, w_branch_b, w_out, ffn2_norm, ffn2_w_gate, ffn2_w_up, ffn2_w_down):
    x = x + 0.5 * swiglu(rms_norm(x, ffn1_norm), ffn1_w_gate, ffn1_w_up, ffn1_w_down)
    h = rms_norm(x, mix_norm)
    proj = h @ w_in
    sizes = [A_WIDTH, A_WIDTH, B_WIDTH] + [KV_WIDTH] * 6 + [3 * N_HEADS, D_MODEL, D_MODEL]
    offs = np.cumsum(sizes)[:-1].tolist()
    (u, v, q, kc_raw, vc_raw, ks_raw, vs_raw, kw_raw, vw_raw,
     nsa_gate, gate_a, gate_b) = jnp.split(proj, offs, axis=-1)
    o_a = chunked_gmlp(jax.nn.gelu(u), jax.nn.gelu(v), a_v_norm, a_w_s, a_b_s)
    o_b = native_sparse_attention(q, kc_raw, vc_raw, ks_raw, vs_raw, kw_raw, vw_raw, nsa_gate,
                                  q_norm, k_cmp_norm, k_slc_norm, k_win_norm,
                                  cmp_k_pe, cmp_k_w1, cmp_k_w2, cmp_v_pe, cmp_v_w1, cmp_v_w2)
    merged = jax.nn.sigmoid(gate_a) * (o_a @ w_branch_a) + jax.nn.sigmoid(gate_b) * (o_b @ w_branch_b)
    x = x + merged @ w_out
    x = x + 0.5 * swiglu(rms_norm(x, ffn2_norm), ffn2_w_gate, ffn2_w_up, ffn2_w_down)
    return x


def setup_inputs(seed: int = 0) -> dict:
    key = jax.random.key(seed)
    ks = jax.random.split(key, 32)

    def w(k, shape, scale):
        return jax.random.normal(k, (DEPTH,) + shape, jnp.float32) * scale

    def gain(k, shape):
        return 1.0 + 0.02 * jax.random.normal(k, (DEPTH,) + shape, jnp.float32)

    return {
        "x": jax.random.normal(ks[0], (BATCH, SEQ, D_MODEL), jnp.float32),
        "ffn1_norm": gain(ks[1], (D_MODEL,)),
        "ffn1_w_gate": w(ks[2], (D_MODEL, D_FF), D_MODEL ** -0.5),
        "ffn1_w_up": w(ks[3], (D_MODEL, D_FF), D_MODEL ** -0.5),
        "ffn1_w_down": w(ks[4], (D_FF, D_MODEL), D_FF ** -0.5),
        "mix_norm": gain(ks[5], (D_MODEL,)),
        "w_in": w(ks[6], (D_MODEL, IN_WIDTH), D_MODEL ** -0.5),
        "a_v_norm": gain(ks[7], (A_WIDTH,)),
        "a_w_s": w(ks[8], (A_GROUPS, A_CHUNK, A_CHUNK), 0.5 * A_CHUNK ** -0.5),
        "a_b_s": gain(ks[9], (A_GROUPS, A_CHUNK)),
        "q_norm": gain(ks[10], (HEAD_DIM,)),
        "k_cmp_norm": gain(ks[11], (HEAD_DIM,)),
        "k_slc_norm": gain(ks[12], (HEAD_DIM,)),
        "k_win_norm": gain(ks[13], (HEAD_DIM,)),
        "cmp_k_pe": w(ks[14], (CMP_BLOCK, HEAD_DIM), 0.1),
        "cmp_k_w1": w(ks[15], (CMP_BLOCK * HEAD_DIM, CMP_HIDDEN), (CMP_BLOCK * HEAD_DIM) ** -0.5),
        "cmp_k_w2": w(ks[16], (CMP_HIDDEN, HEAD_DIM), CMP_HIDDEN ** -0.5),
        "cmp_v_pe": w(ks[17], (CMP_BLOCK, HEAD_DIM), 0.1),
        "cmp_v_w1": w(ks[18], (CMP_BLOCK * HEAD_DIM, CMP_HIDDEN), (CMP_BLOCK * HEAD_DIM) ** -0.5),
        "cmp_v_w2": w(ks[19], (CMP_HIDDEN, HEAD_DIM), CMP_HIDDEN ** -0.5),
        "w_branch_a": w(ks[20], (A_WIDTH, D_MODEL), A_WIDTH ** -0.5),
        "w_branch_b": w(ks[21], (B_WIDTH, D_MODEL), B_WIDTH ** -0.5),
        "w_out": w(ks[22], (D_MODEL, D_MODEL), D_MODEL ** -0.5),
        "ffn2_norm": gain(ks[23], (D_MODEL,)),
        "ffn2_w_gate": w(ks[24], (D_MODEL, D_FF), D_MODEL ** -0.5),
        "ffn2_w_up": w(ks[25], (D_MODEL, D_FF), D_MODEL ** -0.5),
        "ffn2_w_down": w(ks[26], (D_FF, D_MODEL), D_FF ** -0.5),
    }


def reference(x, ffn1_norm, ffn1_w_gate, ffn1_w_up, ffn1_w_down, mix_norm, w_in,
              a_v_norm, a_w_s, a_b_s, q_norm, k_cmp_norm, k_slc_norm, k_win_norm,
              cmp_k_pe, cmp_k_w1, cmp_k_w2, cmp_v_pe, cmp_v_w1, cmp_v_w2,
              w_branch_a, w_branch_b, w_out, ffn2_norm, ffn2_w_gate, ffn2_w_up, ffn2_w_down):
    layer_params = (ffn1_norm, ffn1_w_gate, ffn1_w_up, ffn1_w_down, mix_norm, w_in,
                    a_v_norm, a_w_s, a_b_s, q_norm, k_cmp_norm, k_slc_norm, k_win_norm,
                    cmp_k_pe, cmp_k_w1, cmp_k_w2, cmp_v_pe, cmp_v_w1, cmp_v_w2,
                    w_branch_a, w_branch_b, w_out, ffn2_norm, ffn2_w_gate, ffn2_w_up, ffn2_w_down)
    for l in range(DEPTH):
        x = hybrid_layer(x, *[p[l] for p in layer_params])
    return x
```

```python
import functools

import numpy as np
import jax
import jax.numpy as jnp
from jax import lax
from jax.experimental import pallas as pl
from jax.experimental.pallas import tpu as pltpu

D_MODEL = 2048
D_FF = 5504
A_WIDTH = 2048
A_GROUPS = 8
A_CHUNK = 128
N_HEADS = 16
HEAD_DIM = 128
N_KV_GROUPS = 2
HEADS_PER_GROUP = N_HEADS // N_KV_GROUPS
CMP_BLOCK = 32
CMP_STRIDE = 16
CMP_HIDDEN = 256
SEL_BLOCK = 64
SEL_TOP = 16
WINDOW = 512
ROPE_THETA = 500000.0
ROPE_DIM = HEAD_DIM // 4
EPS = 1e-6
NEG = -1e30
FORCE_BONUS = 1e4
B_WIDTH = N_HEADS * HEAD_DIM
KV_WIDTH = N_KV_GROUPS * HEAD_DIM

LANES = 128
D_FF_PAD = 5632
VMEM_LIMIT = 56 * 1024 * 1024

F32 = jnp.float32
BF16 = jnp.bfloat16
NT_DIMS = (((1,), (1,)), ((), ()))


def _params(*sem):
    return pltpu.CompilerParams(dimension_semantics=sem, vmem_limit_bytes=VMEM_LIMIT)


def _rms(x, g):
    ms = jnp.mean(x * x, axis=-1, keepdims=True)
    return x * lax.rsqrt(ms + EPS) * g


def _ffn_kernel(x_ref, g_ref, wg_ref, wu_ref, wd_ref, o_ref, h_ref, acc_ref, *, nf):
    j = pl.program_id(1)

    @pl.when(j == 0)
    def _():
        h_ref[...] = _rms(x_ref[...], g_ref[...]).astype(BF16)
        acc_ref[...] = jnp.zeros_like(acc_ref)

    h = h_ref[...]
    a = jnp.dot(h, wg_ref[...], preferred_element_type=F32)
    b = jnp.dot(h, wu_ref[...], preferred_element_type=F32)
    act = (a * jax.nn.sigmoid(a) * b).astype(BF16)
    acc_ref[...] += jnp.dot(act, wd_ref[...], preferred_element_type=F32)

    @pl.when(j == nf - 1)
    def _():
        o_ref[...] = x_ref[...] + 0.5 * acc_ref[...]


def _ffn(x, norm, wg, wu, wd, *, tm=512, tf=512):
    T, D = x.shape
    F = wg.shape[1]
    nf = F // tf
    return pl.pallas_call(
        functools.partial(_ffn_kernel, nf=nf),
        grid=(T // tm, nf),
        in_specs=[
            pl.BlockSpec((tm, D), lambda i, j: (i, 0)),
            pl.BlockSpec((1, D), lambda i, j: (0, 0)),
            pl.BlockSpec((D, tf), lambda i, j: (0, j)),
            pl.BlockSpec((D, tf), lambda i, j: (0, j)),
            pl.BlockSpec((tf, D), lambda i, j: (j, 0)),
        ],
        out_specs=pl.BlockSpec((tm, D), lambda i, j: (i, 0)),
        out_shape=jax.ShapeDtypeStruct((T, D), F32),
        scratch_shapes=[pltpu.VMEM((tm, D), BF16), pltpu.VMEM((tm, D), F32)],
        compiler_params=_params("parallel", "arbitrary"),
        name="ffn",
    )(x, norm, wg, wu, wd)


def _proj_kernel(x_ref, g_ref, w_ref, o_ref, h_ref):
    @pl.when(pl.program_id(1) == 0)
    def _():
        h_ref[...] = _rms(x_ref[...], g_ref[...]).astype(BF16)

    o_ref[...] = jnp.dot(h_ref[...], w_ref[...], preferred_element_type=F32).astype(o_ref.dtype)


def _proj(x, norm, w, out_dtype, *, tm, tn):
    T, D = x.shape
    N = w.shape[1]
    return pl.pallas_call(
        _proj_kernel,
        grid=(T // tm, N // tn),
        in_specs=[
            pl.BlockSpec((tm, D), lambda i, j: (i, 0)),
            pl.BlockSpec((1, D), lambda i, j: (0, 0)),
            pl.BlockSpec((D, tn), lambda i, j: (0, j)),
        ],
        out_specs=pl.BlockSpec((tm, tn), lambda i, j: (i, j)),
        out_shape=jax.ShapeDtypeStruct((T, N), out_dtype),
        scratch_shapes=[pltpu.VMEM((tm, D), BF16)],
        compiler_params=_params("parallel", "arbitrary"),
        name="proj",
    )(x, norm, w)


def _gmlp_kernel(u_ref, v_ref, vn_ref, ws_ref, bs_ref, o_ref, *, tr):
    gw = A_WIDTH // A_GROUPS
    v = jax.nn.gelu(v_ref[...].astype(F32))
    vn = _rms(v, vn_ref[...]).astype(BF16)
    row = lax.broadcasted_iota(jnp.int32, (A_CHUNK, A_CHUNK), 0)
    col = lax.broadcasted_iota(jnp.int32, (A_CHUNK, A_CHUNK), 1)
    for g in range(A_GROUPS):
        w = jnp.where(col <= row, ws_ref[g], 0.0).astype(BF16)
        bias = bs_ref[:, g:g + 1]
        for c in range(tr // A_CHUNK):
            rows = slice(c * A_CHUNK, (c + 1) * A_CHUNK)
            cols = slice(g * gw, (g + 1) * gw)
            mixed = jnp.dot(w, vn[rows, cols], preferred_element_type=F32) + bias
            u = jax.nn.gelu(u_ref[rows, cols].astype(F32))
            o_ref[rows, cols] = (u * mixed).astype(o_ref.dtype)


def _gmlp(proj_a, a_v_norm, w_s, b_s_t, *, tr=512):
    T = proj_a.shape[0]
    return pl.pallas_call(
        functools.partial(_gmlp_kernel, tr=tr),
        grid=(T // tr,),
        in_specs=[
            pl.BlockSpec((tr, A_WIDTH), lambda i: (i, 0)),
            pl.BlockSpec((tr, A_WIDTH), lambda i: (i, 1)),
            pl.BlockSpec((1, A_WIDTH), lambda i: (0, 0)),
            pl.BlockSpec((A_GROUPS, A_CHUNK, A_CHUNK), lambda i: (0, 0, 0)),
            pl.BlockSpec((A_CHUNK, A_GROUPS), lambda i: (0, 0)),
        ],
        out_specs=pl.BlockSpec((tr, A_WIDTH), lambda i: (i, 0)),
        out_shape=jax.ShapeDtypeStruct((T, A_WIDTH), BF16),
        compiler_params=_params("parallel"),
        name="gmlp",
    )(proj_a, proj_a, a_v_norm, w_s, b_s_t)


def _rope(y, c, su, sd):
    return y * c + pltpu.roll(y, 16, 1) * su + pltpu.roll(y, HEAD_DIM - 16, 1) * sd


def _prep_kernel(q_ref, ks_ref, vs_ref, kw_ref, vw_ref, qg_ref, ksg_ref, kwg_ref,
                 c_ref, su_ref, sd_ref, qn_ref, ksa_ref, vso_ref, kwn_ref, vwo_ref, *, tr, seq):
    c, su, sd = c_ref[...], su_ref[...], sd_ref[...]
    scale = HEAD_DIM ** -0.5
    for h in range(N_HEADS):
        cols = slice(h * HEAD_DIM, (h + 1) * HEAD_DIM)
        y = _rope(_rms(q_ref[:, cols], qg_ref[...]), c, su, sd)
        qn_ref[:, cols] = (y * scale).astype(BF16)
    t0 = (pl.program_id(0) * tr) % seq
    t = t0 + lax.broadcasted_iota(jnp.int32, (tr, LANES), 0)
    lane = lax.broadcasted_iota(jnp.int32, (tr, LANES), 1)
    onehot = jnp.where((t // SEL_BLOCK) == lane, 1.0, 0.0).astype(BF16)
    for g in range(N_KV_GROUPS):
        cols = slice(g * HEAD_DIM, (g + 1) * HEAD_DIM)
        y = _rope(_rms(ks_ref[:, cols], ksg_ref[...]), c, su, sd)
        ksa_ref[:, 2 * g * HEAD_DIM:(2 * g + 1) * HEAD_DIM] = y.astype(BF16)
        ksa_ref[:, (2 * g + 1) * HEAD_DIM:(2 * g + 2) * HEAD_DIM] = onehot
        y = _rope(_rms(kw_ref[:, cols], kwg_ref[...]), c, su, sd)
        kwn_ref[:, cols] = y.astype(BF16)
    vso_ref[...] = vs_ref[...].astype(BF16)
    vwo_ref[...] = vw_ref[...].astype(BF16)


def _prep(proj_b, q_norm, k_slc_norm, k_win_norm, rope_c, rope_su, rope_sd, *, seq, tr=512):
    T = proj_b.shape[0]
    kvb = B_WIDTH // KV_WIDTH
    nrt = seq // tr
    row = lambda i: (i, 0)
    tab = pl.BlockSpec((tr, HEAD_DIM), lambda i: (i % nrt, 0))
    gain = pl.BlockSpec((1, HEAD_DIM), lambda i: (0, 0))
    kv_in = lambda k: pl.BlockSpec((tr, KV_WIDTH), lambda i: (i, kvb + k))
    return pl.pallas_call(
        functools.partial(_prep_kernel, tr=tr, seq=seq),
        grid=(T // tr,),
        in_specs=[pl.BlockSpec((tr, B_WIDTH), row), kv_in(2), kv_in(3), kv_in(4), kv_in(5),
                  gain, gain, gain, tab, tab, tab],
        out_specs=[pl.BlockSpec((tr, B_WIDTH), row), pl.BlockSpec((tr, 2 * KV_WIDTH), row),
                   pl.BlockSpec((tr, KV_WIDTH), row), pl.BlockSpec((tr, KV_WIDTH), row),
                   pl.BlockSpec((tr, KV_WIDTH), row)],
        out_shape=[jax.ShapeDtypeStruct((T, B_WIDTH), BF16),
                   jax.ShapeDtypeStruct((T, 2 * KV_WIDTH), BF16),
                   jax.ShapeDtypeStruct((T, KV_WIDTH), BF16),
                   jax.ShapeDtypeStruct((T, KV_WIDTH), BF16),
                   jax.ShapeDtypeStruct((T, KV_WIDTH), BF16)],
        compiler_params=_params("parallel"),
        name="qk_prep",
    )(proj_b, proj_b, proj_b, proj_b, proj_b, q_norm, k_slc_norm, k_win_norm,
      rope_c, rope_su, rope_sd)


def _compress_kernel(zk_ref, zv_ref, pek_ref, pev_ref, w1k_ref, w2k_ref, w1v_ref, w2v_ref,
                     kg_ref, c_ref, su_ref, sd_ref, kc_ref, vc_ref):
    half = CMP_STRIDE * HEAD_DIM
    nb = zk_ref.shape[0]

    def comp(z_ref, pe_ref, w1_ref, w2_ref):
        z = z_ref[...]
        top = jnp.dot((z + pe_ref[:, :half]).astype(BF16), w1_ref[:half, :], preferred_element_type=F32)
        bot = jnp.dot((z + pe_ref[:, half:]).astype(BF16), w1_ref[half:, :], preferred_element_type=F32)
        hid = top + pltpu.roll(bot, nb - 1, 0)
        return jnp.dot(jax.nn.gelu(hid).astype(BF16), w2_ref[...], preferred_element_type=F32)

    k = comp(zk_ref, pek_ref, w1k_ref, w2k_ref)
    kc_ref[...] = _rope(_rms(k, kg_ref[...]), c_ref[...], su_ref[...], sd_ref[...]).astype(BF16)
    vc_ref[...] = comp(zv_ref, pev_ref, w1v_ref, w2v_ref).astype(BF16)


def _compress(zk, zv, pek, pev, w1k, w2k, w1v, w2v, k_cmp_norm, cmp_c, cmp_su, cmp_sd):
    B, G, nb, zw = zk.shape
    zspec = pl.BlockSpec((None, None, nb, zw), lambda b, g: (b, g, 0, 0))
    ospec = pl.BlockSpec((None, None, nb, HEAD_DIM), lambda b, g: (b, g, 0, 0))
    full = lambda a: pl.BlockSpec(a.shape, lambda b, g: (0,) * a.ndim)
    consts = (pek, pev, w1k, w2k, w1v, w2v, k_cmp_norm, cmp_c, cmp_su, cmp_sd)
    return pl.pallas_call(
        _compress_kernel,
        grid=(B, G),
        in_specs=[zspec, zspec] + [full(a) for a in consts],
        out_specs=[ospec, ospec],
        out_shape=[jax.ShapeDtypeStruct((B, G, nb, HEAD_DIM), BF16)] * 2,
        compiler_params=_params("parallel", "parallel"),
        name="compress",
    )(zk, zv, *consts)


def _cmp_attn_kernel(q_ref, kc_ref, vc_ref, gate_ref, ovt_ref, oc_ref, bias_ref, qs_ref, *, tq):
    hpg = HEADS_PER_GROUP
    nb = kc_ref.shape[0]
    n_sel = ovt_ref.shape[0]
    t0 = pl.program_id(2) * tq
    for h in range(hpg):
        qs_ref[h * tq:(h + 1) * tq, :] = q_ref[:, h * HEAD_DIM:(h + 1) * HEAD_DIM]
    s = lax.dot_general(qs_ref[...], kc_ref[...], NT_DIMS, preferred_element_type=F32)
    s = s.reshape(hpg, tq, nb)
    t = t0 + lax.broadcasted_iota(jnp.int32, (1, tq, nb), 1)
    n = lax.broadcasted_iota(jnp.int32, (1, tq, nb), 2)
    mask = n * CMP_STRIDE + (CMP_BLOCK - 1) <= t
    s = jnp.where(mask, s, NEG)
    e = jnp.exp(s - jnp.max(s, axis=-1, keepdims=True))
    inv = 1.0 / jnp.sum(e, axis=-1, keepdims=True)
    p = jnp.where(mask, e * inv, 0.0)
    o = jnp.dot(p.reshape(hpg * tq, nb).astype(BF16), vc_ref[...], preferred_element_type=F32)
    sig = jax.nn.sigmoid(gate_ref[...])
    for h in range(hpg):
        oc_ref[:, h * HEAD_DIM:(h + 1) * HEAD_DIM] = (
            o[h * tq:(h + 1) * tq] * sig[:, h:h + 1]).astype(oc_ref.dtype)

    psum = jnp.sum(p, axis=0)
    ovt = ovt_ref[...]
    hi = psum.astype(BF16)
    r1 = psum - hi.astype(F32)
    mid = r1.astype(BF16)
    lo = (r1 - mid.astype(F32)).astype(BF16)
    imp = (lax.dot_general(ovt, hi, NT_DIMS, preferred_element_type=F32)
           + lax.dot_general(ovt, mid, NT_DIMS, preferred_element_type=F32)
           + lax.dot_general(ovt, lo, NT_DIMS, preferred_element_type=F32))
    j = lax.broadcasted_iota(jnp.int32, (n_sel, tq), 0)
    tt = t0 + lax.broadcasted_iota(jnp.int32, (n_sel, tq), 1)
    cur = tt // SEL_BLOCK
    forced = jnp.where((j == 0) | (j == cur) | (j == cur - 1), FORCE_BONUS, 0.0)
    score = jnp.where(j * SEL_BLOCK <= tt, imp + forced, NEG)
    rank = jnp.zeros((n_sel, tq), F32)
    for i in range(n_sel):
        r = score[i:i + 1, :]
        ge = jnp.where(r >= score, 1.0, 0.0)
        gt = jnp.where(r > score, 1.0, 0.0)
        rank = rank + jnp.where(j > i, ge, gt)
    bias = jnp.where(rank < SEL_TOP, 0.0, NEG)
    bias = jnp.concatenate([bias, jnp.zeros((LANES - n_sel, tq), F32)], axis=0)
    bias_ref[...] = bias.T.astype(bias_ref.dtype)


def _cmp_attn(qn, kc, vc, proj_b, ovt, *, batch, seq, tq=128):
    T = qn.shape[0]
    G = N_KV_GROUPS
    nq = seq // tq
    gw = HEADS_PER_GROUP * HEAD_DIM
    nb = kc.shape[2]
    gate_blk = (B_WIDTH + 6 * KV_WIDTH) // LANES
    rowg = lambda b, g, i: (b * nq + i, g)
    return pl.pallas_call(
        functools.partial(_cmp_attn_kernel, tq=tq),
        grid=(batch, G, nq),
        in_specs=[
            pl.BlockSpec((tq, gw), rowg),
            pl.BlockSpec((None, None, nb, HEAD_DIM), lambda b, g, i: (b, g, 0, 0)),
            pl.BlockSpec((None, None, nb, HEAD_DIM), lambda b, g, i: (b, g, 0, 0)),
            pl.BlockSpec((tq, LANES), lambda b, g, i: (b * nq + i, gate_blk + g)),
            pl.BlockSpec(ovt.shape, lambda b, g, i: (0, 0)),
        ],
        out_specs=[pl.BlockSpec((tq, gw), rowg), pl.BlockSpec((tq, LANES), rowg)],
        out_shape=[jax.ShapeDtypeStruct((T, B_WIDTH), BF16),
                   jax.ShapeDtypeStruct((T, G * LANES), BF16)],
        scratch_shapes=[pltpu.VMEM((HEADS_PER_GROUP * tq, HEAD_DIM), BF16)],
        compiler_params=_params("parallel", "parallel", "parallel"),
        name="cmp_attn",
    )(qn, kc, vc, proj_b, ovt)


def _sw_attn_kernel(q_ref, bias_ref, ksa_ref, vs_ref, kw_ref, vw_ref, oc_ref, gate_ref, o_ref,
                    qs_ref, m_ref, l_ref, acc_ref, *, tq, tk):
    hpg = HEADS_PER_GROUP
    M = hpg * tq
    t0 = pl.program_id(2) * tq
    bias = bias_ref[...]
    for h in range(hpg):
        qs_ref[h * tq:(h + 1) * tq, :HEAD_DIM] = q_ref[:, h * HEAD_DIM:(h + 1) * HEAD_DIM]
        qs_ref[h * tq:(h + 1) * tq, HEAD_DIM:] = bias
    t = t0 + lax.broadcasted_iota(jnp.int32, (1, tq, tk), 1)
    kpos = lax.broadcasted_iota(jnp.int32, (1, tq, tk), 2)

    def reset():
        m_ref[...] = jnp.full_like(m_ref, NEG)
        l_ref[...] = jnp.zeros_like(l_ref)
        acc_ref[...] = jnp.zeros_like(acc_ref)

    def step(s, v):
        m_old = m_ref[...]
        m_new = jnp.maximum(m_old, jnp.max(s, axis=-1, keepdims=True))
        alpha = jnp.exp(m_old - m_new)
        p = jnp.exp(s - m_new)
        l_ref[...] = alpha * l_ref[...] + jnp.sum(p, axis=-1, keepdims=True)
        acc_ref[...] = alpha * acc_ref[...] + jnp.dot(p.astype(BF16), v, preferred_element_type=F32)
        m_ref[...] = m_new

    def result():
        return acc_ref[...] * (1.0 / l_ref[...])

    def sel_tile(j, causal):
        base = pl.multiple_of(j * tk, tk)
        k = ksa_ref[pl.ds(base, tk), :]
        s = lax.dot_general(qs_ref[...], k, NT_DIMS, preferred_element_type=F32)
        if causal:
            s = jnp.where(kpos + base <= t, s.reshape(hpg, tq, tk), NEG).reshape(M, tk)
        step(s, vs_ref[pl.ds(base, tk), :])

    reset()
    last = (t0 + tq - 1) // tk

    def sel_body(j, carry):
        sel_tile(j, False)
        return carry

    lax.fori_loop(0, last, sel_body, 0)
    sel_tile(last, True)
    o_s = result()

    reset()
    first = jnp.maximum(t0 - (WINDOW - 1), 0) // tk

    def win_body(j, carry):
        base = pl.multiple_of(j * tk, tk)
        k = kw_ref[pl.ds(base, tk), :]
        s = lax.dot_general(qs_ref[:, :HEAD_DIM], k, NT_DIMS, preferred_element_type=F32)
        diff = (t - (kpos + base)).astype(jnp.uint32)
        s = jnp.where(diff < WINDOW, s.reshape(hpg, tq, tk), NEG).reshape(M, tk)
        step(s, vw_ref[pl.ds(base, tk), :])
        return carry

    lax.fori_loop(first, last + 1, win_body, 0)
    o_w = result()

    sig = jax.nn.sigmoid(gate_ref[...])
    for h in range(hpg):
        rows = slice(h * tq, (h + 1) * tq)
        cols = slice(h * HEAD_DIM, (h + 1) * HEAD_DIM)
        o = (oc_ref[:, cols].astype(F32) + sig[:, hpg + h:hpg + h + 1] * o_s[rows]
             + sig[:, 2 * hpg + h:2 * hpg + h + 1] * o_w[rows])
        o_ref[:, cols] = o.astype(o_ref.dtype)


def _sw_attn(qn, bias, ksa, vs, kwn, vw, o_c, proj_b, *, batch, seq, tq=128, tk=256):
    T = qn.shape[0]
    G = N_KV_GROUPS
    nq = seq // tq
    gw = HEADS_PER_GROUP * HEAD_DIM
    M = HEADS_PER_GROUP * tq
    gate_blk = (B_WIDTH + 6 * KV_WIDTH) // LANES
    rowg = lambda b, g, i: (b * nq + i, g)
    seqg = lambda b, g, i: (b, g)
    return pl.pallas_call(
        functools.partial(_sw_attn_kernel, tq=tq, tk=tk),
        grid=(batch, G, nq),
        in_specs=[
            pl.BlockSpec((tq, gw), rowg),
            pl.BlockSpec((tq, LANES), rowg),
            pl.BlockSpec((seq, 2 * HEAD_DIM), seqg),
            pl.BlockSpec((seq, HEAD_DIM), seqg),
            pl.BlockSpec((seq, HEAD_DIM), seqg),
            pl.BlockSpec((seq, HEAD_DIM), seqg),
            pl.BlockSpec((tq, gw), rowg),
            pl.BlockSpec((tq, LANES), lambda b, g, i: (b * nq + i, gate_blk + g)),
        ],
        out_specs=pl.BlockSpec((tq, gw), rowg),
        out_shape=jax.ShapeDtypeStruct((T, B_WIDTH), BF16),
        scratch_shapes=[pltpu.VMEM((M, 2 * HEAD_DIM), BF16), pltpu.VMEM((M, 1), F32),
                        pltpu.VMEM((M, 1), F32), pltpu.VMEM((M, HEAD_DIM), F32)],
        compiler_params=_params("parallel", "parallel", "parallel"),
        name="sw_attn",
    )(qn, bias, ksa, vs, kwn, vw, o_c, proj_b)


def _merge_kernel(oa_ref, ob_ref, ga_ref, gb_ref, wa_ref, wb_ref, o_ref):
    a = jnp.dot(oa_ref[...], wa_ref[...], preferred_element_type=F32)
    b = jnp.dot(ob_ref[...], wb_ref[...], preferred_element_type=F32)
    ga = jax.nn.sigmoid(ga_ref[...].astype(F32))
    gb = jax.nn.sigmoid(gb_ref[...].astype(F32))
    o_ref[...] = (ga * a + gb * b).astype(o_ref.dtype)


def _merge(o_a, o_b, proj_a, wa, wb, *, tm=1024, tn=512):
    T, D = o_a.shape
    N = wa.shape[1]
    nn = N // tn
    ga_blk = 2 * A_WIDTH // tn
    return pl.pallas_call(
        _merge_kernel,
        grid=(T // tm, nn),
        in_specs=[
            pl.BlockSpec((tm, D), lambda i, j: (i, 0)),
            pl.BlockSpec((tm, D), lambda i, j: (i, 0)),
            pl.BlockSpec((tm, tn), lambda i, j: (i, ga_blk + j)),
            pl.BlockSpec((tm, tn), lambda i, j: (i, ga_blk + nn + j)),
            pl.BlockSpec((D, tn), lambda i, j: (0, j)),
            pl.BlockSpec((D, tn), lambda i, j: (0, j)),
        ],
        out_specs=pl.BlockSpec((tm, tn), lambda i, j: (i, j)),
        out_shape=jax.ShapeDtypeStruct((T, N), BF16),
        compiler_params=_params("parallel", "arbitrary"),
        name="merge",
    )(o_a, o_b, proj_a, proj_a, wa, wb)


def _out_kernel(x_ref, m_ref, w_ref, o_ref):
    o_ref[...] = x_ref[...] + jnp.dot(m_ref[...], w_ref[...], preferred_element_type=F32)


def _out_proj(x, merged, w, *, tm=1024, tn=512):
    T, D = merged.shape
    N = w.shape[1]
    return pl.pallas_call(
        _out_kernel,
        grid=(T // tm, N // tn),
        in_specs=[
            pl.BlockSpec((tm, tn), lambda i, j: (i, j)),
            pl.BlockSpec((tm, D), lambda i, j: (i, 0)),
            pl.BlockSpec((D, tn), lambda i, j: (0, j)),
        ],
        out_specs=pl.BlockSpec((tm, tn), lambda i, j: (i, j)),
        out_shape=jax.ShapeDtypeStruct((T, N), F32),
        compiler_params=_params("parallel", "arbitrary"),
        name="out_proj",
    )(x, merged, w)


def _rope_tables(pos):
    half = ROPE_DIM // 2
    inv = ROPE_THETA ** (-2.0 * jnp.arange(half, dtype=F32) / ROPE_DIM)
    ang = pos.astype(F32)[:, None] * inv
    cos, sin = jnp.cos(ang), jnp.sin(ang)
    n = pos.shape[0]
    z16 = jnp.zeros((n, half), F32)
    rest = HEAD_DIM - ROPE_DIM
    c = jnp.concatenate([cos, cos, jnp.ones((n, rest), F32)], axis=1)
    su = jnp.concatenate([z16, sin, jnp.zeros((n, rest), F32)], axis=1)
    sd = jnp.concatenate([-sin, z16, jnp.zeros((n, rest), F32)], axis=1)
    return c, su, sd


def _ffn_weights(wg, wu, wd):
    pad = D_FF_PAD - D_FF
    return (jnp.pad(wg, ((0, 0), (0, pad))).astype(BF16),
            jnp.pad(wu, ((0, 0), (0, pad))).astype(BF16),
            jnp.pad(wd, ((0, pad), (0, 0))).astype(BF16))


def _split_w_in(w_in):
    sizes = [A_WIDTH, A_WIDTH, B_WIDTH] + [KV_WIDTH] * 6 + [3 * N_HEADS, D_MODEL, D_MODEL]
    offs = np.concatenate([[0], np.cumsum(sizes)])
    seg = lambda k: w_in[:, offs[k]:offs[k + 1]]
    w_a = jnp.concatenate([seg(0), seg(1), seg(10), seg(11)], axis=1)
    gates = seg(9)
    gate_blocks = []
    for g in range(N_KV_GROUPS):
        idx = np.array([br * N_HEADS + g * HEADS_PER_GROUP + h
                        for br in range(3) for h in range(HEADS_PER_GROUP)])
        blk = jnp.take(gates, idx, axis=1)
        gate_blocks.append(jnp.pad(blk, ((0, 0), (0, LANES - idx.size))))
    w_b = jnp.concatenate([seg(k) for k in range(2, 9)] + gate_blocks, axis=1)
    return w_a.astype(BF16), w_b.astype(BF16)


def _overlap_t(nb_pad, n_sel):
    cmp_start = np.arange(nb_pad) * CMP_STRIDE
    sel_start = np.arange(n_sel) * SEL_BLOCK
    ov = ((cmp_start[None, :] < sel_start[:, None] + SEL_BLOCK)
          & (cmp_start[None, :] + CMP_BLOCK > sel_start[:, None]))
    return jnp.asarray(ov, dtype=BF16)


def _layer(x, ffn1_norm, ffn1_w_gate, ffn1_w_up, ffn1_w_down, mix_norm, w_in,
           a_v_norm, a_w_s, a_b_s, q_norm, k_cmp_norm, k_slc_norm, k_win_norm,
           cmp_k_pe, cmp_k_w1, cmp_k_w2, cmp_v_pe, cmp_v_w1, cmp_v_w2,
           w_branch_a, w_branch_b, w_out, ffn2_norm, ffn2_w_gate, ffn2_w_up, ffn2_w_down):
    B, S, D = x.shape
    T = B * S
    G = N_KV_GROUPS
    row = lambda v: v.reshape(1, -1)
    x0 = x.reshape(T, D)

    x1 = _ffn(x0, row(ffn1_norm), *_ffn_weights(ffn1_w_gate, ffn1_w_up, ffn1_w_down))

    w_a, w_b = _split_w_in(w_in)
    proj_a = _proj(x1, row(mix_norm), w_a, BF16, tm=1024, tn=1024)
    proj_b = _proj(x1, row(mix_norm), w_b, F32, tm=1024, tn=768)

    o_a = _gmlp(proj_a, row(a_v_norm), a_w_s, a_b_s.T)

    pos = jnp.arange(S)
    qn, ksa, vs, kwn, vw = _prep(proj_b, row(q_norm), row(k_slc_norm), row(k_win_norm),
                                 *_rope_tables(pos), seq=S)

    nb_pad = S // CMP_STRIDE
    zw = CMP_STRIDE * HEAD_DIM

    def strides(k):
        z = proj_b[:, B_WIDTH + k * KV_WIDTH:B_WIDTH + (k + 1) * KV_WIDTH]
        return z.reshape(B, S, G, HEAD_DIM).transpose(0, 2, 1, 3).reshape(B, G, nb_pad, zw)

    cmp_end = jnp.arange(nb_pad) * CMP_STRIDE + (CMP_BLOCK - 1)
    kc, vc = _compress(strides(0), strides(1), cmp_k_pe.reshape(1, -1), cmp_v_pe.reshape(1, -1),
                       cmp_k_w1.astype(BF16), cmp_k_w2.astype(BF16),
                       cmp_v_w1.astype(BF16), cmp_v_w2.astype(BF16),
                       row(k_cmp_norm), *_rope_tables(cmp_end))

    o_c, bias = _cmp_attn(qn, kc, vc, proj_b, _overlap_t(nb_pad, S // SEL_BLOCK), batch=B, seq=S)
    o_b = _sw_attn(qn, bias, ksa, vs, kwn, vw, o_c, proj_b, batch=B, seq=S)

    merged = _merge(o_a, o_b, proj_a, w_branch_a.astype(BF16), w_branch_b.astype(BF16))
    x2 = _out_proj(x1, merged, w_out.astype(BF16))

    x3 = _ffn(x2, row(ffn2_norm), *_ffn_weights(ffn2_w_gate, ffn2_w_up, ffn2_w_down))
    return x3.reshape(B, S, D)


def kernel(x, ffn1_norm, ffn1_w_gate, ffn1_w_up, ffn1_w_down, mix_norm, w_in, a_v_norm, a_w_s, a_b_s, q_norm, k_cmp_norm, k_slc_norm, k_win_norm, cmp_k_pe, cmp_k_w1, cmp_k_w2, cmp_v_pe, cmp_v_w1, cmp_v_w2, w_branch_a, w_branch_b, w_out, ffn2_norm, ffn2_w_gate, ffn2_w_up, ffn2_w_down):
    params = (ffn1_norm, ffn1_w_gate, ffn1_w_up, ffn1_w_down, mix_norm, w_in,
              a_v_norm, a_w_s, a_b_s, q_norm, k_cmp_norm, k_slc_norm, k_win_norm,
              cmp_k_pe, cmp_k_w1, cmp_k_w2, cmp_v_pe, cmp_v_w1, cmp_v_w2,
              w_branch_a, w_branch_b, w_out, ffn2_norm, ffn2_w_gate, ffn2_w_up, ffn2_w_down)
    for l in range(params[0].shape[0]):
        x = _layer(x, *[p[l] for p in params])
    return x
```

```python
import functools

import numpy as np
import jax
import jax.numpy as jnp
from jax import lax
from jax.experimental import pallas as pl
from jax.experimental.pallas import tpu as pltpu

D_MODEL = 2048
D_FF = 5504
A_WIDTH = 2048
A_GROUPS = 8
A_CHUNK = 128
N_HEADS = 16
HEAD_DIM = 128
N_KV_GROUPS = 2
HEADS_PER_GROUP = N_HEADS // N_KV_GROUPS
CMP_BLOCK = 32
CMP_STRIDE = 16
CMP_HIDDEN = 256
SEL_BLOCK = 64
SEL_TOP = 16
WINDOW = 512
ROPE_THETA = 500000.0
ROPE_DIM = HEAD_DIM // 4
EPS = 1e-6
NEG = -1e30
FORCE_BONUS = 1e4
B_WIDTH = N_HEADS * HEAD_DIM
KV_WIDTH = N_KV_GROUPS * HEAD_DIM

LANES = 128
D_FF_PAD = 5632
VMEM_LIMIT = 56 * 1024 * 1024

F32 = jnp.float32
BF16 = jnp.bfloat16
NT_DIMS = (((1,), (1,)), ((), ()))


def _params(*sem):
    return pltpu.CompilerParams(dimension_semantics=sem, vmem_limit_bytes=VMEM_LIMIT)


def _rms(x, g):
    ms = jnp.mean(x * x, axis=-1, keepdims=True)
    return x * lax.rsqrt(ms + EPS) * g


def _ffn_kernel(x_ref, g_ref, wg_ref, wu_ref, wd_ref, o_ref, h_ref, acc_ref, *, nf):
    j = pl.program_id(1)

    @pl.when(j == 0)
    def _():
        h_ref[...] = _rms(x_ref[...], g_ref[...]).astype(BF16)
        acc_ref[...] = jnp.zeros_like(acc_ref)

    h = h_ref[...]
    a = jnp.dot(h, wg_ref[...], preferred_element_type=F32)
    b = jnp.dot(h, wu_ref[...], preferred_element_type=F32)
    act = (a * jax.nn.sigmoid(a) * b).astype(BF16)
    acc_ref[...] += jnp.dot(act, wd_ref[...], preferred_element_type=F32)

    @pl.when(j == nf - 1)
    def _():
        o_ref[...] = x_ref[...] + 0.5 * acc_ref[...]


def _ffn(x, norm, wg, wu, wd, *, tm=512, tf=512):
    T, D = x.shape
    F = wg.shape[1]
    nf = F // tf
    return pl.pallas_call(
        functools.partial(_ffn_kernel, nf=nf),
        grid=(T // tm, nf),
        in_specs=[
            pl.BlockSpec((tm, D), lambda i, j: (i, 0)),
            pl.BlockSpec((1, D), lambda i, j: (0, 0)),
            pl.BlockSpec((D, tf), lambda i, j: (0, j)),
            pl.BlockSpec((D, tf), lambda i, j: (0, j)),
            pl.BlockSpec((tf, D), lambda i, j: (j, 0)),
        ],
        out_specs=pl.BlockSpec((tm, D), lambda i, j: (i, 0)),
        out_shape=jax.ShapeDtypeStruct((T, D), F32),
        scratch_shapes=[pltpu.VMEM((tm, D), BF16), pltpu.VMEM((tm, D), F32)],
        compiler_params=_params("parallel", "arbitrary"),
        name="ffn",
    )(x, norm, wg, wu, wd)


def _proj_kernel(x_ref, g_ref, w_ref, o_ref, h_ref):
    @pl.when(pl.program_id(1) == 0)
    def _():
        h_ref[...] = _rms(x_ref[...], g_ref[...]).astype(BF16)

    o_ref[...] = jnp.dot(h_ref[...], w_ref[...], preferred_element_type=F32).astype(o_ref.dtype)


def _proj(x, norm, w, out_dtype, *, tm, tn):
    T, D = x.shape
    N = w.shape[1]
    return pl.pallas_call(
        _proj_kernel,
        grid=(T // tm, N // tn),
        in_specs=[
            pl.BlockSpec((tm, D), lambda i, j: (i, 0)),
            pl.BlockSpec((1, D), lambda i, j: (0, 0)),
            pl.BlockSpec((D, tn), lambda i, j: (0, j)),
        ],
        out_specs=pl.BlockSpec((tm, tn), lambda i, j: (i, j)),
        out_shape=jax.ShapeDtypeStruct((T, N), out_dtype),
        scratch_shapes=[pltpu.VMEM((tm, D), BF16)],
        compiler_params=_params("parallel", "arbitrary"),
        name="proj",
    )(x, norm, w)


def _gmlp_kernel(u_ref, v_ref, vn_ref, ws_ref, bs_ref, o_ref, *, tr):
    gw = A_WIDTH // A_GROUPS
    v = jax.nn.gelu(v_ref[...].astype(F32))
    vn = _rms(v, vn_ref[...]).astype(BF16)
    row = lax.broadcasted_iota(jnp.int32, (A_CHUNK, A_CHUNK), 0)
    col = lax.broadcasted_iota(jnp.int32, (A_CHUNK, A_CHUNK), 1)
    for g in range(A_GROUPS):
        w = jnp.where(col <= row, ws_ref[g], 0.0).astype(BF16)
        bias = bs_ref[:, g:g + 1]
        for c in range(tr // A_CHUNK):
            rows = slice(c * A_CHUNK, (c + 1) * A_CHUNK)
            cols = slice(g * gw, (g + 1) * gw)
            mixed = jnp.dot(w, vn[rows, cols], preferred_element_type=F32) + bias
            u = jax.nn.gelu(u_ref[rows, cols].astype(F32))
            o_ref[rows, cols] = (u * mixed).astype(o_ref.dtype)


def _gmlp(proj_a, a_v_norm, w_s, b_s_t, *, tr=512):
    T = proj_a.shape[0]
    return pl.pallas_call(
        functools.partial(_gmlp_kernel, tr=tr),
        grid=(T // tr,),
        in_specs=[
            pl.BlockSpec((tr, A_WIDTH), lambda i: (i, 0)),
            pl.BlockSpec((tr, A_WIDTH), lambda i: (i, 1)),
            pl.BlockSpec((1, A_WIDTH), lambda i: (0, 0)),
            pl.BlockSpec((A_GROUPS, A_CHUNK, A_CHUNK), lambda i: (0, 0, 0)),
            pl.BlockSpec((A_CHUNK, A_GROUPS), lambda i: (0, 0)),
        ],
        out_specs=pl.BlockSpec((tr, A_WIDTH), lambda i: (i, 0)),
        out_shape=jax.ShapeDtypeStruct((T, A_WIDTH), BF16),
        compiler_params=_params("parallel"),
        name="gmlp",
    )(proj_a, proj_a, a_v_norm, w_s, b_s_t)


def _rope(y, c, su, sd):
    return y * c + pltpu.roll(y, 16, 1) * su + pltpu.roll(y, HEAD_DIM - 16, 1) * sd


def _prep_kernel(q_ref, ks_ref, vs_ref, kw_ref, vw_ref, qg_ref, ksg_ref, kwg_ref,
                 c_ref, su_ref, sd_ref, qn_ref, ksa_ref, vso_ref, kwn_ref, vwo_ref, *, tr, seq):
    c, su, sd = c_ref[...], su_ref[...], sd_ref[...]
    scale = HEAD_DIM ** -0.5
    for h in range(N_HEADS):
        cols = slice(h * HEAD_DIM, (h + 1) * HEAD_DIM)
        y = _rope(_rms(q_ref[:, cols], qg_ref[...]), c, su, sd)
        qn_ref[:, cols] = (y * scale).astype(BF16)
    t0 = (pl.program_id(0) * tr) % seq
    t = t0 + lax.broadcasted_iota(jnp.int32, (tr, LANES), 0)
    lane = lax.broadcasted_iota(jnp.int32, (tr, LANES), 1)
    onehot = jnp.where((t // SEL_BLOCK) == lane, 1.0, 0.0).astype(BF16)
    ones = jnp.ones((tr, HEAD_DIM), BF16)
    for g in range(N_KV_GROUPS):
        cols = slice(g * HEAD_DIM, (g + 1) * HEAD_DIM)
        lo = slice(2 * g * HEAD_DIM, (2 * g + 1) * HEAD_DIM)
        hi = slice((2 * g + 1) * HEAD_DIM, (2 * g + 2) * HEAD_DIM)
        y = _rope(_rms(ks_ref[:, cols], ksg_ref[...]), c, su, sd)
        ksa_ref[:, lo] = y.astype(BF16)
        ksa_ref[:, hi] = onehot
        y = _rope(_rms(kw_ref[:, cols], kwg_ref[...]), c, su, sd)
        kwn_ref[:, cols] = y.astype(BF16)
        vso_ref[:, lo] = vs_ref[:, cols].astype(BF16)
        vso_ref[:, hi] = ones
        vwo_ref[:, lo] = vw_ref[:, cols].astype(BF16)
        vwo_ref[:, hi] = ones


def _prep(proj_b, q_norm, k_slc_norm, k_win_norm, rope_c, rope_su, rope_sd, *, seq, tr=512):
    T = proj_b.shape[0]
    kvb = B_WIDTH // KV_WIDTH
    nrt = seq // tr
    row = lambda i: (i, 0)
    tab = pl.BlockSpec((tr, HEAD_DIM), lambda i: (i % nrt, 0))
    gain = pl.BlockSpec((1, HEAD_DIM), lambda i: (0, 0))
    kv_in = lambda k: pl.BlockSpec((tr, KV_WIDTH), lambda i: (i, kvb + k))
    return pl.pallas_call(
        functools.partial(_prep_kernel, tr=tr, seq=seq),
        grid=(T // tr,),
        in_specs=[pl.BlockSpec((tr, B_WIDTH), row), kv_in(2), kv_in(3), kv_in(4), kv_in(5),
                  gain, gain, gain, tab, tab, tab],
        out_specs=[pl.BlockSpec((tr, B_WIDTH), row), pl.BlockSpec((tr, 2 * KV_WIDTH), row),
                   pl.BlockSpec((tr, 2 * KV_WIDTH), row), pl.BlockSpec((tr, KV_WIDTH), row),
                   pl.BlockSpec((tr, 2 * KV_WIDTH), row)],
        out_shape=[jax.ShapeDtypeStruct((T, B_WIDTH), BF16),
                   jax.ShapeDtypeStruct((T, 2 * KV_WIDTH), BF16),
                   jax.ShapeDtypeStruct((T, 2 * KV_WIDTH), BF16),
                   jax.ShapeDtypeStruct((T, KV_WIDTH), BF16),
                   jax.ShapeDtypeStruct((T, 2 * KV_WIDTH), BF16)],
        compiler_params=_params("parallel"),
        name="qk_prep",
    )(proj_b, proj_b, proj_b, proj_b, proj_b, q_norm, k_slc_norm, k_win_norm,
      rope_c, rope_su, rope_sd)


def _compress_kernel(zk_ref, zv_ref, pek_ref, pev_ref, w1k_ref, w2k_ref, w1v_ref, w2v_ref,
                     kg_ref, c_ref, su_ref, sd_ref, kc_ref, vc_ref):
    half = CMP_STRIDE * HEAD_DIM
    nb = zk_ref.shape[0]

    def comp(z_ref, pe_ref, w1_ref, w2_ref):
        z = z_ref[...]
        top = jnp.dot((z + pe_ref[:, :half]).astype(BF16), w1_ref[:half, :], preferred_element_type=F32)
        bot = jnp.dot((z + pe_ref[:, half:]).astype(BF16), w1_ref[half:, :], preferred_element_type=F32)
        hid = top + pltpu.roll(bot, nb - 1, 0)
        return jnp.dot(jax.nn.gelu(hid).astype(BF16), w2_ref[...], preferred_element_type=F32)

    k = comp(zk_ref, pek_ref, w1k_ref, w2k_ref)
    kc_ref[...] = _rope(_rms(k, kg_ref[...]), c_ref[...], su_ref[...], sd_ref[...]).astype(BF16)
    vc_ref[...] = comp(zv_ref, pev_ref, w1v_ref, w2v_ref).astype(BF16)


def _compress(zk, zv, pek, pev, w1k, w2k, w1v, w2v, k_cmp_norm, cmp_c, cmp_su, cmp_sd):
    B, G, nb, zw = zk.shape
    zspec = pl.BlockSpec((None, None, nb, zw), lambda b, g: (b, g, 0, 0))
    ospec = pl.BlockSpec((None, None, nb, HEAD_DIM), lambda b, g: (b, g, 0, 0))
    full = lambda a: pl.BlockSpec(a.shape, lambda b, g: (0,) * a.ndim)
    consts = (pek, pev, w1k, w2k, w1v, w2v, k_cmp_norm, cmp_c, cmp_su, cmp_sd)
    return pl.pallas_call(
        _compress_kernel,
        grid=(B, G),
        in_specs=[zspec, zspec] + [full(a) for a in consts],
        out_specs=[ospec, ospec],
        out_shape=[jax.ShapeDtypeStruct((B, G, nb, HEAD_DIM), BF16)] * 2,
        compiler_params=_params("parallel", "parallel"),
        name="compress",
    )(zk, zv, *consts)


def _cmp_attn_kernel(q_ref, kc_ref, vc_ref, gate_ref, ovt_ref, oc_ref, bias_ref, qs_ref, *, tq):
    hpg = HEADS_PER_GROUP
    nb = kc_ref.shape[0]
    n_sel = ovt_ref.shape[0]
    t0 = pl.program_id(2) * tq
    for h in range(hpg):
        qs_ref[h * tq:(h + 1) * tq, :] = q_ref[:, h * HEAD_DIM:(h + 1) * HEAD_DIM]
    s = lax.dot_general(qs_ref[...], kc_ref[...], NT_DIMS, preferred_element_type=F32)
    s = s.reshape(hpg, tq, nb)
    t = t0 + lax.broadcasted_iota(jnp.int32, (1, tq, nb), 1)
    n = lax.broadcasted_iota(jnp.int32, (1, tq, nb), 2)
    mask = n * CMP_STRIDE + (CMP_BLOCK - 1) <= t
    s = jnp.where(mask, s, NEG)
    e = jnp.exp(s - jnp.max(s, axis=-1, keepdims=True))
    inv = 1.0 / jnp.sum(e, axis=-1, keepdims=True)
    p = jnp.where(mask, e * inv, 0.0)
    o = jnp.dot(p.reshape(hpg * tq, nb).astype(BF16), vc_ref[...], preferred_element_type=F32)
    sig = jax.nn.sigmoid(gate_ref[...])
    for h in range(hpg):
        oc_ref[:, h * HEAD_DIM:(h + 1) * HEAD_DIM] = (
            o[h * tq:(h + 1) * tq] * sig[:, h:h + 1]).astype(oc_ref.dtype)

    psum = jnp.sum(p, axis=0)
    ovt = ovt_ref[...]
    hi = psum.astype(BF16)
    r1 = psum - hi.astype(F32)
    mid = r1.astype(BF16)
    lo = (r1 - mid.astype(F32)).astype(BF16)
    imp = (lax.dot_general(ovt, hi, NT_DIMS, preferred_element_type=F32)
           + lax.dot_general(ovt, mid, NT_DIMS, preferred_element_type=F32)
           + lax.dot_general(ovt, lo, NT_DIMS, preferred_element_type=F32))
    j = lax.broadcasted_iota(jnp.int32, (n_sel, tq), 0)
    tt = t0 + lax.broadcasted_iota(jnp.int32, (n_sel, tq), 1)
    cur = tt // SEL_BLOCK
    forced = jnp.where((j == 0) | (j == cur) | (j == cur - 1), FORCE_BONUS, 0.0)
    score = jnp.where(j * SEL_BLOCK <= tt, imp + forced, NEG)
    rank = jnp.zeros((n_sel, tq), F32)
    for i in range(n_sel):
        r = score[i:i + 1, :]
        ge = jnp.where(r >= score, 1.0, 0.0)
        gt = jnp.where(r > score, 1.0, 0.0)
        rank = rank + jnp.where(j > i, ge, gt)
    bias = jnp.where(rank < SEL_TOP, 0.0, NEG)
    bias = jnp.concatenate([bias, jnp.zeros((LANES - n_sel, tq), F32)], axis=0)
    bias_ref[...] = bias.T.astype(bias_ref.dtype)


def _cmp_attn(qn, kc, vc, proj_b, ovt, *, batch, seq, tq=128):
    T = qn.shape[0]
    G = N_KV_GROUPS
    nq = seq // tq
    gw = HEADS_PER_GROUP * HEAD_DIM
    nb = kc.shape[2]
    gate_blk = (B_WIDTH + 6 * KV_WIDTH) // LANES
    rowg = lambda b, g, i: (b * nq + i, g)
    return pl.pallas_call(
        functools.partial(_cmp_attn_kernel, tq=tq),
        grid=(batch, G, nq),
        in_specs=[
            pl.BlockSpec((tq, gw), rowg),
            pl.BlockSpec((None, None, nb, HEAD_DIM), lambda b, g, i: (b, g, 0, 0)),
            pl.BlockSpec((None, None, nb, HEAD_DIM), lambda b, g, i: (b, g, 0, 0)),
            pl.BlockSpec((tq, LANES), lambda b, g, i: (b * nq + i, gate_blk + g)),
            pl.BlockSpec(ovt.shape, lambda b, g, i: (0, 0)),
        ],
        out_specs=[pl.BlockSpec((tq, gw), rowg), pl.BlockSpec((tq, LANES), rowg)],
        out_shape=[jax.ShapeDtypeStruct((T, B_WIDTH), BF16),
                   jax.ShapeDtypeStruct((T, G * LANES), BF16)],
        scratch_shapes=[pltpu.VMEM((HEADS_PER_GROUP * tq, HEAD_DIM), BF16)],
        compiler_params=_params("parallel", "parallel", "parallel"),
        name="cmp_attn",
    )(qn, kc, vc, proj_b, ovt)


def _sw_attn_kernel(q_ref, bias_ref, ksa_ref, vs_ref, kw_ref, vw_ref, oc_ref, gate_ref, o_ref,
                    qs_ref, m_ref, acc_ref, *, tq, tk):
    hpg = HEADS_PER_GROUP
    M = hpg * tq
    t0 = pl.program_id(2) * tq
    bias = bias_ref[...]
    for h in range(hpg):
        qs_ref[h * tq:(h + 1) * tq, :HEAD_DIM] = q_ref[:, h * HEAD_DIM:(h + 1) * HEAD_DIM]
        qs_ref[h * tq:(h + 1) * tq, HEAD_DIM:] = bias
    t = t0 + lax.broadcasted_iota(jnp.int32, (1, tq, tk), 1)
    kpos = lax.broadcasted_iota(jnp.int32, (1, tq, tk), 2)

    def reset():
        m_ref[...] = jnp.full_like(m_ref, NEG)
        acc_ref[...] = jnp.zeros_like(acc_ref)

    def step(s, v):
        m_old = m_ref[...]
        m_new = jnp.maximum(m_old, jnp.max(s, axis=-1, keepdims=True))
        alpha = jnp.exp(m_old - m_new)
        p = jnp.exp(s - jnp.tile(m_new, (1, tk // LANES)))
        acc_ref[...] = (jnp.tile(alpha, (1, 2)) * acc_ref[...]
                        + jnp.dot(p.astype(BF16), v, preferred_element_type=F32))
        m_ref[...] = m_new

    def result():
        return acc_ref[:, :HEAD_DIM] * (1.0 / acc_ref[:, HEAD_DIM:])

    def sel_tile(j, causal):
        base = pl.multiple_of(j * tk, tk)
        k = ksa_ref[pl.ds(base, tk), :]
        s = lax.dot_general(qs_ref[...], k, NT_DIMS, preferred_element_type=F32)
        if causal:
            s = jnp.where(kpos + base <= t, s.reshape(hpg, tq, tk), NEG).reshape(M, tk)
        step(s, vs_ref[pl.ds(base, tk), :])

    reset()
    last = (t0 + tq - 1) // tk

    def sel_body(j, carry):
        sel_tile(j, False)
        return carry

    lax.fori_loop(0, last, sel_body, 0)
    sel_tile(last, True)
    o_s = result()

    reset()
    first = jnp.maximum(t0 - (WINDOW - 1), 0) // tk

    def win_body(j, carry):
        base = pl.multiple_of(j * tk, tk)
        k = kw_ref[pl.ds(base, tk), :]
        s = lax.dot_general(qs_ref[:, :HEAD_DIM], k, NT_DIMS, preferred_element_type=F32)
        diff = (t - (kpos + base)).astype(jnp.uint32)
        s = jnp.where(diff < WINDOW, s.reshape(hpg, tq, tk), NEG).reshape(M, tk)
        step(s, vw_ref[pl.ds(base, tk), :])
        return carry

    lax.fori_loop(first, last + 1, win_body, 0)
    o_w = result()

    sig = jax.nn.sigmoid(gate_ref[...])
    for h in range(hpg):
        rows = slice(h * tq, (h + 1) * tq)
        cols = slice(h * HEAD_DIM, (h + 1) * HEAD_DIM)
        o = (oc_ref[:, cols].astype(F32) + sig[:, hpg + h:hpg + h + 1] * o_s[rows]
             + sig[:, 2 * hpg + h:2 * hpg + h + 1] * o_w[rows])
        o_ref[:, cols] = o.astype(o_ref.dtype)


def _sw_attn(qn, bias, ksa, vs, kwn, vw, o_c, proj_b, *, batch, seq, tq=128, tk=256):
    T = qn.shape[0]
    G = N_KV_GROUPS
    nq = seq // tq
    gw = HEADS_PER_GROUP * HEAD_DIM
    M = HEADS_PER_GROUP * tq
    gate_blk = (B_WIDTH + 6 * KV_WIDTH) // LANES
    rowg = lambda b, g, i: (b * nq + i, g)
    seqg = lambda b, g, i: (b, g)
    return pl.pallas_call(
        functools.partial(_sw_attn_kernel, tq=tq, tk=tk),
        grid=(batch, G, nq),
        in_specs=[
            pl.BlockSpec((tq, gw), rowg),
            pl.BlockSpec((tq, LANES), rowg),
            pl.BlockSpec((seq, 2 * HEAD_DIM), seqg),
            pl.BlockSpec((seq, 2 * HEAD_DIM), seqg),
            pl.BlockSpec((seq, HEAD_DIM), seqg),
            pl.BlockSpec((seq, 2 * HEAD_DIM), seqg),
            pl.BlockSpec((tq, gw), rowg),
            pl.BlockSpec((tq, LANES), lambda b, g, i: (b * nq + i, gate_blk + g)),
        ],
        out_specs=pl.BlockSpec((tq, gw), rowg),
        out_shape=jax.ShapeDtypeStruct((T, B_WIDTH), BF16),
        scratch_shapes=[pltpu.VMEM((M, 2 * HEAD_DIM), BF16), pltpu.VMEM((M, LANES), F32),
                        pltpu.VMEM((M, 2 * HEAD_DIM), F32)],
        compiler_params=_params("parallel", "parallel", "parallel"),
        name="sw_attn",
    )(qn, bias, ksa, vs, kwn, vw, o_c, proj_b)


def _merge_kernel(oa_ref, ob_ref, ga_ref, gb_ref, wa_ref, wb_ref, o_ref):
    a = jnp.dot(oa_ref[...], wa_ref[...], preferred_element_type=F32)
    b = jnp.dot(ob_ref[...], wb_ref[...], preferred_element_type=F32)
    ga = jax.nn.sigmoid(ga_ref[...].astype(F32))
    gb = jax.nn.sigmoid(gb_ref[...].astype(F32))
    o_ref[...] = (ga * a + gb * b).astype(o_ref.dtype)


def _merge(o_a, o_b, proj_a, wa, wb, *, tm=1024, tn=512):
    T, D = o_a.shape
    N = wa.shape[1]
    nn = N // tn
    ga_blk = 2 * A_WIDTH // tn
    return pl.pallas_call(
        _merge_kernel,
        grid=(T // tm, nn),
        in_specs=[
            pl.BlockSpec((tm, D), lambda i, j: (i, 0)),
            pl.BlockSpec((tm, D), lambda i, j: (i, 0)),
            pl.BlockSpec((tm, tn), lambda i, j: (i, ga_blk + j)),
            pl.BlockSpec((tm, tn), lambda i, j: (i, ga_blk + nn + j)),
            pl.BlockSpec((D, tn), lambda i, j: (0, j)),
            pl.BlockSpec((D, tn), lambda i, j: (0, j)),
        ],
        out_specs=pl.BlockSpec((tm, tn), lambda i, j: (i, j)),
        out_shape=jax.ShapeDtypeStruct((T, N), BF16),
        compiler_params=_params("parallel", "arbitrary"),
        name="merge",
    )(o_a, o_b, proj_a, proj_a, wa, wb)


def _out_kernel(x_ref, m_ref, w_ref, o_ref):
    o_ref[...] = x_ref[...] + jnp.dot(m_ref[...], w_ref[...], preferred_element_type=F32)


def _out_proj(x, merged, w, *, tm=1024, tn=512):
    T, D = merged.shape
    N = w.shape[1]
    return pl.pallas_call(
        _out_kernel,
        grid=(T // tm, N // tn),
        in_specs=[
            pl.BlockSpec((tm, tn), lambda i, j: (i, j)),
            pl.BlockSpec((tm, D), lambda i, j: (i, 0)),
            pl.BlockSpec((D, tn), lambda i, j: (0, j)),
        ],
        out_specs=pl.BlockSpec((tm, tn), lambda i, j: (i, j)),
        out_shape=jax.ShapeDtypeStruct((T, N), F32),
        compiler_params=_params("parallel", "arbitrary"),
        name="out_proj",
    )(x, merged, w)


def _rope_tables(pos):
    half = ROPE_DIM // 2
    inv = ROPE_THETA ** (-2.0 * jnp.arange(half, dtype=F32) / ROPE_DIM)
    ang = pos.astype(F32)[:, None] * inv
    cos, sin = jnp.cos(ang), jnp.sin(ang)
    n = pos.shape[0]
    z16 = jnp.zeros((n, half), F32)
    rest = HEAD_DIM - ROPE_DIM
    c = jnp.concatenate([cos, cos, jnp.ones((n, rest), F32)], axis=1)
    su = jnp.concatenate([z16, sin, jnp.zeros((n, rest), F32)], axis=1)
    sd = jnp.concatenate([-sin, z16, jnp.zeros((n, rest), F32)], axis=1)
    return c, su, sd


def _ffn_weights(wg, wu, wd):
    pad = D_FF_PAD - D_FF
    return (jnp.pad(wg, ((0, 0), (0, pad))).astype(BF16),
            jnp.pad(wu, ((0, 0), (0, pad))).astype(BF16),
            jnp.pad(wd, ((0, pad), (0, 0))).astype(BF16))


def _split_w_in(w_in):
    sizes = [A_WIDTH, A_WIDTH, B_WIDTH] + [KV_WIDTH] * 6 + [3 * N_HEADS, D_MODEL, D_MODEL]
    offs = np.concatenate([[0], np.cumsum(sizes)])
    seg = lambda k: w_in[:, offs[k]:offs[k + 1]]
    w_a = jnp.concatenate([seg(0), seg(1), seg(10), seg(11)], axis=1)
    gates = seg(9)
    gate_blocks = []
    for g in range(N_KV_GROUPS):
        idx = np.array([br * N_HEADS + g * HEADS_PER_GROUP + h
                        for br in range(3) for h in range(HEADS_PER_GROUP)])
        blk = jnp.take(gates, idx, axis=1)
        gate_blocks.append(jnp.pad(blk, ((0, 0), (0, LANES - idx.size))))
    w_b = jnp.concatenate([seg(k) for k in range(2, 9)] + gate_blocks, axis=1)
    return w_a.astype(BF16), w_b.astype(BF16)


def _overlap_t(nb_pad, n_sel):
    cmp_start = np.arange(nb_pad) * CMP_STRIDE
    sel_start = np.arange(n_sel) * SEL_BLOCK
    ov = ((cmp_start[None, :] < sel_start[:, None] + SEL_BLOCK)
          & (cmp_start[None, :] + CMP_BLOCK > sel_start[:, None]))
    return jnp.asarray(ov, dtype=BF16)


def _layer(x, ffn1_norm, ffn1_w_gate, ffn1_w_up, ffn1_w_down, mix_norm, w_in,
           a_v_norm, a_w_s, a_b_s, q_norm, k_cmp_norm, k_slc_norm, k_win_norm,
           cmp_k_pe, cmp_k_w1, cmp_k_w2, cmp_v_pe, cmp_v_w1, cmp_v_w2,
           w_branch_a, w_branch_b, w_out, ffn2_norm, ffn2_w_gate, ffn2_w_up, ffn2_w_down):
    B, S, D = x.shape
    T = B * S
    G = N_KV_GROUPS
    row = lambda v: v.reshape(1, -1)
    x0 = x.reshape(T, D)

    x1 = _ffn(x0, row(ffn1_norm), *_ffn_weights(ffn1_w_gate, ffn1_w_up, ffn1_w_down))

    w_a, w_b = _split_w_in(w_in)
    proj_a = _proj(x1, row(mix_norm), w_a, BF16, tm=1024, tn=1024)
    proj_b = _proj(x1, row(mix_norm), w_b, F32, tm=1024, tn=768)

    o_a = _gmlp(proj_a, row(a_v_norm), a_w_s, a_b_s.T)

    pos = jnp.arange(S)
    qn, ksa, vs, kwn, vw = _prep(proj_b, row(q_norm), row(k_slc_norm), row(k_win_norm),
                                 *_rope_tables(pos), seq=S)

    nb_pad = S // CMP_STRIDE
    zw = CMP_STRIDE * HEAD_DIM

    def strides(k):
        z = proj_b[:, B_WIDTH + k * KV_WIDTH:B_WIDTH + (k + 1) * KV_WIDTH]
        return z.reshape(B, S, G, HEAD_DIM).transpose(0, 2, 1, 3).reshape(B, G, nb_pad, zw)

    cmp_end = jnp.arange(nb_pad) * CMP_STRIDE + (CMP_BLOCK - 1)
    kc, vc = _compress(strides(0), strides(1), cmp_k_pe.reshape(1, -1), cmp_v_pe.reshape(1, -1),
                       cmp_k_w1.astype(BF16), cmp_k_w2.astype(BF16),
                       cmp_v_w1.astype(BF16), cmp_v_w2.astype(BF16),
                       row(k_cmp_norm), *_rope_tables(cmp_end))

    o_c, bias = _cmp_attn(qn, kc, vc, proj_b, _overlap_t(nb_pad, S // SEL_BLOCK), batch=B, seq=S)
    o_b = _sw_attn(qn, bias, ksa, vs, kwn, vw, o_c, proj_b, batch=B, seq=S)

    merged = _merge(o_a, o_b, proj_a, w_branch_a.astype(BF16), w_branch_b.astype(BF16))
    x2 = _out_proj(x1, merged, w_out.astype(BF16))

    x3 = _ffn(x2, row(ffn2_norm), *_ffn_weights(ffn2_w_gate, ffn2_w_up, ffn2_w_down))
    return x3.reshape(B, S, D)


def kernel(x, ffn1_norm, ffn1_w_gate, ffn1_w_up, ffn1_w_down, mix_norm, w_in, a_v_norm, a_w_s, a_b_s, q_norm, k_cmp_norm, k_slc_norm, k_win_norm, cmp_k_pe, cmp_k_w1, cmp_k_w2, cmp_v_pe, cmp_v_w1, cmp_v_w2, w_branch_a, w_branch_b, w_out, ffn2_norm, ffn2_w_gate, ffn2_w_up, ffn2_w_down):
    params = (ffn1_norm, ffn1_w_gate, ffn1_w_up, ffn1_w_down, mix_norm, w_in,
              a_v_norm, a_w_s, a_b_s, q_norm, k_cmp_norm, k_slc_norm, k_win_norm,
              cmp_k_pe, cmp_k_w1, cmp_k_w2, cmp_v_pe, cmp_v_w1, cmp_v_w2,
              w_branch_a, w_branch_b, w_out, ffn2_norm, ffn2_w_gate, ffn2_w_up, ffn2_w_down)
    for l in range(params[0].shape[0]):
        x = _layer(x, *[p[l] for p in params])
    return x
```

```python
import functools

import numpy as np
import jax
import jax.numpy as jnp
from jax import lax
from jax.experimental import pallas as pl
from jax.experimental.pallas import tpu as pltpu

D_MODEL = 2048
D_FF = 5504
A_WIDTH = 2048
A_GROUPS = 8
A_CHUNK = 128
N_HEADS = 16
HEAD_DIM = 128
N_KV_GROUPS = 2
HEADS_PER_GROUP = N_HEADS // N_KV_GROUPS
CMP_BLOCK = 32
CMP_STRIDE = 16
CMP_HIDDEN = 256
SEL_BLOCK = 64
SEL_TOP = 16
WINDOW = 512
ROPE_THETA = 500000.0
ROPE_DIM = HEAD_DIM // 4
EPS = 1e-6
NEG = -1e30
FORCE_BONUS = 1e4
LOG2_E = 1.4426950408889634
B_WIDTH = N_HEADS * HEAD_DIM
KV_WIDTH = N_KV_GROUPS * HEAD_DIM

LANES = 128
VMEM_LIMIT = 56 * 1024 * 1024

F32 = jnp.float32
BF16 = jnp.bfloat16
NT_DIMS = (((1,), (1,)), ((), ()))


def _params(*sem):
    return pltpu.CompilerParams(dimension_semantics=sem, vmem_limit_bytes=VMEM_LIMIT)


def _rms(x, g):
    ms = jnp.mean(x * x, axis=-1, keepdims=True)
    return x * lax.rsqrt(ms + EPS) * g


def _ffn_kernel(x_ref, g_ref, wg_ref, wu_ref, wd_ref, o_ref, h_ref, acc_ref, *, nf, tf, f_last):
    j = pl.program_id(1)

    @pl.when(j == 0)
    def _():
        h_ref[...] = _rms(x_ref[...], g_ref[...]).astype(BF16)
        acc_ref[...] = jnp.zeros_like(acc_ref)

    def partial(width):
        h = h_ref[...]
        a = jnp.dot(h, wg_ref[:, :width], preferred_element_type=F32)
        b = jnp.dot(h, wu_ref[:, :width], preferred_element_type=F32)
        act = (a * jax.nn.sigmoid(a) * b).astype(BF16)
        return jnp.dot(act, wd_ref[:width, :], preferred_element_type=F32)

    @pl.when(j < nf - 1)
    def _():
        acc_ref[...] += partial(tf)

    @pl.when(j == nf - 1)
    def _():
        o_ref[...] = x_ref[...] + 0.5 * (acc_ref[...] + partial(f_last))


def _ffn(x, norm, wg, wu, wd, *, tm=512, tf=512):
    T, D = x.shape
    F = wg.shape[1]
    nf = pl.cdiv(F, tf)
    f_last = F - (nf - 1) * tf
    return pl.pallas_call(
        functools.partial(_ffn_kernel, nf=nf, tf=tf, f_last=f_last),
        grid=(T // tm, nf),
        in_specs=[
            pl.BlockSpec((tm, D), lambda i, j: (i, 0)),
            pl.BlockSpec((1, D), lambda i, j: (0, 0)),
            pl.BlockSpec((D, tf), lambda i, j: (0, j)),
            pl.BlockSpec((D, tf), lambda i, j: (0, j)),
            pl.BlockSpec((tf, D), lambda i, j: (j, 0)),
        ],
        out_specs=pl.BlockSpec((tm, D), lambda i, j: (i, 0)),
        out_shape=jax.ShapeDtypeStruct((T, D), F32),
        scratch_shapes=[pltpu.VMEM((tm, D), BF16), pltpu.VMEM((tm, D), F32)],
        compiler_params=_params("parallel", "arbitrary"),
        name="ffn",
    )(x, norm, wg, wu, wd)


def _proj_kernel(x_ref, g_ref, w_ref, o_ref, h_ref):
    @pl.when(pl.program_id(1) == 0)
    def _():
        h_ref[...] = _rms(x_ref[...], g_ref[...]).astype(BF16)

    o_ref[...] = jnp.dot(h_ref[...], w_ref[...], preferred_element_type=F32).astype(o_ref.dtype)


def _proj(x, norm, w, out_dtype, *, tm, tn):
    T, D = x.shape
    N = w.shape[1]
    return pl.pallas_call(
        _proj_kernel,
        grid=(T // tm, N // tn),
        in_specs=[
            pl.BlockSpec((tm, D), lambda i, j: (i, 0)),
            pl.BlockSpec((1, D), lambda i, j: (0, 0)),
            pl.BlockSpec((D, tn), lambda i, j: (0, j)),
        ],
        out_specs=pl.BlockSpec((tm, tn), lambda i, j: (i, j)),
        out_shape=jax.ShapeDtypeStruct((T, N), out_dtype),
        scratch_shapes=[pltpu.VMEM((tm, D), BF16)],
        compiler_params=_params("parallel", "arbitrary"),
        name="proj",
    )(x, norm, w)


def _gmlp_kernel(u_ref, v_ref, vn_ref, ws_ref, bs_ref, o_ref, *, tr):
    gw = A_WIDTH // A_GROUPS
    v = jax.nn.gelu(v_ref[...].astype(F32))
    vn = _rms(v, vn_ref[...]).astype(BF16)
    row = lax.broadcasted_iota(jnp.int32, (A_CHUNK, A_CHUNK), 0)
    col = lax.broadcasted_iota(jnp.int32, (A_CHUNK, A_CHUNK), 1)
    for g in range(A_GROUPS):
        w = jnp.where(col <= row, ws_ref[g], 0.0).astype(BF16)
        bias = bs_ref[:, g:g + 1]
        for c in range(tr // A_CHUNK):
            rows = slice(c * A_CHUNK, (c + 1) * A_CHUNK)
            cols = slice(g * gw, (g + 1) * gw)
            mixed = jnp.dot(w, vn[rows, cols], preferred_element_type=F32) + bias
            u = jax.nn.gelu(u_ref[rows, cols].astype(F32))
            o_ref[rows, cols] = (u * mixed).astype(o_ref.dtype)


def _gmlp(proj_a, a_v_norm, w_s, b_s_t, *, tr=512):
    T = proj_a.shape[0]
    return pl.pallas_call(
        functools.partial(_gmlp_kernel, tr=tr),
        grid=(T // tr,),
        in_specs=[
            pl.BlockSpec((tr, A_WIDTH), lambda i: (i, 0)),
            pl.BlockSpec((tr, A_WIDTH), lambda i: (i, 1)),
            pl.BlockSpec((1, A_WIDTH), lambda i: (0, 0)),
            pl.BlockSpec((A_GROUPS, A_CHUNK, A_CHUNK), lambda i: (0, 0, 0)),
            pl.BlockSpec((A_CHUNK, A_GROUPS), lambda i: (0, 0)),
        ],
        out_specs=pl.BlockSpec((tr, A_WIDTH), lambda i: (i, 0)),
        out_shape=jax.ShapeDtypeStruct((T, A_WIDTH), BF16),
        compiler_params=_params("parallel"),
        name="gmlp",
    )(proj_a, proj_a, a_v_norm, w_s, b_s_t)


def _rope(y, c, su, sd):
    return y * c + pltpu.roll(y, 16, 1) * su + pltpu.roll(y, HEAD_DIM - 16, 1) * sd


def _prep_kernel(q_ref, ks_ref, vs_ref, kw_ref, vw_ref, qg_ref, ksg_ref, kwg_ref,
                 c_ref, su_ref, sd_ref, qn_ref, ksa_ref, vso_ref, kwn_ref, vwo_ref, *, tr, seq):
    c, su, sd = c_ref[...], su_ref[...], sd_ref[...]
    scale = HEAD_DIM ** -0.5 * LOG2_E
    for h in range(N_HEADS):
        cols = slice(h * HEAD_DIM, (h + 1) * HEAD_DIM)
        y = _rope(_rms(q_ref[:, cols], qg_ref[...]), c, su, sd)
        qn_ref[:, cols] = (y * scale).astype(BF16)
    t0 = (pl.program_id(0) * tr) % seq
    t = t0 + lax.broadcasted_iota(jnp.int32, (tr, LANES), 0)
    lane = lax.broadcasted_iota(jnp.int32, (tr, LANES), 1)
    onehot = jnp.where((t // SEL_BLOCK) == lane, 1.0, 0.0).astype(BF16)
    ones = jnp.ones((tr, HEAD_DIM), BF16)
    for g in range(N_KV_GROUPS):
        cols = slice(g * HEAD_DIM, (g + 1) * HEAD_DIM)
        lo = slice(2 * g * HEAD_DIM, (2 * g + 1) * HEAD_DIM)
        hi = slice((2 * g + 1) * HEAD_DIM, (2 * g + 2) * HEAD_DIM)
        y = _rope(_rms(ks_ref[:, cols], ksg_ref[...]), c, su, sd)
        ksa_ref[:, lo] = y.astype(BF16)
        ksa_ref[:, hi] = onehot
        y = _rope(_rms(kw_ref[:, cols], kwg_ref[...]), c, su, sd)
        kwn_ref[:, cols] = y.astype(BF16)
        vso_ref[:, lo] = vs_ref[:, cols].astype(BF16)
        vso_ref[:, hi] = ones
        vwo_ref[:, lo] = vw_ref[:, cols].astype(BF16)
        vwo_ref[:, hi] = ones


def _prep(proj_b, q_norm, k_slc_norm, k_win_norm, rope_c, rope_su, rope_sd, *, seq, tr=512):
    T = proj_b.shape[0]
    kvb = B_WIDTH // KV_WIDTH
    nrt = seq // tr
    row = lambda i: (i, 0)
    tab = pl.BlockSpec((tr, HEAD_DIM), lambda i: (i % nrt, 0))
    gain = pl.BlockSpec((1, HEAD_DIM), lambda i: (0, 0))
    kv_in = lambda k: pl.BlockSpec((tr, KV_WIDTH), lambda i: (i, kvb + k))
    return pl.pallas_call(
        functools.partial(_prep_kernel, tr=tr, seq=seq),
        grid=(T // tr,),
        in_specs=[pl.BlockSpec((tr, B_WIDTH), row), kv_in(2), kv_in(3), kv_in(4), kv_in(5),
                  gain, gain, gain, tab, tab, tab],
        out_specs=[pl.BlockSpec((tr, B_WIDTH), row), pl.BlockSpec((tr, 2 * KV_WIDTH), row),
                   pl.BlockSpec((tr, 2 * KV_WIDTH), row), pl.BlockSpec((tr, KV_WIDTH), row),
                   pl.BlockSpec((tr, 2 * KV_WIDTH), row)],
        out_shape=[jax.ShapeDtypeStruct((T, B_WIDTH), BF16),
                   jax.ShapeDtypeStruct((T, 2 * KV_WIDTH), BF16),
                   jax.ShapeDtypeStruct((T, 2 * KV_WIDTH), BF16),
                   jax.ShapeDtypeStruct((T, KV_WIDTH), BF16),
                   jax.ShapeDtypeStruct((T, 2 * KV_WIDTH), BF16)],
        compiler_params=_params("parallel"),
        name="qk_prep",
    )(proj_b, proj_b, proj_b, proj_b, proj_b, q_norm, k_slc_norm, k_win_norm,
      rope_c, rope_su, rope_sd)


def _compress_kernel(zk_ref, zv_ref, pek_ref, pev_ref, w1k_ref, w2k_ref, w1v_ref, w2v_ref,
                     kg_ref, c_ref, su_ref, sd_ref, kc_ref, vc_ref, zr_ref):
    half = CMP_STRIDE * HEAD_DIM
    nb = zr_ref.shape[0]

    def comp(z_ref, pe_ref, w1_ref, w2_ref):
        for l in range(CMP_STRIDE):
            zr_ref[:, l * HEAD_DIM:(l + 1) * HEAD_DIM] = z_ref[pl.ds(l, nb, stride=CMP_STRIDE), :]
        z = zr_ref[...]
        top = jnp.dot((z + pe_ref[:, :half]).astype(BF16), w1_ref[:half, :], preferred_element_type=F32)
        bot = jnp.dot((z + pe_ref[:, half:]).astype(BF16), w1_ref[half:, :], preferred_element_type=F32)
        hid = top + pltpu.roll(bot, nb - 1, 0)
        return jnp.dot(jax.nn.gelu(hid).astype(BF16), w2_ref[...], preferred_element_type=F32)

    k = comp(zk_ref, pek_ref, w1k_ref, w2k_ref)
    kc_ref[...] = _rope(_rms(k, kg_ref[...]), c_ref[...], su_ref[...], sd_ref[...]).astype(BF16)
    vc_ref[...] = comp(zv_ref, pev_ref, w1v_ref, w2v_ref).astype(BF16)


def _compress(proj_b, pek, pev, w1k, w2k, w1v, w2v, k_cmp_norm, cmp_c, cmp_su, cmp_sd, *, batch, seq):
    G = N_KV_GROUPS
    nb = seq // CMP_STRIDE
    kc_blk = B_WIDTH // HEAD_DIM
    vc_blk = kc_blk + G
    ospec = pl.BlockSpec((None, None, nb, HEAD_DIM), lambda b, g: (b, g, 0, 0))
    full = lambda a: pl.BlockSpec(a.shape, lambda b, g: (0,) * a.ndim)
    consts = (pek, pev, w1k, w2k, w1v, w2v, k_cmp_norm, cmp_c, cmp_su, cmp_sd)
    return pl.pallas_call(
        _compress_kernel,
        grid=(batch, G),
        in_specs=[pl.BlockSpec((seq, HEAD_DIM), lambda b, g: (b, kc_blk + g)),
                  pl.BlockSpec((seq, HEAD_DIM), lambda b, g: (b, vc_blk + g))]
                 + [full(a) for a in consts],
        out_specs=[ospec, ospec],
        out_shape=[jax.ShapeDtypeStruct((batch, G, nb, HEAD_DIM), BF16)] * 2,
        scratch_shapes=[pltpu.VMEM((nb, CMP_STRIDE * HEAD_DIM), F32)],
        compiler_params=_params("parallel", "parallel"),
        name="compress",
    )(proj_b, proj_b, *consts)


def _cmp_attn_kernel(q_ref, kc_ref, vc_ref, gate_ref, ovt_ref, oc_ref, bias_ref, qs_ref, *, tq):
    hpg = HEADS_PER_GROUP
    nb = kc_ref.shape[0]
    n_sel = ovt_ref.shape[0]
    t0 = pl.program_id(2) * tq
    for h in range(hpg):
        qs_ref[h * tq:(h + 1) * tq, :] = q_ref[:, h * HEAD_DIM:(h + 1) * HEAD_DIM]
    s = lax.dot_general(qs_ref[...], kc_ref[...], NT_DIMS, preferred_element_type=F32)
    s = s.reshape(hpg, tq, nb)
    t = t0 + lax.broadcasted_iota(jnp.int32, (1, tq, nb), 1)
    n = lax.broadcasted_iota(jnp.int32, (1, tq, nb), 2)
    mask = n * CMP_STRIDE + (CMP_BLOCK - 1) <= t
    s = jnp.where(mask, s, NEG)
    e = jnp.exp2(s - jnp.max(s, axis=-1, keepdims=True))
    inv = 1.0 / jnp.sum(e, axis=-1, keepdims=True)
    p = jnp.where(mask, e * inv, 0.0)
    o = jnp.dot(p.reshape(hpg * tq, nb).astype(BF16), vc_ref[...], preferred_element_type=F32)
    sig = jax.nn.sigmoid(gate_ref[...])
    for h in range(hpg):
        oc_ref[:, h * HEAD_DIM:(h + 1) * HEAD_DIM] = (
            o[h * tq:(h + 1) * tq] * sig[:, h:h + 1]).astype(oc_ref.dtype)

    psum = jnp.sum(p, axis=0)
    ovt = ovt_ref[...]
    hi = psum.astype(BF16)
    r1 = psum - hi.astype(F32)
    mid = r1.astype(BF16)
    lo = (r1 - mid.astype(F32)).astype(BF16)
    imp = (lax.dot_general(ovt, hi, NT_DIMS, preferred_element_type=F32)
           + lax.dot_general(ovt, mid, NT_DIMS, preferred_element_type=F32)
           + lax.dot_general(ovt, lo, NT_DIMS, preferred_element_type=F32))
    j = lax.broadcasted_iota(jnp.int32, (n_sel, tq), 0)
    tt = t0 + lax.broadcasted_iota(jnp.int32, (n_sel, tq), 1)
    cur = tt // SEL_BLOCK
    forced = jnp.where((j == 0) | (j == cur) | (j == cur - 1), FORCE_BONUS, 0.0)
    score = jnp.where(j * SEL_BLOCK <= tt, imp + forced, NEG)
    sub = 8
    groups = [score[k * sub:(k + 1) * sub, :] for k in range(n_sel // sub)]
    ranks = [jnp.zeros((sub, tq), F32) for _ in groups]
    jrow = lax.broadcasted_iota(jnp.int32, (sub, tq), 0)
    for i in range(n_sel):
        r = jnp.broadcast_to(score[i:i + 1, :], (sub, tq))
        for k, grp in enumerate(groups):
            if k * sub > i:
                beats = r >= grp
            elif (k + 1) * sub <= i:
                beats = r > grp
            else:
                beats = jnp.where(jrow + k * sub > i, jnp.where(r >= grp, 1.0, 0.0),
                                  jnp.where(r > grp, 1.0, 0.0)) > 0.5
            ranks[k] = ranks[k] + jnp.where(beats, 1.0, 0.0)
    rank = jnp.concatenate(ranks, axis=0)
    bias = jnp.where(rank < SEL_TOP, 0.0, NEG)
    bias = jnp.concatenate([bias, jnp.zeros((LANES - n_sel, tq), F32)], axis=0)
    bias_ref[...] = bias.T.astype(bias_ref.dtype)


def _cmp_attn(qn, kc, vc, proj_b, ovt, *, batch, seq, tq=128):
    T = qn.shape[0]
    G = N_KV_GROUPS
    nq = seq // tq
    gw = HEADS_PER_GROUP * HEAD_DIM
    nb = kc.shape[2]
    gate_blk = (B_WIDTH + 6 * KV_WIDTH) // LANES
    rowg = lambda b, g, i: (b * nq + i, g)
    return pl.pallas_call(
        functools.partial(_cmp_attn_kernel, tq=tq),
        grid=(batch, G, nq),
        in_specs=[
            pl.BlockSpec((tq, gw), rowg),
            pl.BlockSpec((None, None, nb, HEAD_DIM), lambda b, g, i: (b, g, 0, 0)),
            pl.BlockSpec((None, None, nb, HEAD_DIM), lambda b, g, i: (b, g, 0, 0)),
            pl.BlockSpec((tq, LANES), lambda b, g, i: (b * nq + i, gate_blk + g)),
            pl.BlockSpec(ovt.shape, lambda b, g, i: (0, 0)),
        ],
        out_specs=[pl.BlockSpec((tq, gw), rowg), pl.BlockSpec((tq, LANES), rowg)],
        out_shape=[jax.ShapeDtypeStruct((T, B_WIDTH), BF16),
                   jax.ShapeDtypeStruct((T, G * LANES), BF16)],
        scratch_shapes=[pltpu.VMEM((HEADS_PER_GROUP * tq, HEAD_DIM), BF16)],
        compiler_params=_params("parallel", "parallel", "parallel"),
        name="cmp_attn",
    )(qn, kc, vc, proj_b, ovt)


def _sw_attn_kernel(q_ref, bias_ref, ksa_ref, vs_ref, kw_ref, vw_ref, oc_ref, gate_ref, o_ref,
                    qs_ref, m_ref, acc_ref, ss0_ref, ss1_ref, sw0_ref, sw1_ref, *, tq, tks, tkw):
    hpg = HEADS_PER_GROUP
    M = hpg * tq
    ss_ref = (ss0_ref, ss1_ref)
    sw_ref = (sw0_ref, sw1_ref)
    t0 = pl.program_id(2) * tq
    bias = bias_ref[...]
    for h in range(hpg):
        qs_ref[h * tq:(h + 1) * tq, :HEAD_DIM] = q_ref[:, h * HEAD_DIM:(h + 1) * HEAD_DIM]
        qs_ref[h * tq:(h + 1) * tq, HEAD_DIM:] = bias

    def reset():
        m_ref[...] = jnp.full_like(m_ref, NEG)
        acc_ref[...] = jnp.zeros_like(acc_ref)

    def step(s, v):
        m_old = m_ref[...]
        m_new = jnp.maximum(m_old, jnp.max(s, axis=-1, keepdims=True))
        alpha = jnp.exp2(m_old - m_new)
        p = jnp.exp2(s - jnp.tile(m_new, (1, s.shape[1] // LANES)))
        acc_ref[...] = (jnp.tile(alpha, (1, 2)) * acc_ref[...]
                        + jnp.dot(p.astype(BF16), v, preferred_element_type=F32))
        m_ref[...] = m_new

    def result():
        return acc_ref[:, :HEAD_DIM] * (1.0 / acc_ref[:, HEAD_DIM:])

    def masked(s, keep):
        tk = s.shape[1]
        return jnp.where(keep, s.reshape(hpg, tq, tk), NEG).reshape(M, tk)

    def positions(tk):
        t = t0 + lax.broadcasted_iota(jnp.int32, (1, tq, tk), 1)
        kpos = lax.broadcasted_iota(jnp.int32, (1, tq, tk), 2)
        return t, kpos

    def sel_base(j):
        return pl.multiple_of(j * tks, tks)

    def win_base(j):
        return pl.multiple_of(j * tkw, tkw)

    def sel_scores(j):
        k = ksa_ref[pl.ds(sel_base(j), tks), :]
        return lax.dot_general(qs_ref[...], k, NT_DIMS, preferred_element_type=F32)

    def win_scores(j):
        k = kw_ref[pl.ds(win_base(j), tkw), :]
        return lax.dot_general(qs_ref[:, :HEAD_DIM], k, NT_DIMS, preferred_element_type=F32)

    sel_last = (t0 + tq - 1) // tks
    win_last = (t0 + tq - 1) // tkw
    win_first = jnp.maximum(t0 - (WINDOW - 1), 0) // tkw

    def by_parity(n, fn):
        @pl.when(n % 2 == 0)
        def _():
            fn(0, 1)

        @pl.when(n % 2 == 1)
        def _():
            fn(1, 0)

    reset()
    ss_ref[0][...] = sel_scores(0)

    def sel_body(j, carry):
        def run(cur, nxt):
            ss_ref[nxt][...] = sel_scores(j + 1)
            step(ss_ref[cur][...], vs_ref[pl.ds(sel_base(j), tks), :])

        by_parity(j, run)
        return carry

    lax.fori_loop(0, sel_last, sel_body, 0)
    sw_ref[0][...] = win_scores(win_first)

    def sel_final(cur, nxt):
        t, kpos = positions(tks)
        s = masked(ss_ref[cur][...], kpos + sel_base(sel_last) <= t)
        step(s, vs_ref[pl.ds(sel_base(sel_last), tks), :])

    by_parity(sel_last, sel_final)
    o_s = result()

    def win_step(s, j):
        t, kpos = positions(tkw)
        diff = (t - (kpos + win_base(j))).astype(jnp.uint32)
        step(masked(s, diff < WINDOW), vw_ref[pl.ds(win_base(j), tkw), :])

    reset()

    def win_body(j, carry):
        def run(cur, nxt):
            sw_ref[nxt][...] = win_scores(j + 1)
            win_step(sw_ref[cur][...], j)

        by_parity(j - win_first, run)
        return carry

    lax.fori_loop(win_first, win_last, win_body, 0)
    by_parity(win_last - win_first, lambda cur, nxt: win_step(sw_ref[cur][...], win_last))
    o_w = result()

    sig = jax.nn.sigmoid(gate_ref[...])
    for h in range(hpg):
        rows = slice(h * tq, (h + 1) * tq)
        cols = slice(h * HEAD_DIM, (h + 1) * HEAD_DIM)
        o = (oc_ref[:, cols].astype(F32) + sig[:, hpg + h:hpg + h + 1] * o_s[rows]
             + sig[:, 2 * hpg + h:2 * hpg + h + 1] * o_w[rows])
        o_ref[:, cols] = o.astype(o_ref.dtype)


def _sw_attn(qn, bias, ksa, vs, kwn, vw, o_c, proj_b, *, batch, seq, tq=128, tks=512, tkw=256):
    T = qn.shape[0]
    G = N_KV_GROUPS
    nq = seq // tq
    gw = HEADS_PER_GROUP * HEAD_DIM
    M = HEADS_PER_GROUP * tq
    gate_blk = (B_WIDTH + 6 * KV_WIDTH) // LANES
    rowg = lambda b, g, i: (b * nq + i, g)
    seqg = lambda b, g, i: (b, g)
    return pl.pallas_call(
        functools.partial(_sw_attn_kernel, tq=tq, tks=tks, tkw=tkw),
        grid=(batch, G, nq),
        in_specs=[
            pl.BlockSpec((tq, gw), rowg),
            pl.BlockSpec((tq, LANES), rowg),
            pl.BlockSpec((seq, 2 * HEAD_DIM), seqg),
            pl.BlockSpec((seq, 2 * HEAD_DIM), seqg),
            pl.BlockSpec((seq, HEAD_DIM), seqg),
            pl.BlockSpec((seq, 2 * HEAD_DIM), seqg),
            pl.BlockSpec((tq, gw), rowg),
            pl.BlockSpec((tq, LANES), lambda b, g, i: (b * nq + i, gate_blk + g)),
        ],
        out_specs=pl.BlockSpec((tq, gw), rowg),
        out_shape=jax.ShapeDtypeStruct((T, B_WIDTH), BF16),
        scratch_shapes=[pltpu.VMEM((M, 2 * HEAD_DIM), BF16), pltpu.VMEM((M, LANES), F32),
                        pltpu.VMEM((M, 2 * HEAD_DIM), F32),
                        pltpu.VMEM((M, tks), F32), pltpu.VMEM((M, tks), F32),
                        pltpu.VMEM((M, tkw), F32), pltpu.VMEM((M, tkw), F32)],
        compiler_params=_params("parallel", "parallel", "parallel"),
        name="sw_attn",
    )(qn, bias, ksa, vs, kwn, vw, o_c, proj_b)


def _merge_kernel(oa_ref, ob_ref, ga_ref, gb_ref, wa_ref, wb_ref, o_ref):
    a = jnp.dot(oa_ref[...], wa_ref[...], preferred_element_type=F32)
    b = jnp.dot(ob_ref[...], wb_ref[...], preferred_element_type=F32)
    ga = jax.nn.sigmoid(ga_ref[...].astype(F32))
    gb = jax.nn.sigmoid(gb_ref[...].astype(F32))
    o_ref[...] = (ga * a + gb * b).astype(o_ref.dtype)


def _merge(o_a, o_b, proj_a, wa, wb, *, tm=1024, tn=512):
    T, D = o_a.shape
    N = wa.shape[1]
    nn = N // tn
    ga_blk = 2 * A_WIDTH // tn
    return pl.pallas_call(
        _merge_kernel,
        grid=(T // tm, nn),
        in_specs=[
            pl.BlockSpec((tm, D), lambda i, j: (i, 0)),
            pl.BlockSpec((tm, D), lambda i, j: (i, 0)),
            pl.BlockSpec((tm, tn), lambda i, j: (i, ga_blk + j)),
            pl.BlockSpec((tm, tn), lambda i, j: (i, ga_blk + nn + j)),
            pl.BlockSpec((D, tn), lambda i, j: (0, j)),
            pl.BlockSpec((D, tn), lambda i, j: (0, j)),
        ],
        out_specs=pl.BlockSpec((tm, tn), lambda i, j: (i, j)),
        out_shape=jax.ShapeDtypeStruct((T, N), BF16),
        compiler_params=_params("parallel", "arbitrary"),
        name="merge",
    )(o_a, o_b, proj_a, proj_a, wa, wb)


def _out_kernel(x_ref, m_ref, w_ref, o_ref):
    o_ref[...] = x_ref[...] + jnp.dot(m_ref[...], w_ref[...], preferred_element_type=F32)


def _out_proj(x, merged, w, *, tm=1024, tn=512):
    T, D = merged.shape
    N = w.shape[1]
    return pl.pallas_call(
        _out_kernel,
        grid=(T // tm, N // tn),
        in_specs=[
            pl.BlockSpec((tm, tn), lambda i, j: (i, j)),
            pl.BlockSpec((tm, D), lambda i, j: (i, 0)),
            pl.BlockSpec((D, tn), lambda i, j: (0, j)),
        ],
        out_specs=pl.BlockSpec((tm, tn), lambda i, j: (i, j)),
        out_shape=jax.ShapeDtypeStruct((T, N), F32),
        compiler_params=_params("parallel", "arbitrary"),
        name="out_proj",
    )(x, merged, w)


def _rope_tables(pos):
    half = ROPE_DIM // 2
    inv = ROPE_THETA ** (-2.0 * jnp.arange(half, dtype=F32) / ROPE_DIM)
    ang = pos.astype(F32)[:, None] * inv
    cos, sin = jnp.cos(ang), jnp.sin(ang)
    n = pos.shape[0]
    z16 = jnp.zeros((n, half), F32)
    rest = HEAD_DIM - ROPE_DIM
    c = jnp.concatenate([cos, cos, jnp.ones((n, rest), F32)], axis=1)
    su = jnp.concatenate([z16, sin, jnp.zeros((n, rest), F32)], axis=1)
    sd = jnp.concatenate([-sin, z16, jnp.zeros((n, rest), F32)], axis=1)
    return c, su, sd


def _split_w_in(w_in):
    sizes = [A_WIDTH, A_WIDTH, B_WIDTH] + [KV_WIDTH] * 6 + [3 * N_HEADS, D_MODEL, D_MODEL]
    offs = np.concatenate([[0], np.cumsum(sizes)])
    w = w_in.astype(BF16)
    w_a = jnp.concatenate([w[:, :offs[2]], w[:, offs[10]:]], axis=1)
    gates = w[:, offs[9]:offs[10]]
    gate_blocks = []
    for g in range(N_KV_GROUPS):
        cols = [gates[:, br * N_HEADS + g * HEADS_PER_GROUP:br * N_HEADS + (g + 1) * HEADS_PER_GROUP]
                for br in range(3)]
        pad = jnp.zeros((w.shape[0], LANES - 3 * HEADS_PER_GROUP), BF16)
        gate_blocks.extend(cols + [pad])
    w_b = jnp.concatenate([w[:, offs[2]:offs[9]]] + gate_blocks, axis=1)
    return w_a, w_b


def _overlap_t(nb_pad, n_sel):
    cmp_start = np.arange(nb_pad) * CMP_STRIDE
    sel_start = np.arange(n_sel) * SEL_BLOCK
    ov = ((cmp_start[None, :] < sel_start[:, None] + SEL_BLOCK)
          & (cmp_start[None, :] + CMP_BLOCK > sel_start[:, None]))
    return jnp.asarray(ov, dtype=BF16)


def _layer(x, ffn1_norm, ffn1_w_gate, ffn1_w_up, ffn1_w_down, mix_norm, w_in,
           a_v_norm, a_w_s, a_b_s, q_norm, k_cmp_norm, k_slc_norm, k_win_norm,
           cmp_k_pe, cmp_k_w1, cmp_k_w2, cmp_v_pe, cmp_v_w1, cmp_v_w2,
           w_branch_a, w_branch_b, w_out, ffn2_norm, ffn2_w_gate, ffn2_w_up, ffn2_w_down):
    B, S, D = x.shape
    T = B * S
    G = N_KV_GROUPS
    row = lambda v: v.reshape(1, -1)
    x0 = x.reshape(T, D)

    x1 = _ffn(x0, row(ffn1_norm), ffn1_w_gate.astype(BF16), ffn1_w_up.astype(BF16),
              ffn1_w_down.astype(BF16))

    w_a, w_b = _split_w_in(w_in)
    proj_a = _proj(x1, row(mix_norm), w_a, BF16, tm=1024, tn=1024)
    proj_b = _proj(x1, row(mix_norm), w_b, F32, tm=1024, tn=768)

    o_a = _gmlp(proj_a, row(a_v_norm), a_w_s, a_b_s.T)

    pos = jnp.arange(S)
    qn, ksa, vs, kwn, vw = _prep(proj_b, row(q_norm), row(k_slc_norm), row(k_win_norm),
                                 *_rope_tables(pos), seq=S)

    nb_pad = S // CMP_STRIDE
    cmp_end = jnp.arange(nb_pad) * CMP_STRIDE + (CMP_BLOCK - 1)
    kc, vc = _compress(proj_b, cmp_k_pe.reshape(1, -1), cmp_v_pe.reshape(1, -1),
                       cmp_k_w1.astype(BF16), cmp_k_w2.astype(BF16),
                       cmp_v_w1.astype(BF16), cmp_v_w2.astype(BF16),
                       row(k_cmp_norm), *_rope_tables(cmp_end), batch=B, seq=S)

    o_c, bias = _cmp_attn(qn, kc, vc, proj_b, _overlap_t(nb_pad, S // SEL_BLOCK), batch=B, seq=S)
    o_b = _sw_attn(qn, bias, ksa, vs, kwn, vw, o_c, proj_b, batch=B, seq=S)

    merged = _merge(o_a, o_b, proj_a, w_branch_a.astype(BF16), w_branch_b.astype(BF16))
    x2 = _out_proj(x1, merged, w_out.astype(BF16))

    x3 = _ffn(x2, row(ffn2_norm), ffn2_w_gate.astype(BF16), ffn2_w_up.astype(BF16),
              ffn2_w_down.astype(BF16))
    return x3.reshape(B, S, D)


def kernel(x, ffn1_norm, ffn1_w_gate, ffn1_w_up, ffn1_w_down, mix_norm, w_in, a_v_norm, a_w_s, a_b_s, q_norm, k_cmp_norm, k_slc_norm, k_win_norm, cmp_k_pe, cmp_k_w1, cmp_k_w2, cmp_v_pe, cmp_v_w1, cmp_v_w2, w_branch_a, w_branch_b, w_out, ffn2_norm, ffn2_w_gate, ffn2_w_up, ffn2_w_down):
    params = (ffn1_norm, ffn1_w_gate, ffn1_w_up, ffn1_w_down, mix_norm, w_in,
              a_v_norm, a_w_s, a_b_s, q_norm, k_cmp_norm, k_slc_norm, k_win_norm,
              cmp_k_pe, cmp_k_w1, cmp_k_w2, cmp_v_pe, cmp_v_w1, cmp_v_w2,
              w_branch_a, w_branch_b, w_out, ffn2_norm, ffn2_w_gate, ffn2_w_up, ffn2_w_down)
    for l in range(params[0].shape[0]):
        x = _layer(x, *[p[l] for p in params])
    return x
```

```python
import functools

import numpy as np
import jax
import jax.numpy as jnp
from jax import lax
from jax.experimental import pallas as pl
from jax.experimental.pallas import tpu as pltpu

D_MODEL = 2048
D_FF = 5504
A_WIDTH = 2048
A_GROUPS = 8
A_CHUNK = 128
N_HEADS = 16
HEAD_DIM = 128
N_KV_GROUPS = 2
HEADS_PER_GROUP = N_HEADS // N_KV_GROUPS
CMP_BLOCK = 32
CMP_STRIDE = 16
CMP_HIDDEN = 256
SEL_BLOCK = 64
SEL_TOP = 16
WINDOW = 512
ROPE_THETA = 500000.0
ROPE_DIM = HEAD_DIM // 4
EPS = 1e-6
NEG = -1e30
FORCE_BONUS = 1e4
LOG2_E = 1.4426950408889634
B_WIDTH = N_HEADS * HEAD_DIM
KV_WIDTH = N_KV_GROUPS * HEAD_DIM

LANES = 128
VMEM_LIMIT = 56 * 1024 * 1024

F32 = jnp.float32
BF16 = jnp.bfloat16
NT_DIMS = (((1,), (1,)), ((), ()))


def _params(*sem):
    return pltpu.CompilerParams(dimension_semantics=sem, vmem_limit_bytes=VMEM_LIMIT)


def _rms(x, g):
    ms = jnp.mean(x * x, axis=-1, keepdims=True)
    return x * lax.rsqrt(ms + EPS) * g


def _ffn_kernel(x_ref, g_ref, wg_ref, wu_ref, wd_ref, o_ref, h_ref, acc_ref, *, nf, tf, f_last):
    j = pl.program_id(1)

    @pl.when(j == 0)
    def _():
        h_ref[...] = _rms(x_ref[...], g_ref[...]).astype(BF16)
        acc_ref[...] = jnp.zeros_like(acc_ref)

    def partial(width):
        h = h_ref[...]
        a = jnp.dot(h, wg_ref[:, :width], preferred_element_type=F32)
        b = jnp.dot(h, wu_ref[:, :width], preferred_element_type=F32)
        act = (a * jax.nn.sigmoid(a) * b).astype(BF16)
        return jnp.dot(act, wd_ref[:width, :], preferred_element_type=F32)

    @pl.when(j < nf - 1)
    def _():
        acc_ref[...] += partial(tf)

    @pl.when(j == nf - 1)
    def _():
        o_ref[...] = x_ref[...] + 0.5 * (acc_ref[...] + partial(f_last))


def _ffn(x, norm, wg, wu, wd, *, tm=512, tf=1024):
    T, D = x.shape
    F = wg.shape[1]
    nf = pl.cdiv(F, tf)
    f_last = F - (nf - 1) * tf
    return pl.pallas_call(
        functools.partial(_ffn_kernel, nf=nf, tf=tf, f_last=f_last),
        grid=(T // tm, nf),
        in_specs=[
            pl.BlockSpec((tm, D), lambda i, j: (i, 0)),
            pl.BlockSpec((1, D), lambda i, j: (0, 0)),
            pl.BlockSpec((D, tf), lambda i, j: (0, j)),
            pl.BlockSpec((D, tf), lambda i, j: (0, j)),
            pl.BlockSpec((tf, D), lambda i, j: (j, 0)),
        ],
        out_specs=pl.BlockSpec((tm, D), lambda i, j: (i, 0)),
        out_shape=jax.ShapeDtypeStruct((T, D), F32),
        scratch_shapes=[pltpu.VMEM((tm, D), BF16), pltpu.VMEM((tm, D), F32)],
        compiler_params=_params("parallel", "arbitrary"),
        name="ffn",
    )(x, norm, wg, wu, wd)


def _proj_kernel(x_ref, g_ref, w_ref, o_ref, h_ref):
    @pl.when(pl.program_id(1) == 0)
    def _():
        h_ref[...] = _rms(x_ref[...], g_ref[...]).astype(BF16)

    o_ref[...] = jnp.dot(h_ref[...], w_ref[...], preferred_element_type=F32).astype(o_ref.dtype)


def _proj(x, norm, w, out_dtype, *, tm, tn):
    T, D = x.shape
    N = w.shape[1]
    return pl.pallas_call(
        _proj_kernel,
        grid=(T // tm, N // tn),
        in_specs=[
            pl.BlockSpec((tm, D), lambda i, j: (i, 0)),
            pl.BlockSpec((1, D), lambda i, j: (0, 0)),
            pl.BlockSpec((D, tn), lambda i, j: (0, j)),
        ],
        out_specs=pl.BlockSpec((tm, tn), lambda i, j: (i, j)),
        out_shape=jax.ShapeDtypeStruct((T, N), out_dtype),
        scratch_shapes=[pltpu.VMEM((tm, D), BF16)],
        compiler_params=_params("parallel", "arbitrary"),
        name="proj",
    )(x, norm, w)


def _gmlp_kernel(u_ref, v_ref, vn_ref, ws_ref, bs_ref, o_ref, *, tr):
    gw = A_WIDTH // A_GROUPS
    v = jax.nn.gelu(v_ref[...].astype(F32))
    vn = _rms(v, vn_ref[...]).astype(BF16)
    row = lax.broadcasted_iota(jnp.int32, (A_CHUNK, A_CHUNK), 0)
    col = lax.broadcasted_iota(jnp.int32, (A_CHUNK, A_CHUNK), 1)
    for g in range(A_GROUPS):
        w = jnp.where(col <= row, ws_ref[g], 0.0).astype(BF16)
        bias = bs_ref[:, g:g + 1]
        for c in range(tr // A_CHUNK):
            rows = slice(c * A_CHUNK, (c + 1) * A_CHUNK)
            cols = slice(g * gw, (g + 1) * gw)
            mixed = jnp.dot(w, vn[rows, cols], preferred_element_type=F32) + bias
            u = jax.nn.gelu(u_ref[rows, cols].astype(F32))
            o_ref[rows, cols] = (u * mixed).astype(o_ref.dtype)


def _gmlp(proj_a, a_v_norm, w_s, b_s_t, *, tr=512):
    T = proj_a.shape[0]
    return pl.pallas_call(
        functools.partial(_gmlp_kernel, tr=tr),
        grid=(T // tr,),
        in_specs=[
            pl.BlockSpec((tr, A_WIDTH), lambda i: (i, 0)),
            pl.BlockSpec((tr, A_WIDTH), lambda i: (i, 1)),
            pl.BlockSpec((1, A_WIDTH), lambda i: (0, 0)),
            pl.BlockSpec((A_GROUPS, A_CHUNK, A_CHUNK), lambda i: (0, 0, 0)),
            pl.BlockSpec((A_CHUNK, A_GROUPS), lambda i: (0, 0)),
        ],
        out_specs=pl.BlockSpec((tr, A_WIDTH), lambda i: (i, 0)),
        out_shape=jax.ShapeDtypeStruct((T, A_WIDTH), BF16),
        compiler_params=_params("parallel"),
        name="gmlp",
    )(proj_a, proj_a, a_v_norm, w_s, b_s_t)


def _rope(y, c, su, sd):
    return y * c + pltpu.roll(y, 16, 1) * su + pltpu.roll(y, HEAD_DIM - 16, 1) * sd


def _prep_kernel(q_ref, ks_ref, vs_ref, kw_ref, vw_ref, qg_ref, ksg_ref, kwg_ref,
                 c_ref, su_ref, sd_ref, qn_ref, ksa_ref, vso_ref, kwn_ref, vwo_ref, *, tr, seq):
    c, su, sd = c_ref[...], su_ref[...], sd_ref[...]
    scale = HEAD_DIM ** -0.5 * LOG2_E
    for h in range(N_HEADS):
        cols = slice(h * HEAD_DIM, (h + 1) * HEAD_DIM)
        y = _rope(_rms(q_ref[:, cols], qg_ref[...]), c, su, sd)
        qn_ref[:, cols] = (y * scale).astype(BF16)
    t0 = (pl.program_id(0) * tr) % seq
    t = t0 + lax.broadcasted_iota(jnp.int32, (tr, LANES), 0)
    lane = lax.broadcasted_iota(jnp.int32, (tr, LANES), 1)
    onehot = jnp.where((t // SEL_BLOCK) == lane, 1.0, 0.0).astype(BF16)
    ones = jnp.ones((tr, HEAD_DIM), BF16)
    for g in range(N_KV_GROUPS):
        cols = slice(g * HEAD_DIM, (g + 1) * HEAD_DIM)
        lo = slice(2 * g * HEAD_DIM, (2 * g + 1) * HEAD_DIM)
        hi = slice((2 * g + 1) * HEAD_DIM, (2 * g + 2) * HEAD_DIM)
        y = _rope(_rms(ks_ref[:, cols], ksg_ref[...]), c, su, sd)
        ksa_ref[:, lo] = y.astype(BF16)
        ksa_ref[:, hi] = onehot
        y = _rope(_rms(kw_ref[:, cols], kwg_ref[...]), c, su, sd)
        kwn_ref[:, cols] = y.astype(BF16)
        vso_ref[:, lo] = vs_ref[:, cols].astype(BF16)
        vso_ref[:, hi] = ones
        vwo_ref[:, lo] = vw_ref[:, cols].astype(BF16)
        vwo_ref[:, hi] = ones


def _prep(proj_b, q_norm, k_slc_norm, k_win_norm, rope_c, rope_su, rope_sd, *, seq, tr=512):
    T = proj_b.shape[0]
    kvb = B_WIDTH // KV_WIDTH
    nrt = seq // tr
    row = lambda i: (i, 0)
    tab = pl.BlockSpec((tr, HEAD_DIM), lambda i: (i % nrt, 0))
    gain = pl.BlockSpec((1, HEAD_DIM), lambda i: (0, 0))
    kv_in = lambda k: pl.BlockSpec((tr, KV_WIDTH), lambda i: (i, kvb + k))
    return pl.pallas_call(
        functools.partial(_prep_kernel, tr=tr, seq=seq),
        grid=(T // tr,),
        in_specs=[pl.BlockSpec((tr, B_WIDTH), row), kv_in(2), kv_in(3), kv_in(4), kv_in(5),
                  gain, gain, gain, tab, tab, tab],
        out_specs=[pl.BlockSpec((tr, B_WIDTH), row), pl.BlockSpec((tr, 2 * KV_WIDTH), row),
                   pl.BlockSpec((tr, 2 * KV_WIDTH), row), pl.BlockSpec((tr, KV_WIDTH), row),
                   pl.BlockSpec((tr, 2 * KV_WIDTH), row)],
        out_shape=[jax.ShapeDtypeStruct((T, B_WIDTH), BF16),
                   jax.ShapeDtypeStruct((T, 2 * KV_WIDTH), BF16),
                   jax.ShapeDtypeStruct((T, 2 * KV_WIDTH), BF16),
                   jax.ShapeDtypeStruct((T, KV_WIDTH), BF16),
                   jax.ShapeDtypeStruct((T, 2 * KV_WIDTH), BF16)],
        compiler_params=_params("parallel"),
        name="qk_prep",
    )(proj_b, proj_b, proj_b, proj_b, proj_b, q_norm, k_slc_norm, k_win_norm,
      rope_c, rope_su, rope_sd)


def _compress_kernel(zk_ref, zv_ref, pek_ref, pev_ref, w1k_ref, w2k_ref, w1v_ref, w2v_ref,
                     kg_ref, c_ref, su_ref, sd_ref, kc_ref, vc_ref, zr_ref):
    half = CMP_STRIDE * HEAD_DIM
    nb = zr_ref.shape[0]

    def comp(z_ref, pe_ref, w1_ref, w2_ref):
        for l in range(CMP_STRIDE):
            zr_ref[:, l * HEAD_DIM:(l + 1) * HEAD_DIM] = z_ref[pl.ds(l, nb, stride=CMP_STRIDE), :]
        z = zr_ref[...]
        top = jnp.dot((z + pe_ref[:, :half]).astype(BF16), w1_ref[:half, :], preferred_element_type=F32)
        bot = jnp.dot((z + pe_ref[:, half:]).astype(BF16), w1_ref[half:, :], preferred_element_type=F32)
        hid = top + pltpu.roll(bot, nb - 1, 0)
        return jnp.dot(jax.nn.gelu(hid).astype(BF16), w2_ref[...], preferred_element_type=F32)

    k = comp(zk_ref, pek_ref, w1k_ref, w2k_ref)
    kc_ref[...] = _rope(_rms(k, kg_ref[...]), c_ref[...], su_ref[...], sd_ref[...]).astype(BF16)
    vc_ref[...] = comp(zv_ref, pev_ref, w1v_ref, w2v_ref).astype(BF16)


def _compress(proj_b, pek, pev, w1k, w2k, w1v, w2v, k_cmp_norm, cmp_c, cmp_su, cmp_sd, *, batch, seq):
    G = N_KV_GROUPS
    nb = seq // CMP_STRIDE
    kc_blk = B_WIDTH // HEAD_DIM
    vc_blk = kc_blk + G
    ospec = pl.BlockSpec((None, None, nb, HEAD_DIM), lambda b, g: (b, g, 0, 0))
    full = lambda a: pl.BlockSpec(a.shape, lambda b, g: (0,) * a.ndim)
    consts = (pek, pev, w1k, w2k, w1v, w2v, k_cmp_norm, cmp_c, cmp_su, cmp_sd)
    return pl.pallas_call(
        _compress_kernel,
        grid=(batch, G),
        in_specs=[pl.BlockSpec((seq, HEAD_DIM), lambda b, g: (b, kc_blk + g)),
                  pl.BlockSpec((seq, HEAD_DIM), lambda b, g: (b, vc_blk + g))]
                 + [full(a) for a in consts],
        out_specs=[ospec, ospec],
        out_shape=[jax.ShapeDtypeStruct((batch, G, nb, HEAD_DIM), BF16)] * 2,
        scratch_shapes=[pltpu.VMEM((nb, CMP_STRIDE * HEAD_DIM), F32)],
        compiler_params=_params("parallel", "parallel"),
        name="compress",
    )(proj_b, proj_b, *consts)


def _cmp_attn_kernel(q_ref, kc_ref, vc_ref, gate_ref, ovt_ref, oc_ref, bias_ref, qs_ref, *, tq):
    hpg = HEADS_PER_GROUP
    nb = kc_ref.shape[0]
    n_sel = ovt_ref.shape[0]
    t0 = pl.program_id(2) * tq
    for h in range(hpg):
        qs_ref[h * tq:(h + 1) * tq, :] = q_ref[:, h * HEAD_DIM:(h + 1) * HEAD_DIM]
    s = lax.dot_general(qs_ref[...], kc_ref[...], NT_DIMS, preferred_element_type=F32)
    s = s.reshape(hpg, tq, nb)
    t = t0 + lax.broadcasted_iota(jnp.int32, (1, tq, nb), 1)
    n = lax.broadcasted_iota(jnp.int32, (1, tq, nb), 2)
    mask = n * CMP_STRIDE + (CMP_BLOCK - 1) <= t
    s = jnp.where(mask, s, NEG)
    e = jnp.exp2(s - jnp.max(s, axis=-1, keepdims=True))
    inv = 1.0 / jnp.sum(e, axis=-1, keepdims=True)
    p = jnp.where(mask, e * inv, 0.0)
    o = jnp.dot(p.reshape(hpg * tq, nb).astype(BF16), vc_ref[...], preferred_element_type=F32)
    sig = jax.nn.sigmoid(gate_ref[...])
    for h in range(hpg):
        oc_ref[:, h * HEAD_DIM:(h + 1) * HEAD_DIM] = (
            o[h * tq:(h + 1) * tq] * sig[:, h:h + 1]).astype(oc_ref.dtype)

    psum = jnp.sum(p, axis=0)
    ovt = ovt_ref[...]
    hi = psum.astype(BF16)
    r1 = psum - hi.astype(F32)
    mid = r1.astype(BF16)
    lo = (r1 - mid.astype(F32)).astype(BF16)
    imp = (lax.dot_general(ovt, hi, NT_DIMS, preferred_element_type=F32)
           + lax.dot_general(ovt, mid, NT_DIMS, preferred_element_type=F32)
           + lax.dot_general(ovt, lo, NT_DIMS, preferred_element_type=F32))
    j = lax.broadcasted_iota(jnp.int32, (n_sel, tq), 0)
    tt = t0 + lax.broadcasted_iota(jnp.int32, (n_sel, tq), 1)
    cur = tt // SEL_BLOCK
    forced = jnp.where((j == 0) | (j == cur) | (j == cur - 1), FORCE_BONUS, 0.0)
    score = jnp.where(j * SEL_BLOCK <= tt, imp + forced, NEG)
    sub = 8
    groups = [score[k * sub:(k + 1) * sub, :] for k in range(n_sel // sub)]
    ranks = [jnp.zeros((sub, tq), F32) for _ in groups]
    jrow = lax.broadcasted_iota(jnp.int32, (sub, tq), 0)
    for i in range(n_sel):
        r = jnp.broadcast_to(score[i:i + 1, :], (sub, tq))
        for k, grp in enumerate(groups):
            if k * sub > i:
                beats = r >= grp
            elif (k + 1) * sub <= i:
                beats = r > grp
            else:
                beats = jnp.where(jrow + k * sub > i, jnp.where(r >= grp, 1.0, 0.0),
                                  jnp.where(r > grp, 1.0, 0.0)) > 0.5
            ranks[k] = ranks[k] + jnp.where(beats, 1.0, 0.0)
    rank = jnp.concatenate(ranks, axis=0)
    bias = jnp.where(rank < SEL_TOP, 0.0, NEG)
    bias = jnp.concatenate([bias, jnp.zeros((LANES - n_sel, tq), F32)], axis=0)
    bias_ref[...] = bias.T.astype(bias_ref.dtype)


def _cmp_attn(qn, kc, vc, proj_b, ovt, *, batch, seq, tq=128):
    T = qn.shape[0]
    G = N_KV_GROUPS
    nq = seq // tq
    gw = HEADS_PER_GROUP * HEAD_DIM
    nb = kc.shape[2]
    gate_blk = (B_WIDTH + 6 * KV_WIDTH) // LANES
    rowg = lambda b, g, i: (b * nq + i, g)
    return pl.pallas_call(
        functools.partial(_cmp_attn_kernel, tq=tq),
        grid=(batch, G, nq),
        in_specs=[
            pl.BlockSpec((tq, gw), rowg),
            pl.BlockSpec((None, None, nb, HEAD_DIM), lambda b, g, i: (b, g, 0, 0)),
            pl.BlockSpec((None, None, nb, HEAD_DIM), lambda b, g, i: (b, g, 0, 0)),
            pl.BlockSpec((tq, LANES), lambda b, g, i: (b * nq + i, gate_blk + g)),
            pl.BlockSpec(ovt.shape, lambda b, g, i: (0, 0)),
        ],
        out_specs=[pl.BlockSpec((tq, gw), rowg), pl.BlockSpec((tq, LANES), rowg)],
        out_shape=[jax.ShapeDtypeStruct((T, B_WIDTH), BF16),
                   jax.ShapeDtypeStruct((T, G * LANES), BF16)],
        scratch_shapes=[pltpu.VMEM((HEADS_PER_GROUP * tq, HEAD_DIM), BF16)],
        compiler_params=_params("parallel", "parallel", "parallel"),
        name="cmp_attn",
    )(qn, kc, vc, proj_b, ovt)


def _sw_attn_kernel(q_ref, bias_ref, ksa_ref, vs_ref, kw_ref, vw_ref, oc_ref, gate_ref, o_ref,
                    qs_ref, m_ref, acc_ref, ss0_ref, ss1_ref, sw0_ref, sw1_ref, *, tq, tks, tkw):
    hpg = HEADS_PER_GROUP
    M = hpg * tq
    ss_ref = (ss0_ref, ss1_ref)
    sw_ref = (sw0_ref, sw1_ref)
    t0 = pl.program_id(2) * tq
    bias = bias_ref[...]
    for h in range(hpg):
        qs_ref[h * tq:(h + 1) * tq, :HEAD_DIM] = q_ref[:, h * HEAD_DIM:(h + 1) * HEAD_DIM]
        qs_ref[h * tq:(h + 1) * tq, HEAD_DIM:] = bias

    def reset():
        m_ref[...] = jnp.full_like(m_ref, NEG)
        acc_ref[...] = jnp.zeros_like(acc_ref)

    def step(s, v):
        m_old = m_ref[...]
        m_new = jnp.maximum(m_old, jnp.max(s, axis=-1, keepdims=True))
        alpha = jnp.exp2(m_old - m_new)
        p = jnp.exp2(s - jnp.tile(m_new, (1, s.shape[1] // LANES)))
        acc_ref[...] = (jnp.tile(alpha, (1, 2)) * acc_ref[...]
                        + jnp.dot(p.astype(BF16), v, preferred_element_type=F32))
        m_ref[...] = m_new

    def result():
        return acc_ref[:, :HEAD_DIM] * (1.0 / acc_ref[:, HEAD_DIM:])

    def masked(s, keep):
        tk = s.shape[1]
        return jnp.where(keep, s.reshape(hpg, tq, tk), NEG).reshape(M, tk)

    def positions(tk):
        t = t0 + lax.broadcasted_iota(jnp.int32, (1, tq, tk), 1)
        kpos = lax.broadcasted_iota(jnp.int32, (1, tq, tk), 2)
        return t, kpos

    def sel_base(j):
        return pl.multiple_of(j * tks, tks)

    def win_base(j):
        return pl.multiple_of(j * tkw, tkw)

    def sel_scores(j):
        base = sel_base(j)
        k = ksa_ref[pl.ds(base, tks), :]
        s = lax.dot_general(qs_ref[...], k, NT_DIMS, preferred_element_type=F32)
        t, kpos = positions(tks)
        return masked(s, kpos + base <= t)

    def win_scores(j):
        base = win_base(j)
        k = kw_ref[pl.ds(base, tkw), :]
        s = lax.dot_general(qs_ref[:, :HEAD_DIM], k, NT_DIMS, preferred_element_type=F32)
        t, kpos = positions(tkw)
        diff = (t - (kpos + base)).astype(jnp.uint32)
        return masked(s, diff < WINDOW)

    sel_last = (t0 + tq - 1) // tks
    win_last = (t0 + tq - 1) // tkw
    win_first = jnp.maximum(t0 - (WINDOW - 1), 0) // tkw

    def by_parity(n, fn):
        @pl.when(n % 2 == 0)
        def _():
            fn(0, 1)

        @pl.when(n % 2 == 1)
        def _():
            fn(1, 0)

    reset()
    ss_ref[0][...] = sel_scores(0)

    def sel_body(j, carry):
        def run(cur, nxt):
            ss_ref[nxt][...] = sel_scores(j + 1)
            step(ss_ref[cur][...], vs_ref[pl.ds(sel_base(j), tks), :])

        by_parity(j, run)
        return carry

    lax.fori_loop(0, sel_last, sel_body, 0)
    sw_ref[0][...] = win_scores(win_first)
    by_parity(sel_last,
              lambda cur, nxt: step(ss_ref[cur][...], vs_ref[pl.ds(sel_base(sel_last), tks), :]))
    o_s = result()

    reset()

    def win_body(j, carry):
        def run(cur, nxt):
            sw_ref[nxt][...] = win_scores(j + 1)
            step(sw_ref[cur][...], vw_ref[pl.ds(win_base(j), tkw), :])

        by_parity(j - win_first, run)
        return carry

    lax.fori_loop(win_first, win_last, win_body, 0)
    by_parity(win_last - win_first,
              lambda cur, nxt: step(sw_ref[cur][...], vw_ref[pl.ds(win_base(win_last), tkw), :]))
    o_w = result()

    sig = jax.nn.sigmoid(gate_ref[...])
    for h in range(hpg):
        rows = slice(h * tq, (h + 1) * tq)
        cols = slice(h * HEAD_DIM, (h + 1) * HEAD_DIM)
        o = (oc_ref[:, cols].astype(F32) + sig[:, hpg + h:hpg + h + 1] * o_s[rows]
             + sig[:, 2 * hpg + h:2 * hpg + h + 1] * o_w[rows])
        o_ref[:, cols] = o.astype(o_ref.dtype)


def _sw_attn(qn, bias, ksa, vs, kwn, vw, o_c, proj_b, *, batch, seq, tq=256, tks=512, tkw=256):
    T = qn.shape[0]
    G = N_KV_GROUPS
    nq = seq // tq
    gw = HEADS_PER_GROUP * HEAD_DIM
    M = HEADS_PER_GROUP * tq
    gate_blk = (B_WIDTH + 6 * KV_WIDTH) // LANES
    rowg = lambda b, g, i: (b * nq + i, g)
    seqg = lambda b, g, i: (b, g)
    return pl.pallas_call(
        functools.partial(_sw_attn_kernel, tq=tq, tks=tks, tkw=tkw),
        grid=(batch, G, nq),
        in_specs=[
            pl.BlockSpec((tq, gw), rowg),
            pl.BlockSpec((tq, LANES), rowg),
            pl.BlockSpec((seq, 2 * HEAD_DIM), seqg),
            pl.BlockSpec((seq, 2 * HEAD_DIM), seqg),
            pl.BlockSpec((seq, HEAD_DIM), seqg),
            pl.BlockSpec((seq, 2 * HEAD_DIM), seqg),
            pl.BlockSpec((tq, gw), rowg),
            pl.BlockSpec((tq, LANES), lambda b, g, i: (b * nq + i, gate_blk + g)),
        ],
        out_specs=pl.BlockSpec((tq, gw), rowg),
        out_shape=jax.ShapeDtypeStruct((T, B_WIDTH), BF16),
        scratch_shapes=[pltpu.VMEM((M, 2 * HEAD_DIM), BF16), pltpu.VMEM((M, LANES), F32),
                        pltpu.VMEM((M, 2 * HEAD_DIM), F32),
                        pltpu.VMEM((M, tks), F32), pltpu.VMEM((M, tks), F32),
                        pltpu.VMEM((M, tkw), F32), pltpu.VMEM((M, tkw), F32)],
        compiler_params=_params("parallel", "parallel", "parallel"),
        name="sw_attn",
    )(qn, bias, ksa, vs, kwn, vw, o_c, proj_b)


def _merge_kernel(oa_ref, ob_ref, ga_ref, gb_ref, wa_ref, wb_ref, o_ref):
    a = jnp.dot(oa_ref[...], wa_ref[...], preferred_element_type=F32)
    b = jnp.dot(ob_ref[...], wb_ref[...], preferred_element_type=F32)
    ga = jax.nn.sigmoid(ga_ref[...].astype(F32))
    gb = jax.nn.sigmoid(gb_ref[...].astype(F32))
    o_ref[...] = (ga * a + gb * b).astype(o_ref.dtype)


def _merge(o_a, o_b, proj_a, wa, wb, *, tm=1024, tn=512):
    T, D = o_a.shape
    N = wa.shape[1]
    nn = N // tn
    ga_blk = 2 * A_WIDTH // tn
    return pl.pallas_call(
        _merge_kernel,
        grid=(T // tm, nn),
        in_specs=[
            pl.BlockSpec((tm, D), lambda i, j: (i, 0)),
            pl.BlockSpec((tm, D), lambda i, j: (i, 0)),
            pl.BlockSpec((tm, tn), lambda i, j: (i, ga_blk + j)),
            pl.BlockSpec((tm, tn), lambda i, j: (i, ga_blk + nn + j)),
            pl.BlockSpec((D, tn), lambda i, j: (0, j)),
            pl.BlockSpec((D, tn), lambda i, j: (0, j)),
        ],
        out_specs=pl.BlockSpec((tm, tn), lambda i, j: (i, j)),
        out_shape=jax.ShapeDtypeStruct((T, N), BF16),
        compiler_params=_params("parallel", "arbitrary"),
        name="merge",
    )(o_a, o_b, proj_a, proj_a, wa, wb)


def _out_kernel(x_ref, m_ref, w_ref, o_ref):
    o_ref[...] = x_ref[...] + jnp.dot(m_ref[...], w_ref[...], preferred_element_type=F32)


def _out_proj(x, merged, w, *, tm=1024, tn=512):
    T, D = merged.shape
    N = w.shape[1]
    return pl.pallas_call(
        _out_kernel,
        grid=(T // tm, N // tn),
        in_specs=[
            pl.BlockSpec((tm, tn), lambda i, j: (i, j)),
            pl.BlockSpec((tm, D), lambda i, j: (i, 0)),
            pl.BlockSpec((D, tn), lambda i, j: (0, j)),
        ],
        out_specs=pl.BlockSpec((tm, tn), lambda i, j: (i, j)),
        out_shape=jax.ShapeDtypeStruct((T, N), F32),
        compiler_params=_params("parallel", "arbitrary"),
        name="out_proj",
    )(x, merged, w)


def _rope_tables(pos):
    half = ROPE_DIM // 2
    inv = ROPE_THETA ** (-2.0 * jnp.arange(half, dtype=F32) / ROPE_DIM)
    ang = pos.astype(F32)[:, None] * inv
    cos, sin = jnp.cos(ang), jnp.sin(ang)
    n = pos.shape[0]
    z16 = jnp.zeros((n, half), F32)
    rest = HEAD_DIM - ROPE_DIM
    c = jnp.concatenate([cos, cos, jnp.ones((n, rest), F32)], axis=1)
    su = jnp.concatenate([z16, sin, jnp.zeros((n, rest), F32)], axis=1)
    sd = jnp.concatenate([-sin, z16, jnp.zeros((n, rest), F32)], axis=1)
    return c, su, sd


def _split_w_in(w_in):
    sizes = [A_WIDTH, A_WIDTH, B_WIDTH] + [KV_WIDTH] * 6 + [3 * N_HEADS, D_MODEL, D_MODEL]
    offs = np.concatenate([[0], np.cumsum(sizes)])
    w = w_in.astype(BF16)
    w_a = jnp.concatenate([w[:, :offs[2]], w[:, offs[10]:]], axis=1)
    gates = w[:, offs[9]:offs[10]]
    gate_blocks = []
    for g in range(N_KV_GROUPS):
        cols = [gates[:, br * N_HEADS + g * HEADS_PER_GROUP:br * N_HEADS + (g + 1) * HEADS_PER_GROUP]
                for br in range(3)]
        pad = jnp.zeros((w.shape[0], LANES - 3 * HEADS_PER_GROUP), BF16)
        gate_blocks.extend(cols + [pad])
    w_b = jnp.concatenate([w[:, offs[2]:offs[9]]] + gate_blocks, axis=1)
    return w_a, w_b


def _overlap_t(nb_pad, n_sel):
    cmp_start = np.arange(nb_pad) * CMP_STRIDE
    sel_start = np.arange(n_sel) * SEL_BLOCK
    ov = ((cmp_start[None, :] < sel_start[:, None] + SEL_BLOCK)
          & (cmp_start[None, :] + CMP_BLOCK > sel_start[:, None]))
    return jnp.asarray(ov, dtype=BF16)


def _layer(x, ffn1_norm, ffn1_w_gate, ffn1_w_up, ffn1_w_down, mix_norm, w_in,
           a_v_norm, a_w_s, a_b_s, q_norm, k_cmp_norm, k_slc_norm, k_win_norm,
           cmp_k_pe, cmp_k_w1, cmp_k_w2, cmp_v_pe, cmp_v_w1, cmp_v_w2,
           w_branch_a, w_branch_b, w_out, ffn2_norm, ffn2_w_gate, ffn2_w_up, ffn2_w_down):
    B, S, D = x.shape
    T = B * S
    G = N_KV_GROUPS
    row = lambda v: v.reshape(1, -1)
    x0 = x.reshape(T, D)

    x1 = _ffn(x0, row(ffn1_norm), ffn1_w_gate.astype(BF16), ffn1_w_up.astype(BF16),
              ffn1_w_down.astype(BF16))

    w_a, w_b = _split_w_in(w_in)
    proj_a = _proj(x1, row(mix_norm), w_a, BF16, tm=1024, tn=1024)
    proj_b = _proj(x1, row(mix_norm), w_b, F32, tm=1024, tn=768)

    o_a = _gmlp(proj_a, row(a_v_norm), a_w_s, a_b_s.T)

    pos = jnp.arange(S)
    qn, ksa, vs, kwn, vw = _prep(proj_b, row(q_norm), row(k_slc_norm), row(k_win_norm),
                                 *_rope_tables(pos), seq=S)

    nb_pad = S // CMP_STRIDE
    cmp_end = jnp.arange(nb_pad) * CMP_STRIDE + (CMP_BLOCK - 1)
    kc, vc = _compress(proj_b, cmp_k_pe.reshape(1, -1), cmp_v_pe.reshape(1, -1),
                       cmp_k_w1.astype(BF16), cmp_k_w2.astype(BF16),
                       cmp_v_w1.astype(BF16), cmp_v_w2.astype(BF16),
                       row(k_cmp_norm), *_rope_tables(cmp_end), batch=B, seq=S)

    o_c, bias = _cmp_attn(qn, kc, vc, proj_b, _overlap_t(nb_pad, S // SEL_BLOCK), batch=B, seq=S)
    o_b = _sw_attn(qn, bias, ksa, vs, kwn, vw, o_c, proj_b, batch=B, seq=S)

    merged = _merge(o_a, o_b, proj_a, w_branch_a.astype(BF16), w_branch_b.astype(BF16))
    x2 = _out_proj(x1, merged, w_out.astype(BF16))

    x3 = _ffn(x2, row(ffn2_norm), ffn2_w_gate.astype(BF16), ffn2_w_up.astype(BF16),
              ffn2_w_down.astype(BF16))
    return x3.reshape(B, S, D)


def kernel(x, ffn1_norm, ffn1_w_gate, ffn1_w_up, ffn1_w_down, mix_norm, w_in, a_v_norm, a_w_s, a_b_s, q_norm, k_cmp_norm, k_slc_norm, k_win_norm, cmp_k_pe, cmp_k_w1, cmp_k_w2, cmp_v_pe, cmp_v_w1, cmp_v_w2, w_branch_a, w_branch_b, w_out, ffn2_norm, ffn2_w_gate, ffn2_w_up, ffn2_w_down):
    params = (ffn1_norm, ffn1_w_gate, ffn1_w_up, ffn1_w_down, mix_norm, w_in,
              a_v_norm, a_w_s, a_b_s, q_norm, k_cmp_norm, k_slc_norm, k_win_norm,
              cmp_k_pe, cmp_k_w1, cmp_k_w2, cmp_v_pe, cmp_v_w1, cmp_v_w2,
              w_branch_a, w_branch_b, w_out, ffn2_norm, ffn2_w_gate, ffn2_w_up, ffn2_w_down)
    for l in range(params[0].shape[0]):
        x = _layer(x, *[p[l] for p in params])
    return x
```

```python
import functools

import numpy as np
import jax
import jax.numpy as jnp
from jax import lax
from jax.experimental import pallas as pl
from jax.experimental.pallas import tpu as pltpu

D_MODEL = 2048
D_FF = 5504
A_WIDTH = 2048
A_GROUPS = 8
A_CHUNK = 128
N_HEADS = 16
HEAD_DIM = 128
N_KV_GROUPS = 2
HEADS_PER_GROUP = N_HEADS // N_KV_GROUPS
CMP_BLOCK = 32
CMP_STRIDE = 16
CMP_HIDDEN = 256
SEL_BLOCK = 64
SEL_TOP = 16
WINDOW = 512
ROPE_THETA = 500000.0
ROPE_DIM = HEAD_DIM // 4
EPS = 1e-6
NEG = -1e30
FORCE_BONUS = 1e4
LOG2_E = 1.4426950408889634
B_WIDTH = N_HEADS * HEAD_DIM
KV_WIDTH = N_KV_GROUPS * HEAD_DIM

LANES = 128
VMEM_LIMIT = 56 * 1024 * 1024

F32 = jnp.float32
BF16 = jnp.bfloat16
NT_DIMS = (((1,), (1,)), ((), ()))


def _params(*sem):
    return pltpu.CompilerParams(dimension_semantics=sem, vmem_limit_bytes=VMEM_LIMIT)


def _rms(x, g):
    ms = jnp.mean(x * x, axis=-1, keepdims=True)
    return x * lax.rsqrt(ms + EPS) * g


def _ffn_kernel(x_ref, g_ref, wg_ref, wu_ref, wd_ref, o_ref, h_ref, *, nf, tf, f_last):
    j = pl.program_id(1)

    def partial(width):
        h = h_ref[...]
        a = jnp.dot(h, wg_ref[:, :width].astype(BF16), preferred_element_type=F32)
        b = jnp.dot(h, wu_ref[:, :width].astype(BF16), preferred_element_type=F32)
        act = (a * jax.nn.sigmoid(a) * b).astype(BF16)
        return jnp.dot(act, wd_ref[:width, :].astype(BF16), preferred_element_type=F32)

    @pl.when(j == 0)
    def _():
        h_ref[...] = _rms(x_ref[...], g_ref[...]).astype(BF16)
        o_ref[...] = partial(f_last)

    @pl.when((j > 0) & (j < nf - 1))
    def _():
        o_ref[...] += partial(tf)

    @pl.when(j == nf - 1)
    def _():
        o_ref[...] = x_ref[...] + 0.5 * (o_ref[...] + partial(tf))


def _ffn(x, norm, wg, wu, wd, *, tm=1024, tf=256):
    T, D = x.shape
    F = wg.shape[1]
    nf = pl.cdiv(F, tf)
    f_last = F - (nf - 1) * tf
    hid = lambda j: (j + nf - 1) % nf
    return pl.pallas_call(
        functools.partial(_ffn_kernel, nf=nf, tf=tf, f_last=f_last),
        grid=(T // tm, nf),
        in_specs=[
            pl.BlockSpec((tm, D), lambda i, j: (i, 0)),
            pl.BlockSpec((1, D), lambda i, j: (0, 0)),
            pl.BlockSpec((D, tf), lambda i, j: (0, hid(j))),
            pl.BlockSpec((D, tf), lambda i, j: (0, hid(j))),
            pl.BlockSpec((tf, D), lambda i, j: (hid(j), 0)),
        ],
        out_specs=pl.BlockSpec((tm, D), lambda i, j: (i, 0)),
        out_shape=jax.ShapeDtypeStruct((T, D), F32),
        scratch_shapes=[pltpu.VMEM((tm, D), BF16)],
        compiler_params=_params("parallel", "arbitrary"),
        name="ffn",
    )(x, norm, wg, wu, wd)


def _proj_kernel(x_ref, g_ref, w_ref, o_ref, h_ref):
    @pl.when(pl.program_id(1) == 0)
    def _():
        h_ref[...] = _rms(x_ref[...], g_ref[...]).astype(BF16)

    o_ref[...] = jnp.dot(h_ref[...], w_ref[...], preferred_element_type=F32).astype(o_ref.dtype)


def _proj(x, norm, w, out_dtype, *, tm, tn):
    T, D = x.shape
    N = w.shape[1]
    return pl.pallas_call(
        _proj_kernel,
        grid=(T // tm, N // tn),
        in_specs=[
            pl.BlockSpec((tm, D), lambda i, j: (i, 0)),
            pl.BlockSpec((1, D), lambda i, j: (0, 0)),
            pl.BlockSpec((D, tn), lambda i, j: (0, j)),
        ],
        out_specs=pl.BlockSpec((tm, tn), lambda i, j: (i, j)),
        out_shape=jax.ShapeDtypeStruct((T, N), out_dtype),
        scratch_shapes=[pltpu.VMEM((tm, D), BF16)],
        compiler_params=_params("parallel", "arbitrary"),
        name="proj",
    )(x, norm, w)


def _gmlp_kernel(u_ref, v_ref, vn_ref, ws_ref, bs_ref, o_ref, *, tr):
    gw = A_WIDTH // A_GROUPS
    v = jax.nn.gelu(v_ref[...].astype(F32))
    vn = _rms(v, vn_ref[...]).astype(BF16)
    row = lax.broadcasted_iota(jnp.int32, (A_CHUNK, A_CHUNK), 0)
    col = lax.broadcasted_iota(jnp.int32, (A_CHUNK, A_CHUNK), 1)
    for g in range(A_GROUPS):
        w = jnp.where(col <= row, ws_ref[g], 0.0).astype(BF16)
        bias = bs_ref[:, g:g + 1]
        for c in range(tr // A_CHUNK):
            rows = slice(c * A_CHUNK, (c + 1) * A_CHUNK)
            cols = slice(g * gw, (g + 1) * gw)
            mixed = jnp.dot(w, vn[rows, cols], preferred_element_type=F32) + bias
            u = jax.nn.gelu(u_ref[rows, cols].astype(F32))
            o_ref[rows, cols] = (u * mixed).astype(o_ref.dtype)


def _gmlp(proj_a, a_v_norm, w_s, b_s_t, *, tr=512):
    T = proj_a.shape[0]
    return pl.pallas_call(
        functools.partial(_gmlp_kernel, tr=tr),
        grid=(T // tr,),
        in_specs=[
            pl.BlockSpec((tr, A_WIDTH), lambda i: (i, 0)),
            pl.BlockSpec((tr, A_WIDTH), lambda i: (i, 1)),
            pl.BlockSpec((1, A_WIDTH), lambda i: (0, 0)),
            pl.BlockSpec((A_GROUPS, A_CHUNK, A_CHUNK), lambda i: (0, 0, 0)),
            pl.BlockSpec((A_CHUNK, A_GROUPS), lambda i: (0, 0)),
        ],
        out_specs=pl.BlockSpec((tr, A_WIDTH), lambda i: (i, 0)),
        out_shape=jax.ShapeDtypeStruct((T, A_WIDTH), BF16),
        compiler_params=_params("parallel"),
        name="gmlp",
    )(proj_a, proj_a, a_v_norm, w_s, b_s_t)


def _rope(y, c, su, sd):
    return y * c + pltpu.roll(y, 16, 1) * su + pltpu.roll(y, HEAD_DIM - 16, 1) * sd


def _prep_kernel(q_ref, ks_ref, vs_ref, kw_ref, vw_ref, qg_ref, ksg_ref, kwg_ref,
                 c_ref, su_ref, sd_ref, qn_ref, ksa_ref, vso_ref, kwn_ref, vwo_ref, *, tr, seq):
    c, su, sd = c_ref[...], su_ref[...], sd_ref[...]
    scale = HEAD_DIM ** -0.5 * LOG2_E
    for h in range(N_HEADS):
        cols = slice(h * HEAD_DIM, (h + 1) * HEAD_DIM)
        y = _rope(_rms(q_ref[:, cols], qg_ref[...]), c, su, sd)
        qn_ref[:, cols] = (y * scale).astype(BF16)
    t0 = (pl.program_id(0) * tr) % seq
    t = t0 + lax.broadcasted_iota(jnp.int32, (tr, LANES), 0)
    lane = lax.broadcasted_iota(jnp.int32, (tr, LANES), 1)
    onehot = jnp.where((t // SEL_BLOCK) == lane, 1.0, 0.0).astype(BF16)
    ones = jnp.ones((tr, HEAD_DIM), BF16)
    for g in range(N_KV_GROUPS):
        cols = slice(g * HEAD_DIM, (g + 1) * HEAD_DIM)
        lo = slice(2 * g * HEAD_DIM, (2 * g + 1) * HEAD_DIM)
        hi = slice((2 * g + 1) * HEAD_DIM, (2 * g + 2) * HEAD_DIM)
        y = _rope(_rms(ks_ref[:, cols], ksg_ref[...]), c, su, sd)
        ksa_ref[:, lo] = y.astype(BF16)
        ksa_ref[:, hi] = onehot
        y = _rope(_rms(kw_ref[:, cols], kwg_ref[...]), c, su, sd)
        kwn_ref[:, cols] = y.astype(BF16)
        vso_ref[:, lo] = vs_ref[:, cols].astype(BF16)
        vso_ref[:, hi] = ones
        vwo_ref[:, lo] = vw_ref[:, cols].astype(BF16)
        vwo_ref[:, hi] = ones


def _prep(proj_b, q_norm, k_slc_norm, k_win_norm, rope_c, rope_su, rope_sd, *, seq, tr=512):
    T = proj_b.shape[0]
    kvb = B_WIDTH // KV_WIDTH
    nrt = seq // tr
    row = lambda i: (i, 0)
    tab = pl.BlockSpec((tr, HEAD_DIM), lambda i: (i % nrt, 0))
    gain = pl.BlockSpec((1, HEAD_DIM), lambda i: (0, 0))
    kv_in = lambda k: pl.BlockSpec((tr, KV_WIDTH), lambda i: (i, kvb + k))
    return pl.pallas_call(
        functools.partial(_prep_kernel, tr=tr, seq=seq),
        grid=(T // tr,),
        in_specs=[pl.BlockSpec((tr, B_WIDTH), row), kv_in(2), kv_in(3), kv_in(4), kv_in(5),
                  gain, gain, gain, tab, tab, tab],
        out_specs=[pl.BlockSpec((tr, B_WIDTH), row), pl.BlockSpec((tr, 2 * KV_WIDTH), row),
                   pl.BlockSpec((tr, 2 * KV_WIDTH), row), pl.BlockSpec((tr, KV_WIDTH), row),
                   pl.BlockSpec((tr, 2 * KV_WIDTH), row)],
        out_shape=[jax.ShapeDtypeStruct((T, B_WIDTH), BF16),
                   jax.ShapeDtypeStruct((T, 2 * KV_WIDTH), BF16),
                   jax.ShapeDtypeStruct((T, 2 * KV_WIDTH), BF16),
                   jax.ShapeDtypeStruct((T, KV_WIDTH), BF16),
                   jax.ShapeDtypeStruct((T, 2 * KV_WIDTH), BF16)],
        compiler_params=_params("parallel"),
        name="qk_prep",
    )(proj_b, proj_b, proj_b, proj_b, proj_b, q_norm, k_slc_norm, k_win_norm,
      rope_c, rope_su, rope_sd)


def _compress_kernel(zk_ref, zv_ref, pek_ref, pev_ref, w1k_ref, w2k_ref, w1v_ref, w2v_ref,
                     kg_ref, c_ref, su_ref, sd_ref, kc_ref, vc_ref, zr_ref):
    half = CMP_STRIDE * HEAD_DIM
    nb = zr_ref.shape[0]

    def comp(z_ref, pe_ref, w1_ref, w2_ref):
        for l in range(CMP_STRIDE):
            zr_ref[:, l * HEAD_DIM:(l + 1) * HEAD_DIM] = z_ref[pl.ds(l, nb, stride=CMP_STRIDE), :]
        z = zr_ref[...]
        top = jnp.dot((z + pe_ref[:, :half]).astype(BF16), w1_ref[:half, :], preferred_element_type=F32)
        bot = jnp.dot((z + pe_ref[:, half:]).astype(BF16), w1_ref[half:, :], preferred_element_type=F32)
        hid = top + pltpu.roll(bot, nb - 1, 0)
        return jnp.dot(jax.nn.gelu(hid).astype(BF16), w2_ref[...], preferred_element_type=F32)

    k = comp(zk_ref, pek_ref, w1k_ref, w2k_ref)
    kc_ref[...] = _rope(_rms(k, kg_ref[...]), c_ref[...], su_ref[...], sd_ref[...]).astype(BF16)
    vc_ref[...] = comp(zv_ref, pev_ref, w1v_ref, w2v_ref).astype(BF16)


def _compress(proj_b, pek, pev, w1k, w2k, w1v, w2v, k_cmp_norm, cmp_c, cmp_su, cmp_sd, *, batch, seq):
    G = N_KV_GROUPS
    nb = seq // CMP_STRIDE
    kc_blk = B_WIDTH // HEAD_DIM
    vc_blk = kc_blk + G
    ospec = pl.BlockSpec((None, None, nb, HEAD_DIM), lambda b, g: (b, g, 0, 0))
    full = lambda a: pl.BlockSpec(a.shape, lambda b, g: (0,) * a.ndim)
    consts = (pek, pev, w1k, w2k, w1v, w2v, k_cmp_norm, cmp_c, cmp_su, cmp_sd)
    return pl.pallas_call(
        _compress_kernel,
        grid=(batch, G),
        in_specs=[pl.BlockSpec((seq, HEAD_DIM), lambda b, g: (b, kc_blk + g)),
                  pl.BlockSpec((seq, HEAD_DIM), lambda b, g: (b, vc_blk + g))]
                 + [full(a) for a in consts],
        out_specs=[ospec, ospec],
        out_shape=[jax.ShapeDtypeStruct((batch, G, nb, HEAD_DIM), BF16)] * 2,
        scratch_shapes=[pltpu.VMEM((nb, CMP_STRIDE * HEAD_DIM), F32)],
        compiler_params=_params("parallel", "parallel"),
        name="compress",
    )(proj_b, proj_b, *consts)


def _cmp_attn_kernel(q_ref, kc_ref, vc_ref, gate_ref, ovt_ref, oc_ref, bias_ref, qs_ref, *, tq):
    hpg = HEADS_PER_GROUP
    nb = kc_ref.shape[0]
    n_sel = ovt_ref.shape[0]
    t0 = pl.program_id(2) * tq
    for h in range(hpg):
        qs_ref[h * tq:(h + 1) * tq, :] = q_ref[:, h * HEAD_DIM:(h + 1) * HEAD_DIM]
    s = lax.dot_general(qs_ref[...], kc_ref[...], NT_DIMS, preferred_element_type=F32)
    s = s.reshape(hpg, tq, nb)
    t = t0 + lax.broadcasted_iota(jnp.int32, (1, tq, nb), 1)
    n = lax.broadcasted_iota(jnp.int32, (1, tq, nb), 2)
    mask = n * CMP_STRIDE + (CMP_BLOCK - 1) <= t
    s = jnp.where(mask, s, NEG)
    e = jnp.exp2(s - jnp.max(s, axis=-1, keepdims=True))
    inv = 1.0 / jnp.sum(e, axis=-1, keepdims=True)
    p = jnp.where(mask, e * inv, 0.0)
    o = jnp.dot(p.reshape(hpg * tq, nb).astype(BF16), vc_ref[...], preferred_element_type=F32)
    sig = jax.nn.sigmoid(gate_ref[...])
    for h in range(hpg):
        oc_ref[:, h * HEAD_DIM:(h + 1) * HEAD_DIM] = (
            o[h * tq:(h + 1) * tq] * sig[:, h:h + 1]).astype(oc_ref.dtype)

    psum = jnp.sum(p, axis=0)
    ovt = ovt_ref[...]
    hi = psum.astype(BF16)
    r1 = psum - hi.astype(F32)
    mid = r1.astype(BF16)
    lo = (r1 - mid.astype(F32)).astype(BF16)
    imp = (lax.dot_general(ovt, hi, NT_DIMS, preferred_element_type=F32)
           + lax.dot_general(ovt, mid, NT_DIMS, preferred_element_type=F32)
           + lax.dot_general(ovt, lo, NT_DIMS, preferred_element_type=F32))
    j = lax.broadcasted_iota(jnp.int32, (n_sel, tq), 0)
    tt = t0 + lax.broadcasted_iota(jnp.int32, (n_sel, tq), 1)
    cur = tt // SEL_BLOCK
    forced = jnp.where((j == 0) | (j == cur) | (j == cur - 1), FORCE_BONUS, 0.0)
    score = jnp.where(j * SEL_BLOCK <= tt, imp + forced, NEG)
    sub = 8
    groups = [score[k * sub:(k + 1) * sub, :] for k in range(n_sel // sub)]
    ranks = [jnp.zeros((sub, tq), F32) for _ in groups]
    jrow = lax.broadcasted_iota(jnp.int32, (sub, tq), 0)
    for i in range(n_sel):
        r = jnp.broadcast_to(score[i:i + 1, :], (sub, tq))
        for k, grp in enumerate(groups):
            if k * sub > i:
                beats = r >= grp
            elif (k + 1) * sub <= i:
                beats = r > grp
            else:
                beats = jnp.where(jrow + k * sub > i, jnp.where(r >= grp, 1.0, 0.0),
                                  jnp.where(r > grp, 1.0, 0.0)) > 0.5
            ranks[k] = ranks[k] + jnp.where(beats, 1.0, 0.0)
    rank = jnp.concatenate(ranks, axis=0)
    bias = jnp.where(rank < SEL_TOP, 0.0, NEG)
    bias = jnp.concatenate([bias, jnp.zeros((LANES - n_sel, tq), F32)], axis=0)
    bias_ref[...] = bias.T.astype(bias_ref.dtype)


def _cmp_attn(qn, kc, vc, proj_b, ovt, *, batch, seq, tq=128):
    T = qn.shape[0]
    G = N_KV_GROUPS
    nq = seq // tq
    gw = HEADS_PER_GROUP * HEAD_DIM
    nb = kc.shape[2]
    gate_blk = (B_WIDTH + 6 * KV_WIDTH) // LANES
    rowg = lambda b, g, i: (b * nq + i, g)
    return pl.pallas_call(
        functools.partial(_cmp_attn_kernel, tq=tq),
        grid=(batch, G, nq),
        in_specs=[
            pl.BlockSpec((tq, gw), rowg),
            pl.BlockSpec((None, None, nb, HEAD_DIM), lambda b, g, i: (b, g, 0, 0)),
            pl.BlockSpec((None, None, nb, HEAD_DIM), lambda b, g, i: (b, g, 0, 0)),
            pl.BlockSpec((tq, LANES), lambda b, g, i: (b * nq + i, gate_blk + g)),
            pl.BlockSpec(ovt.shape, lambda b, g, i: (0, 0)),
        ],
        out_specs=[pl.BlockSpec((tq, gw), rowg), pl.BlockSpec((tq, LANES), rowg)],
        out_shape=[jax.ShapeDtypeStruct((T, B_WIDTH), BF16),
                   jax.ShapeDtypeStruct((T, G * LANES), BF16)],
        scratch_shapes=[pltpu.VMEM((HEADS_PER_GROUP * tq, HEAD_DIM), BF16)],
        compiler_params=_params("parallel", "parallel", "parallel"),
        name="cmp_attn",
    )(qn, kc, vc, proj_b, ovt)


def _sw_attn_kernel(q_ref, bias_ref, ksa_ref, vs_ref, kw_ref, vw_ref, oc_ref, gate_ref, o_ref,
                    qs_ref, m_ref, acc_ref, ss0_ref, ss1_ref, sw0_ref, sw1_ref, *, tq, tks, tkw):
    hpg = HEADS_PER_GROUP
    M = hpg * tq
    ss_ref = (ss0_ref, ss1_ref)
    sw_ref = (sw0_ref, sw1_ref)
    t0 = pl.program_id(2) * tq
    bias = bias_ref[...]
    for h in range(hpg):
        qs_ref[h * tq:(h + 1) * tq, :HEAD_DIM] = q_ref[:, h * HEAD_DIM:(h + 1) * HEAD_DIM]
        qs_ref[h * tq:(h + 1) * tq, HEAD_DIM:] = bias

    def reset():
        m_ref[...] = jnp.full_like(m_ref, NEG)
        acc_ref[...] = jnp.zeros_like(acc_ref)

    def step(s, v):
        m_old = m_ref[...]
        m_new = jnp.maximum(m_old, jnp.max(s, axis=-1, keepdims=True))
        alpha = jnp.exp2(m_old - m_new)
        p = jnp.exp2(s - jnp.tile(m_new, (1, s.shape[1] // LANES)))
        acc_ref[...] = (jnp.tile(alpha, (1, 2)) * acc_ref[...]
                        + jnp.dot(p.astype(BF16), v, preferred_element_type=F32))
        m_ref[...] = m_new

    def result():
        return acc_ref[:, :HEAD_DIM] * (1.0 / acc_ref[:, HEAD_DIM:])

    def masked(s, keep):
        tk = s.shape[1]
        return jnp.where(keep, s.reshape(hpg, tq, tk), NEG).reshape(M, tk)

    def positions(tk):
        t = t0 + lax.broadcasted_iota(jnp.int32, (1, tq, tk), 1)
        kpos = lax.broadcasted_iota(jnp.int32, (1, tq, tk), 2)
        return t, kpos

    def sel_base(j):
        return pl.multiple_of(j * tks, tks)

    def win_base(j):
        return pl.multiple_of(j * tkw, tkw)

    def sel_scores(j):
        base = sel_base(j)
        k = ksa_ref[pl.ds(base, tks), :]
        s = lax.dot_general(qs_ref[...], k, NT_DIMS, preferred_element_type=F32)
        t, kpos = positions(tks)
        return masked(s, kpos + base <= t)

    def win_scores(j):
        base = win_base(j)
        k = kw_ref[pl.ds(base, tkw), :]
        s = lax.dot_general(qs_ref[:, :HEAD_DIM], k, NT_DIMS, preferred_element_type=F32)
        t, kpos = positions(tkw)
        diff = (t - (kpos + base)).astype(jnp.uint32)
        return masked(s, diff < WINDOW)

    sel_last = (t0 + tq - 1) // tks
    win_last = (t0 + tq - 1) // tkw
    win_first = jnp.maximum(t0 - (WINDOW - 1), 0) // tkw

    def by_parity(n, fn):
        @pl.when(n % 2 == 0)
        def _():
            fn(0, 1)

        @pl.when(n % 2 == 1)
        def _():
            fn(1, 0)

    reset()
    ss_ref[0][...] = sel_scores(0)

    def sel_body(j, carry):
        def run(cur, nxt):
            ss_ref[nxt][...] = sel_scores(j + 1)
            step(ss_ref[cur][...], vs_ref[pl.ds(sel_base(j), tks), :])

        by_parity(j, run)
        return carry

    lax.fori_loop(0, sel_last, sel_body, 0)
    sw_ref[0][...] = win_scores(win_first)
    by_parity(sel_last,
              lambda cur, nxt: step(ss_ref[cur][...], vs_ref[pl.ds(sel_base(sel_last), tks), :]))
    o_s = result()

    reset()

    def win_body(j, carry):
        def run(cur, nxt):
            sw_ref[nxt][...] = win_scores(j + 1)
            step(sw_ref[cur][...], vw_ref[pl.ds(win_base(j), tkw), :])

        by_parity(j - win_first, run)
        return carry

    lax.fori_loop(win_first, win_last, win_body, 0)
    by_parity(win_last - win_first,
              lambda cur, nxt: step(sw_ref[cur][...], vw_ref[pl.ds(win_base(win_last), tkw), :]))
    o_w = result()

    sig = jax.nn.sigmoid(gate_ref[...])
    for h in range(hpg):
        rows = slice(h * tq, (h + 1) * tq)
        cols = slice(h * HEAD_DIM, (h + 1) * HEAD_DIM)
        o = (oc_ref[:, cols].astype(F32) + sig[:, hpg + h:hpg + h + 1] * o_s[rows]
             + sig[:, 2 * hpg + h:2 * hpg + h + 1] * o_w[rows])
        o_ref[:, cols] = o.astype(o_ref.dtype)


def _sw_attn(qn, bias, ksa, vs, kwn, vw, o_c, proj_b, *, batch, seq, tq=256, tks=512, tkw=256):
    T = qn.shape[0]
    G = N_KV_GROUPS
    nq = seq // tq
    gw = HEADS_PER_GROUP * HEAD_DIM
    M = HEADS_PER_GROUP * tq
    gate_blk = (B_WIDTH + 6 * KV_WIDTH) // LANES
    rowg = lambda b, g, i: (b * nq + i, g)
    seqg = lambda b, g, i: (b, g)
    return pl.pallas_call(
        functools.partial(_sw_attn_kernel, tq=tq, tks=tks, tkw=tkw),
        grid=(batch, G, nq),
        in_specs=[
            pl.BlockSpec((tq, gw), rowg),
            pl.BlockSpec((tq, LANES), rowg),
            pl.BlockSpec((seq, 2 * HEAD_DIM), seqg),
            pl.BlockSpec((seq, 2 * HEAD_DIM), seqg),
            pl.BlockSpec((seq, HEAD_DIM), seqg),
            pl.BlockSpec((seq, 2 * HEAD_DIM), seqg),
            pl.BlockSpec((tq, gw), rowg),
            pl.BlockSpec((tq, LANES), lambda b, g, i: (b * nq + i, gate_blk + g)),
        ],
        out_specs=pl.BlockSpec((tq, gw), rowg),
        out_shape=jax.ShapeDtypeStruct((T, B_WIDTH), BF16),
        scratch_shapes=[pltpu.VMEM((M, 2 * HEAD_DIM), BF16), pltpu.VMEM((M, LANES), F32),
                        pltpu.VMEM((M, 2 * HEAD_DIM), F32),
                        pltpu.VMEM((M, tks), F32), pltpu.VMEM((M, tks), F32),
                        pltpu.VMEM((M, tkw), F32), pltpu.VMEM((M, tkw), F32)],
        compiler_params=_params("parallel", "parallel", "parallel"),
        name="sw_attn",
    )(qn, bias, ksa, vs, kwn, vw, o_c, proj_b)


def _merge_kernel(oa_ref, ob_ref, ga_ref, gb_ref, wa_ref, wb_ref, o_ref):
    a = jnp.dot(oa_ref[...], wa_ref[...].astype(BF16), preferred_element_type=F32)
    b = jnp.dot(ob_ref[...], wb_ref[...].astype(BF16), preferred_element_type=F32)
    ga = jax.nn.sigmoid(ga_ref[...].astype(F32))
    gb = jax.nn.sigmoid(gb_ref[...].astype(F32))
    o_ref[...] = (ga * a + gb * b).astype(o_ref.dtype)


def _merge(o_a, o_b, proj_a, wa, wb, *, tm=1024, tn=512):
    T, D = o_a.shape
    N = wa.shape[1]
    nn = N // tn
    ga_blk = 2 * A_WIDTH // tn
    return pl.pallas_call(
        _merge_kernel,
        grid=(T // tm, nn),
        in_specs=[
            pl.BlockSpec((tm, D), lambda i, j: (i, 0)),
            pl.BlockSpec((tm, D), lambda i, j: (i, 0)),
            pl.BlockSpec((tm, tn), lambda i, j: (i, ga_blk + j)),
            pl.BlockSpec((tm, tn), lambda i, j: (i, ga_blk + nn + j)),
            pl.BlockSpec((D, tn), lambda i, j: (0, j)),
            pl.BlockSpec((D, tn), lambda i, j: (0, j)),
        ],
        out_specs=pl.BlockSpec((tm, tn), lambda i, j: (i, j)),
        out_shape=jax.ShapeDtypeStruct((T, N), BF16),
        compiler_params=_params("parallel", "arbitrary"),
        name="merge",
    )(o_a, o_b, proj_a, proj_a, wa, wb)


def _out_kernel(x_ref, m_ref, w_ref, o_ref):
    o_ref[...] = x_ref[...] + jnp.dot(m_ref[...], w_ref[...].astype(BF16),
                                      preferred_element_type=F32)


def _out_proj(x, merged, w, *, tm=1024, tn=512):
    T, D = merged.shape
    N = w.shape[1]
    return pl.pallas_call(
        _out_kernel,
        grid=(T // tm, N // tn),
        in_specs=[
            pl.BlockSpec((tm, tn), lambda i, j: (i, j)),
            pl.BlockSpec((tm, D), lambda i, j: (i, 0)),
            pl.BlockSpec((D, tn), lambda i, j: (0, j)),
        ],
        out_specs=pl.BlockSpec((tm, tn), lambda i, j: (i, j)),
        out_shape=jax.ShapeDtypeStruct((T, N), F32),
        compiler_params=_params("parallel", "arbitrary"),
        name="out_proj",
    )(x, merged, w)


def _rope_tables(pos):
    half = ROPE_DIM // 2
    inv = ROPE_THETA ** (-2.0 * jnp.arange(half, dtype=F32) / ROPE_DIM)
    ang = pos.astype(F32)[:, None] * inv
    cos, sin = jnp.cos(ang), jnp.sin(ang)
    n = pos.shape[0]
    z16 = jnp.zeros((n, half), F32)
    rest = HEAD_DIM - ROPE_DIM
    c = jnp.concatenate([cos, cos, jnp.ones((n, rest), F32)], axis=1)
    su = jnp.concatenate([z16, sin, jnp.zeros((n, rest), F32)], axis=1)
    sd = jnp.concatenate([-sin, z16, jnp.zeros((n, rest), F32)], axis=1)
    return c, su, sd


def _split_w_in(w_in):
    sizes = [A_WIDTH, A_WIDTH, B_WIDTH] + [KV_WIDTH] * 6 + [3 * N_HEADS, D_MODEL, D_MODEL]
    offs = np.concatenate([[0], np.cumsum(sizes)])
    w = w_in.astype(BF16)
    w_a = jnp.concatenate([w[:, :offs[2]], w[:, offs[10]:]], axis=1)
    gates = w[:, offs[9]:offs[10]]
    gate_blocks = []
    for g in range(N_KV_GROUPS):
        cols = [gates[:, br * N_HEADS + g * HEADS_PER_GROUP:br * N_HEADS + (g + 1) * HEADS_PER_GROUP]
                for br in range(3)]
        pad = jnp.zeros((w.shape[0], LANES - 3 * HEADS_PER_GROUP), BF16)
        gate_blocks.extend(cols + [pad])
    w_b = jnp.concatenate([w[:, offs[2]:offs[9]]] + gate_blocks, axis=1)
    return w_a, w_b


def _overlap_t(nb_pad, n_sel):
    cmp_start = np.arange(nb_pad) * CMP_STRIDE
    sel_start = np.arange(n_sel) * SEL_BLOCK
    ov = ((cmp_start[None, :] < sel_start[:, None] + SEL_BLOCK)
          & (cmp_start[None, :] + CMP_BLOCK > sel_start[:, None]))
    return jnp.asarray(ov, dtype=BF16)


def _layer(x, ffn1_norm, ffn1_w_gate, ffn1_w_up, ffn1_w_down, mix_norm, w_in,
           a_v_norm, a_w_s, a_b_s, q_norm, k_cmp_norm, k_slc_norm, k_win_norm,
           cmp_k_pe, cmp_k_w1, cmp_k_w2, cmp_v_pe, cmp_v_w1, cmp_v_w2,
           w_branch_a, w_branch_b, w_out, ffn2_norm, ffn2_w_gate, ffn2_w_up, ffn2_w_down):
    B, S, D = x.shape
    T = B * S
    G = N_KV_GROUPS
    row = lambda v: v.reshape(1, -1)
    x0 = x.reshape(T, D)

    x1 = _ffn(x0, row(ffn1_norm), ffn1_w_gate, ffn1_w_up, ffn1_w_down)

    w_a, w_b = _split_w_in(w_in)
    proj_a = _proj(x1, row(mix_norm), w_a, BF16, tm=1024, tn=1024)
    proj_b = _proj(x1, row(mix_norm), w_b, F32, tm=1024, tn=768)

    o_a = _gmlp(proj_a, row(a_v_norm), a_w_s, a_b_s.T)

    pos = jnp.arange(S)
    qn, ksa, vs, kwn, vw = _prep(proj_b, row(q_norm), row(k_slc_norm), row(k_win_norm),
                                 *_rope_tables(pos), seq=S)

    nb_pad = S // CMP_STRIDE
    cmp_end = jnp.arange(nb_pad) * CMP_STRIDE + (CMP_BLOCK - 1)
    kc, vc = _compress(proj_b, cmp_k_pe.reshape(1, -1), cmp_v_pe.reshape(1, -1),
                       cmp_k_w1.astype(BF16), cmp_k_w2.astype(BF16),
                       cmp_v_w1.astype(BF16), cmp_v_w2.astype(BF16),
                       row(k_cmp_norm), *_rope_tables(cmp_end), batch=B, seq=S)

    o_c, bias = _cmp_attn(qn, kc, vc, proj_b, _overlap_t(nb_pad, S // SEL_BLOCK), batch=B, seq=S)
    o_b = _sw_attn(qn, bias, ksa, vs, kwn, vw, o_c, proj_b, batch=B, seq=S)

    merged = _merge(o_a, o_b, proj_a, w_branch_a, w_branch_b)
    x2 = _out_proj(x1, merged, w_out)

    x3 = _ffn(x2, row(ffn2_norm), ffn2_w_gate, ffn2_w_up, ffn2_w_down)
    return x3.reshape(B, S, D)


def kernel(x, ffn1_norm, ffn1_w_gate, ffn1_w_up, ffn1_w_down, mix_norm, w_in, a_v_norm, a_w_s, a_b_s, q_norm, k_cmp_norm, k_slc_norm, k_win_norm, cmp_k_pe, cmp_k_w1, cmp_k_w2, cmp_v_pe, cmp_v_w1, cmp_v_w2, w_branch_a, w_branch_b, w_out, ffn2_norm, ffn2_w_gate, ffn2_w_up, ffn2_w_down):
    params = (ffn1_norm, ffn1_w_gate, ffn1_w_up, ffn1_w_down, mix_norm, w_in,
              a_v_norm, a_w_s, a_b_s, q_norm, k_cmp_norm, k_slc_norm, k_win_norm,
              cmp_k_pe, cmp_k_w1, cmp_k_w2, cmp_v_pe, cmp_v_w1, cmp_v_w2,
              w_branch_a, w_branch_b, w_out, ffn2_norm, ffn2_w_gate, ffn2_w_up, ffn2_w_down)
    for l in range(params[0].shape[0]):
        x = _layer(x, *[p[l] for p in params])
    return x
```

```python
import functools

import numpy as np
import jax
import jax.numpy as jnp
from jax import lax
from jax.experimental import pallas as pl
from jax.experimental.pallas import tpu as pltpu

D_MODEL = 2048
D_FF = 5504
A_WIDTH = 2048
A_GROUPS = 8
A_CHUNK = 128
N_HEADS = 16
HEAD_DIM = 128
N_KV_GROUPS = 2
HEADS_PER_GROUP = N_HEADS // N_KV_GROUPS
CMP_BLOCK = 32
CMP_STRIDE = 16
CMP_HIDDEN = 256
SEL_BLOCK = 64
SEL_TOP = 16
WINDOW = 512
ROPE_THETA = 500000.0
ROPE_DIM = HEAD_DIM // 4
EPS = 1e-6
NEG = -1e30
FORCE_BONUS = 1e4
LOG2_E = 1.4426950408889634
B_WIDTH = N_HEADS * HEAD_DIM
KV_WIDTH = N_KV_GROUPS * HEAD_DIM

LANES = 128
VMEM_LIMIT = 56 * 1024 * 1024

F32 = jnp.float32
BF16 = jnp.bfloat16
NT_DIMS = (((1,), (1,)), ((), ()))


def _params(*sem):
    return pltpu.CompilerParams(dimension_semantics=sem, vmem_limit_bytes=VMEM_LIMIT)


def _rms(x, g):
    ms = jnp.mean(x * x, axis=-1, keepdims=True)
    return x * lax.rsqrt(ms + EPS) * g


def _ffn_kernel(x_ref, g_ref, wg_ref, wu_ref, wd_ref, o_ref, h_ref, *, nf, tf, f_last):
    j = pl.program_id(1)

    def partial(width):
        h = h_ref[...]
        a = jnp.dot(h, wg_ref[:, :width].astype(BF16), preferred_element_type=F32)
        b = jnp.dot(h, wu_ref[:, :width].astype(BF16), preferred_element_type=F32)
        act = (a * jax.nn.sigmoid(a) * b).astype(BF16)
        return jnp.dot(act, wd_ref[:width, :].astype(BF16), preferred_element_type=F32)

    @pl.when(j == 0)
    def _():
        h_ref[...] = _rms(x_ref[...], g_ref[...]).astype(BF16)
        o_ref[...] = partial(f_last)

    @pl.when((j > 0) & (j < nf - 1))
    def _():
        o_ref[...] += partial(tf)

    @pl.when(j == nf - 1)
    def _():
        o_ref[...] = x_ref[...] + 0.5 * (o_ref[...] + partial(tf))


def _ffn(x, norm, wg, wu, wd, *, tm=1024, tf=256):
    T, D = x.shape
    F = wg.shape[1]
    nf = pl.cdiv(F, tf)
    f_last = F - (nf - 1) * tf
    hid = lambda j: (j + nf - 1) % nf
    return pl.pallas_call(
        functools.partial(_ffn_kernel, nf=nf, tf=tf, f_last=f_last),
        grid=(T // tm, nf),
        in_specs=[
            pl.BlockSpec((tm, D), lambda i, j: (i, 0)),
            pl.BlockSpec((1, D), lambda i, j: (0, 0)),
            pl.BlockSpec((D, tf), lambda i, j: (0, hid(j))),
            pl.BlockSpec((D, tf), lambda i, j: (0, hid(j))),
            pl.BlockSpec((tf, D), lambda i, j: (hid(j), 0)),
        ],
        out_specs=pl.BlockSpec((tm, D), lambda i, j: (i, 0)),
        out_shape=jax.ShapeDtypeStruct((T, D), F32),
        scratch_shapes=[pltpu.VMEM((tm, D), BF16)],
        compiler_params=_params("parallel", "arbitrary"),
        name="ffn",
    )(x, norm, wg, wu, wd)


def _proj_kernel(x_ref, g_ref, w_ref, o_ref, h_ref):
    @pl.when(pl.program_id(1) == 0)
    def _():
        h_ref[...] = _rms(x_ref[...], g_ref[...]).astype(BF16)

    o_ref[...] = jnp.dot(h_ref[...], w_ref[...], preferred_element_type=F32).astype(o_ref.dtype)


def _proj(x, norm, w, out_dtype, *, tm, tn):
    T, D = x.shape
    N = w.shape[1]
    return pl.pallas_call(
        _proj_kernel,
        grid=(T // tm, N // tn),
        in_specs=[
            pl.BlockSpec((tm, D), lambda i, j: (i, 0)),
            pl.BlockSpec((1, D), lambda i, j: (0, 0)),
            pl.BlockSpec((D, tn), lambda i, j: (0, j)),
        ],
        out_specs=pl.BlockSpec((tm, tn), lambda i, j: (i, j)),
        out_shape=jax.ShapeDtypeStruct((T, N), out_dtype),
        scratch_shapes=[pltpu.VMEM((tm, D), BF16)],
        compiler_params=_params("parallel", "arbitrary"),
        name="proj",
    )(x, norm, w)


def _gmlp_kernel(u_ref, v_ref, vn_ref, ws_ref, bs_ref, o_ref, *, tr):
    gw = A_WIDTH // A_GROUPS
    v = jax.nn.gelu(v_ref[...].astype(F32))
    vn = _rms(v, vn_ref[...]).astype(BF16)
    row = lax.broadcasted_iota(jnp.int32, (A_CHUNK, A_CHUNK), 0)
    col = lax.broadcasted_iota(jnp.int32, (A_CHUNK, A_CHUNK), 1)
    for g in range(A_GROUPS):
        w = jnp.where(col <= row, ws_ref[g], 0.0).astype(BF16)
        bias = bs_ref[:, g:g + 1]
        for c in range(tr // A_CHUNK):
            rows = slice(c * A_CHUNK, (c + 1) * A_CHUNK)
            cols = slice(g * gw, (g + 1) * gw)
            mixed = jnp.dot(w, vn[rows, cols], preferred_element_type=F32) + bias
            u = jax.nn.gelu(u_ref[rows, cols].astype(F32))
            o_ref[rows, cols] = (u * mixed).astype(o_ref.dtype)


def _gmlp(proj_a, a_v_norm, w_s, b_s_t, *, tr=512):
    T = proj_a.shape[0]
    return pl.pallas_call(
        functools.partial(_gmlp_kernel, tr=tr),
        grid=(T // tr,),
        in_specs=[
            pl.BlockSpec((tr, A_WIDTH), lambda i: (i, 0)),
            pl.BlockSpec((tr, A_WIDTH), lambda i: (i, 1)),
            pl.BlockSpec((1, A_WIDTH), lambda i: (0, 0)),
            pl.BlockSpec((A_GROUPS, A_CHUNK, A_CHUNK), lambda i: (0, 0, 0)),
            pl.BlockSpec((A_CHUNK, A_GROUPS), lambda i: (0, 0)),
        ],
        out_specs=pl.BlockSpec((tr, A_WIDTH), lambda i: (i, 0)),
        out_shape=jax.ShapeDtypeStruct((T, A_WIDTH), BF16),
        compiler_params=_params("parallel"),
        name="gmlp",
    )(proj_a, proj_a, a_v_norm, w_s, b_s_t)


def _rope(y, c, su, sd):
    return y * c + pltpu.roll(y, 16, 1) * su + pltpu.roll(y, HEAD_DIM - 16, 1) * sd


def _prep_kernel(q_ref, ks_ref, vs_ref, kw_ref, vw_ref, qg_ref, ksg_ref, kwg_ref,
                 c_ref, su_ref, sd_ref, qn_ref, ksa_ref, vso_ref, kwn_ref, vwo_ref, *, tr, seq):
    c, su, sd = c_ref[...], su_ref[...], sd_ref[...]
    scale = HEAD_DIM ** -0.5 * LOG2_E
    for h in range(N_HEADS):
        cols = slice(h * HEAD_DIM, (h + 1) * HEAD_DIM)
        y = _rope(_rms(q_ref[:, cols], qg_ref[...]), c, su, sd)
        qn_ref[:, cols] = (y * scale).astype(BF16)
    t0 = (pl.program_id(0) * tr) % seq
    t = t0 + lax.broadcasted_iota(jnp.int32, (tr, LANES), 0)
    lane = lax.broadcasted_iota(jnp.int32, (tr, LANES), 1)
    onehot = jnp.where((t // SEL_BLOCK) == lane, 1.0, 0.0).astype(BF16)
    ones = jnp.ones((tr, HEAD_DIM), BF16)
    for g in range(N_KV_GROUPS):
        cols = slice(g * HEAD_DIM, (g + 1) * HEAD_DIM)
        lo = slice(2 * g * HEAD_DIM, (2 * g + 1) * HEAD_DIM)
        hi = slice((2 * g + 1) * HEAD_DIM, (2 * g + 2) * HEAD_DIM)
        y = _rope(_rms(ks_ref[:, cols], ksg_ref[...]), c, su, sd)
        ksa_ref[:, lo] = y.astype(BF16)
        ksa_ref[:, hi] = onehot
        y = _rope(_rms(kw_ref[:, cols], kwg_ref[...]), c, su, sd)
        kwn_ref[:, cols] = y.astype(BF16)
        vso_ref[:, lo] = vs_ref[:, cols].astype(BF16)
        vso_ref[:, hi] = ones
        vwo_ref[:, lo] = vw_ref[:, cols].astype(BF16)
        vwo_ref[:, hi] = ones


def _prep(proj_b, q_norm, k_slc_norm, k_win_norm, rope_c, rope_su, rope_sd, *, seq, tr=512):
    T = proj_b.shape[0]
    kvb = B_WIDTH // KV_WIDTH
    nrt = seq // tr
    row = lambda i: (i, 0)
    tab = pl.BlockSpec((tr, HEAD_DIM), lambda i: (i % nrt, 0))
    gain = pl.BlockSpec((1, HEAD_DIM), lambda i: (0, 0))
    kv_in = lambda k: pl.BlockSpec((tr, KV_WIDTH), lambda i: (i, kvb + k))
    return pl.pallas_call(
        functools.partial(_prep_kernel, tr=tr, seq=seq),
        grid=(T // tr,),
        in_specs=[pl.BlockSpec((tr, B_WIDTH), row), kv_in(2), kv_in(3), kv_in(4), kv_in(5),
                  gain, gain, gain, tab, tab, tab],
        out_specs=[pl.BlockSpec((tr, B_WIDTH), row), pl.BlockSpec((tr, 2 * KV_WIDTH), row),
                   pl.BlockSpec((tr, 2 * KV_WIDTH), row), pl.BlockSpec((tr, KV_WIDTH), row),
                   pl.BlockSpec((tr, 2 * KV_WIDTH), row)],
        out_shape=[jax.ShapeDtypeStruct((T, B_WIDTH), BF16),
                   jax.ShapeDtypeStruct((T, 2 * KV_WIDTH), BF16),
                   jax.ShapeDtypeStruct((T, 2 * KV_WIDTH), BF16),
                   jax.ShapeDtypeStruct((T, KV_WIDTH), BF16),
                   jax.ShapeDtypeStruct((T, 2 * KV_WIDTH), BF16)],
        compiler_params=_params("parallel"),
        name="qk_prep",
    )(proj_b, proj_b, proj_b, proj_b, proj_b, q_norm, k_slc_norm, k_win_norm,
      rope_c, rope_su, rope_sd)


def _compress_kernel(zk_ref, zv_ref, pek_ref, pev_ref, w1k_ref, w2k_ref, w1v_ref, w2v_ref,
                     kg_ref, c_ref, su_ref, sd_ref, kc_ref, vc_ref, zr_ref):
    half = CMP_STRIDE * HEAD_DIM
    nb = zr_ref.shape[0]

    def comp(z_ref, pe_ref, w1_ref, w2_ref):
        for l in range(CMP_STRIDE):
            zr_ref[:, l * HEAD_DIM:(l + 1) * HEAD_DIM] = z_ref[pl.ds(l, nb, stride=CMP_STRIDE), :]
        z = zr_ref[...]
        top = jnp.dot((z + pe_ref[:, :half]).astype(BF16), w1_ref[:half, :], preferred_element_type=F32)
        bot = jnp.dot((z + pe_ref[:, half:]).astype(BF16), w1_ref[half:, :], preferred_element_type=F32)
        hid = top + pltpu.roll(bot, nb - 1, 0)
        return jnp.dot(jax.nn.gelu(hid).astype(BF16), w2_ref[...], preferred_element_type=F32)

    k = comp(zk_ref, pek_ref, w1k_ref, w2k_ref)
    kc_ref[...] = _rope(_rms(k, kg_ref[...]), c_ref[...], su_ref[...], sd_ref[...]).astype(BF16)
    vc_ref[...] = comp(zv_ref, pev_ref, w1v_ref, w2v_ref).astype(BF16)


def _compress(proj_b, pek, pev, w1k, w2k, w1v, w2v, k_cmp_norm, cmp_c, cmp_su, cmp_sd, *, batch, seq):
    G = N_KV_GROUPS
    nb = seq // CMP_STRIDE
    kc_blk = B_WIDTH // HEAD_DIM
    vc_blk = kc_blk + G
    ospec = pl.BlockSpec((None, None, nb, HEAD_DIM), lambda b, g: (b, g, 0, 0))
    full = lambda a: pl.BlockSpec(a.shape, lambda b, g: (0,) * a.ndim)
    consts = (pek, pev, w1k, w2k, w1v, w2v, k_cmp_norm, cmp_c, cmp_su, cmp_sd)
    return pl.pallas_call(
        _compress_kernel,
        grid=(batch, G),
        in_specs=[pl.BlockSpec((seq, HEAD_DIM), lambda b, g: (b, kc_blk + g)),
                  pl.BlockSpec((seq, HEAD_DIM), lambda b, g: (b, vc_blk + g))]
                 + [full(a) for a in consts],
        out_specs=[ospec, ospec],
        out_shape=[jax.ShapeDtypeStruct((batch, G, nb, HEAD_DIM), BF16)] * 2,
        scratch_shapes=[pltpu.VMEM((nb, CMP_STRIDE * HEAD_DIM), F32)],
        compiler_params=_params("parallel", "parallel"),
        name="compress",
    )(proj_b, proj_b, *consts)


def _cmp_attn_kernel(q_ref, kc_ref, vc_ref, kw_ref, vw_ref, gate_ref, ovt_ref, oc_ref, bias_ref,
                     qs_ref, *, tq):
    hpg = HEADS_PER_GROUP
    M = hpg * tq
    nb = kc_ref.shape[0]
    n_sel = ovt_ref.shape[0]
    t0 = pl.program_id(2) * tq
    for h in range(hpg):
        qs_ref[h * tq:(h + 1) * tq, :] = q_ref[:, h * HEAD_DIM:(h + 1) * HEAD_DIM]

    wlen = WINDOW + tq
    kstart = pl.multiple_of(jnp.maximum(t0 - WINDOW, 0), tq)
    sw = lax.dot_general(qs_ref[...], kw_ref[pl.ds(kstart, wlen), :], NT_DIMS,
                         preferred_element_type=F32)
    tw = t0 + lax.broadcasted_iota(jnp.int32, (1, tq, wlen), 1)
    kp = kstart + lax.broadcasted_iota(jnp.int32, (1, tq, wlen), 2)
    keep = (tw - kp).astype(jnp.uint32) < WINDOW
    sw = jnp.where(keep, sw.reshape(hpg, tq, wlen), NEG).reshape(M, wlen)
    pw = jnp.exp2(sw - jnp.max(sw, axis=-1, keepdims=True))
    aw = jnp.dot(pw.astype(BF16), vw_ref[pl.ds(kstart, wlen), :], preferred_element_type=F32)
    o_w = aw[:, :HEAD_DIM] * (1.0 / aw[:, HEAD_DIM:])

    s = lax.dot_general(qs_ref[...], kc_ref[...], NT_DIMS, preferred_element_type=F32)
    s = s.reshape(hpg, tq, nb)
    t = t0 + lax.broadcasted_iota(jnp.int32, (1, tq, nb), 1)
    n = lax.broadcasted_iota(jnp.int32, (1, tq, nb), 2)
    mask = n * CMP_STRIDE + (CMP_BLOCK - 1) <= t
    s = jnp.where(mask, s, NEG)
    e = jnp.exp2(s - jnp.max(s, axis=-1, keepdims=True))
    inv = 1.0 / jnp.sum(e, axis=-1, keepdims=True)
    p = jnp.where(mask, e * inv, 0.0)
    o = jnp.dot(p.reshape(hpg * tq, nb).astype(BF16), vc_ref[...], preferred_element_type=F32)
    sig = jax.nn.sigmoid(gate_ref[...])
    for h in range(hpg):
        rows = slice(h * tq, (h + 1) * tq)
        oc_ref[:, h * HEAD_DIM:(h + 1) * HEAD_DIM] = (
            o[rows] * sig[:, h:h + 1]
            + o_w[rows] * sig[:, 2 * hpg + h:2 * hpg + h + 1]).astype(oc_ref.dtype)

    psum = jnp.sum(p, axis=0)
    ovt = ovt_ref[...]
    hi = psum.astype(BF16)
    r1 = psum - hi.astype(F32)
    mid = r1.astype(BF16)
    lo = (r1 - mid.astype(F32)).astype(BF16)
    imp = (lax.dot_general(ovt, hi, NT_DIMS, preferred_element_type=F32)
           + lax.dot_general(ovt, mid, NT_DIMS, preferred_element_type=F32)
           + lax.dot_general(ovt, lo, NT_DIMS, preferred_element_type=F32))
    j = lax.broadcasted_iota(jnp.int32, (n_sel, tq), 0)
    tt = t0 + lax.broadcasted_iota(jnp.int32, (n_sel, tq), 1)
    cur = tt // SEL_BLOCK
    forced = jnp.where((j == 0) | (j == cur) | (j == cur - 1), FORCE_BONUS, 0.0)
    score = jnp.where(j * SEL_BLOCK <= tt, imp + forced, NEG)
    sub = 8
    groups = [score[k * sub:(k + 1) * sub, :] for k in range(n_sel // sub)]
    ranks = [jnp.zeros((sub, tq), F32) for _ in groups]
    jrow = lax.broadcasted_iota(jnp.int32, (sub, tq), 0)
    for i in range(n_sel):
        r = jnp.broadcast_to(score[i:i + 1, :], (sub, tq))
        for k, grp in enumerate(groups):
            if k * sub > i:
                beats = r >= grp
            elif (k + 1) * sub <= i:
                beats = r > grp
            else:
                beats = jnp.where(jrow + k * sub > i, jnp.where(r >= grp, 1.0, 0.0),
                                  jnp.where(r > grp, 1.0, 0.0)) > 0.5
            ranks[k] = ranks[k] + jnp.where(beats, 1.0, 0.0)
    rank = jnp.concatenate(ranks, axis=0)
    bias = jnp.where(rank < SEL_TOP, 0.0, NEG)
    bias = jnp.concatenate([bias, jnp.zeros((LANES - n_sel, tq), F32)], axis=0)
    bias_ref[...] = bias.T.astype(bias_ref.dtype)


def _cmp_attn(qn, kc, vc, kwn, vw, proj_b, ovt, *, batch, seq, tq=256):
    assert WINDOW % tq == 0
    T = qn.shape[0]
    G = N_KV_GROUPS
    nq = seq // tq
    gw = HEADS_PER_GROUP * HEAD_DIM
    nb = kc.shape[2]
    gate_blk = (B_WIDTH + 6 * KV_WIDTH) // LANES
    rowg = lambda b, g, i: (b * nq + i, g)
    seqg = lambda b, g, i: (b, g)
    return pl.pallas_call(
        functools.partial(_cmp_attn_kernel, tq=tq),
        grid=(batch, G, nq),
        in_specs=[
            pl.BlockSpec((tq, gw), rowg),
            pl.BlockSpec((None, None, nb, HEAD_DIM), lambda b, g, i: (b, g, 0, 0)),
            pl.BlockSpec((None, None, nb, HEAD_DIM), lambda b, g, i: (b, g, 0, 0)),
            pl.BlockSpec((seq, HEAD_DIM), seqg),
            pl.BlockSpec((seq, 2 * HEAD_DIM), seqg),
            pl.BlockSpec((tq, LANES), lambda b, g, i: (b * nq + i, gate_blk + g)),
            pl.BlockSpec(ovt.shape, lambda b, g, i: (0, 0)),
        ],
        out_specs=[pl.BlockSpec((tq, gw), rowg), pl.BlockSpec((tq, LANES), rowg)],
        out_shape=[jax.ShapeDtypeStruct((T, B_WIDTH), BF16),
                   jax.ShapeDtypeStruct((T, G * LANES), BF16)],
        scratch_shapes=[pltpu.VMEM((HEADS_PER_GROUP * tq, HEAD_DIM), BF16)],
        compiler_params=_params("parallel", "parallel", "parallel"),
        name="cmp_win_attn",
    )(qn, kc, vc, kwn, vw, proj_b, ovt)


def _sw_attn_kernel(q_ref, bias_ref, ksa_ref, vs_ref, oc_ref, gate_ref, o_ref,
                    qs_ref, m_ref, acc_ref, ss0_ref, ss1_ref, *, tq, tks):
    hpg = HEADS_PER_GROUP
    M = hpg * tq
    ss_ref = (ss0_ref, ss1_ref)
    t0 = pl.program_id(2) * tq
    bias = bias_ref[...]
    for h in range(hpg):
        qs_ref[h * tq:(h + 1) * tq, :HEAD_DIM] = q_ref[:, h * HEAD_DIM:(h + 1) * HEAD_DIM]
        qs_ref[h * tq:(h + 1) * tq, HEAD_DIM:] = bias

    def reset():
        m_ref[...] = jnp.full_like(m_ref, NEG)
        acc_ref[...] = jnp.zeros_like(acc_ref)

    def step(s, v):
        m_old = m_ref[...]
        m_new = jnp.maximum(m_old, jnp.max(s, axis=-1, keepdims=True))
        alpha = jnp.exp2(m_old - m_new)
        p = jnp.exp2(s - jnp.tile(m_new, (1, s.shape[1] // LANES)))
        acc_ref[...] = (jnp.tile(alpha, (1, 2)) * acc_ref[...]
                        + jnp.dot(p.astype(BF16), v, preferred_element_type=F32))
        m_ref[...] = m_new

    def result():
        return acc_ref[:, :HEAD_DIM] * (1.0 / acc_ref[:, HEAD_DIM:])

    def masked(s, keep):
        tk = s.shape[1]
        return jnp.where(keep, s.reshape(hpg, tq, tk), NEG).reshape(M, tk)

    def positions(tk):
        t = t0 + lax.broadcasted_iota(jnp.int32, (1, tq, tk), 1)
        kpos = lax.broadcasted_iota(jnp.int32, (1, tq, tk), 2)
        return t, kpos

    def sel_base(j):
        return pl.multiple_of(j * tks, tks)

    def sel_scores(j):
        base = sel_base(j)
        k = ksa_ref[pl.ds(base, tks), :]
        s = lax.dot_general(qs_ref[...], k, NT_DIMS, preferred_element_type=F32)
        t, kpos = positions(tks)
        return masked(s, kpos + base <= t)

    sel_last = (t0 + tq - 1) // tks

    def by_parity(n, fn):
        @pl.when(n % 2 == 0)
        def _():
            fn(0, 1)

        @pl.when(n % 2 == 1)
        def _():
            fn(1, 0)

    reset()
    ss_ref[0][...] = sel_scores(0)

    def sel_body(j, carry):
        def run(cur, nxt):
            ss_ref[nxt][...] = sel_scores(j + 1)
            step(ss_ref[cur][...], vs_ref[pl.ds(sel_base(j), tks), :])

        by_parity(j, run)
        return carry

    lax.fori_loop(0, sel_last, sel_body, 0)
    by_parity(sel_last,
              lambda cur, nxt: step(ss_ref[cur][...], vs_ref[pl.ds(sel_base(sel_last), tks), :]))
    o_s = result()

    sig = jax.nn.sigmoid(gate_ref[...])
    for h in range(hpg):
        rows = slice(h * tq, (h + 1) * tq)
        cols = slice(h * HEAD_DIM, (h + 1) * HEAD_DIM)
        o = oc_ref[:, cols].astype(F32) + sig[:, hpg + h:hpg + h + 1] * o_s[rows]
        o_ref[:, cols] = o.astype(o_ref.dtype)


def _sw_attn(qn, bias, ksa, vs, o_c, proj_b, *, batch, seq, tq=256, tks=512):
    T = qn.shape[0]
    G = N_KV_GROUPS
    nq = seq // tq
    gw = HEADS_PER_GROUP * HEAD_DIM
    M = HEADS_PER_GROUP * tq
    gate_blk = (B_WIDTH + 6 * KV_WIDTH) // LANES
    rowg = lambda b, g, i: (b * nq + i, g)
    seqg = lambda b, g, i: (b, g)
    return pl.pallas_call(
        functools.partial(_sw_attn_kernel, tq=tq, tks=tks),
        grid=(batch, G, nq),
        in_specs=[
            pl.BlockSpec((tq, gw), rowg),
            pl.BlockSpec((tq, LANES), rowg),
            pl.BlockSpec((seq, 2 * HEAD_DIM), seqg),
            pl.BlockSpec((seq, 2 * HEAD_DIM), seqg),
            pl.BlockSpec((tq, gw), rowg),
            pl.BlockSpec((tq, LANES), lambda b, g, i: (b * nq + i, gate_blk + g)),
        ],
        out_specs=pl.BlockSpec((tq, gw), rowg),
        out_shape=jax.ShapeDtypeStruct((T, B_WIDTH), BF16),
        scratch_shapes=[pltpu.VMEM((M, 2 * HEAD_DIM), BF16), pltpu.VMEM((M, LANES), F32),
                        pltpu.VMEM((M, 2 * HEAD_DIM), F32),
                        pltpu.VMEM((M, tks), F32), pltpu.VMEM((M, tks), F32)],
        compiler_params=_params("parallel", "parallel", "parallel"),
        name="sel_attn",
    )(qn, bias, ksa, vs, o_c, proj_b)


def _merge_kernel(oa_ref, ob_ref, ga_ref, gb_ref, wa_ref, wb_ref, o_ref):
    a = jnp.dot(oa_ref[...], wa_ref[...].astype(BF16), preferred_element_type=F32)
    b = jnp.dot(ob_ref[...], wb_ref[...].astype(BF16), preferred_element_type=F32)
    ga = jax.nn.sigmoid(ga_ref[...].astype(F32))
    gb = jax.nn.sigmoid(gb_ref[...].astype(F32))
    o_ref[...] = (ga * a + gb * b).astype(o_ref.dtype)


def _merge(o_a, o_b, proj_a, wa, wb, *, tm=1024, tn=512):
    T, D = o_a.shape
    N = wa.shape[1]
    nn = N // tn
    ga_blk = 2 * A_WIDTH // tn
    return pl.pallas_call(
        _merge_kernel,
        grid=(T // tm, nn),
        in_specs=[
            pl.BlockSpec((tm, D), lambda i, j: (i, 0)),
            pl.BlockSpec((tm, D), lambda i, j: (i, 0)),
            pl.BlockSpec((tm, tn), lambda i, j: (i, ga_blk + j)),
            pl.BlockSpec((tm, tn), lambda i, j: (i, ga_blk + nn + j)),
            pl.BlockSpec((D, tn), lambda i, j: (0, j)),
            pl.BlockSpec((D, tn), lambda i, j: (0, j)),
        ],
        out_specs=pl.BlockSpec((tm, tn), lambda i, j: (i, j)),
        out_shape=jax.ShapeDtypeStruct((T, N), BF16),
        compiler_params=_params("parallel", "arbitrary"),
        name="merge",
    )(o_a, o_b, proj_a, proj_a, wa, wb)


def _out_kernel(x_ref, m_ref, w_ref, o_ref):
    o_ref[...] = x_ref[...] + jnp.dot(m_ref[...], w_ref[...].astype(BF16),
                                      preferred_element_type=F32)


def _out_proj(x, merged, w, *, tm=1024, tn=512):
    T, D = merged.shape
    N = w.shape[1]
    return pl.pallas_call(
        _out_kernel,
        grid=(T // tm, N // tn),
        in_specs=[
            pl.BlockSpec((tm, tn), lambda i, j: (i, j)),
            pl.BlockSpec((tm, D), lambda i, j: (i, 0)),
            pl.BlockSpec((D, tn), lambda i, j: (0, j)),
        ],
        out_specs=pl.BlockSpec((tm, tn), lambda i, j: (i, j)),
        out_shape=jax.ShapeDtypeStruct((T, N), F32),
        compiler_params=_params("parallel", "arbitrary"),
        name="out_proj",
    )(x, merged, w)


def _rope_tables(pos):
    half = ROPE_DIM // 2
    inv = ROPE_THETA ** (-2.0 * jnp.arange(half, dtype=F32) / ROPE_DIM)
    ang = pos.astype(F32)[:, None] * inv
    cos, sin = jnp.cos(ang), jnp.sin(ang)
    n = pos.shape[0]
    z16 = jnp.zeros((n, half), F32)
    rest = HEAD_DIM - ROPE_DIM
    c = jnp.concatenate([cos, cos, jnp.ones((n, rest), F32)], axis=1)
    su = jnp.concatenate([z16, sin, jnp.zeros((n, rest), F32)], axis=1)
    sd = jnp.concatenate([-sin, z16, jnp.zeros((n, rest), F32)], axis=1)
    return c, su, sd


def _split_w_in(w_in):
    sizes = [A_WIDTH, A_WIDTH, B_WIDTH] + [KV_WIDTH] * 6 + [3 * N_HEADS, D_MODEL, D_MODEL]
    offs = np.concatenate([[0], np.cumsum(sizes)])
    w = w_in.astype(BF16)
    w_a = jnp.concatenate([w[:, :offs[2]], w[:, offs[10]:]], axis=1)
    gates = w[:, offs[9]:offs[10]]
    gate_blocks = []
    for g in range(N_KV_GROUPS):
        cols = [gates[:, br * N_HEADS + g * HEADS_PER_GROUP:br * N_HEADS + (g + 1) * HEADS_PER_GROUP]
                for br in range(3)]
        pad = jnp.zeros((w.shape[0], LANES - 3 * HEADS_PER_GROUP), BF16)
        gate_blocks.extend(cols + [pad])
    w_b = jnp.concatenate([w[:, offs[2]:offs[9]]] + gate_blocks, axis=1)
    return w_a, w_b


def _overlap_t(nb_pad, n_sel):
    cmp_start = np.arange(nb_pad) * CMP_STRIDE
    sel_start = np.arange(n_sel) * SEL_BLOCK
    ov = ((cmp_start[None, :] < sel_start[:, None] + SEL_BLOCK)
          & (cmp_start[None, :] + CMP_BLOCK > sel_start[:, None]))
    return jnp.asarray(ov, dtype=BF16)


def _layer(x, ffn1_norm, ffn1_w_gate, ffn1_w_up, ffn1_w_down, mix_norm, w_in,
           a_v_norm, a_w_s, a_b_s, q_norm, k_cmp_norm, k_slc_norm, k_win_norm,
           cmp_k_pe, cmp_k_w1, cmp_k_w2, cmp_v_pe, cmp_v_w1, cmp_v_w2,
           w_branch_a, w_branch_b, w_out, ffn2_norm, ffn2_w_gate, ffn2_w_up, ffn2_w_down):
    B, S, D = x.shape
    T = B * S
    G = N_KV_GROUPS
    row = lambda v: v.reshape(1, -1)
    x0 = x.reshape(T, D)

    x1 = _ffn(x0, row(ffn1_norm), ffn1_w_gate, ffn1_w_up, ffn1_w_down)

    w_a, w_b = _split_w_in(w_in)
    proj_a = _proj(x1, row(mix_norm), w_a, BF16, tm=1024, tn=1024)
    proj_b = _proj(x1, row(mix_norm), w_b, F32, tm=1024, tn=768)

    o_a = _gmlp(proj_a, row(a_v_norm), a_w_s, a_b_s.T)

    pos = jnp.arange(S)
    qn, ksa, vs, kwn, vw = _prep(proj_b, row(q_norm), row(k_slc_norm), row(k_win_norm),
                                 *_rope_tables(pos), seq=S)

    nb_pad = S // CMP_STRIDE
    cmp_end = jnp.arange(nb_pad) * CMP_STRIDE + (CMP_BLOCK - 1)
    kc, vc = _compress(proj_b, cmp_k_pe.reshape(1, -1), cmp_v_pe.reshape(1, -1),
                       cmp_k_w1.astype(BF16), cmp_k_w2.astype(BF16),
                       cmp_v_w1.astype(BF16), cmp_v_w2.astype(BF16),
                       row(k_cmp_norm), *_rope_tables(cmp_end), batch=B, seq=S)

    o_cw, bias = _cmp_attn(qn, kc, vc, kwn, vw, proj_b, _overlap_t(nb_pad, S // SEL_BLOCK),
                           batch=B, seq=S)
    o_b = _sw_attn(qn, bias, ksa, vs, o_cw, proj_b, batch=B, seq=S)

    merged = _merge(o_a, o_b, proj_a, w_branch_a, w_branch_b)
    x2 = _out_proj(x1, merged, w_out)

    x3 = _ffn(x2, row(ffn2_norm), ffn2_w_gate, ffn2_w_up, ffn2_w_down)
    return x3.reshape(B, S, D)


def kernel(x, ffn1_norm, ffn1_w_gate, ffn1_w_up, ffn1_w_down, mix_norm, w_in, a_v_norm, a_w_s, a_b_s, q_norm, k_cmp_norm, k_slc_norm, k_win_norm, cmp_k_pe, cmp_k_w1, cmp_k_w2, cmp_v_pe, cmp_v_w1, cmp_v_w2, w_branch_a, w_branch_b, w_out, ffn2_norm, ffn2_w_gate, ffn2_w_up, ffn2_w_down):
    params = (ffn1_norm, ffn1_w_gate, ffn1_w_up, ffn1_w_down, mix_norm, w_in,
              a_v_norm, a_w_s, a_b_s, q_norm, k_cmp_norm, k_slc_norm, k_win_norm,
              cmp_k_pe, cmp_k_w1, cmp_k_w2, cmp_v_pe, cmp_v_w1, cmp_v_w2,
              w_branch_a, w_branch_b, w_out, ffn2_norm, ffn2_w_gate, ffn2_w_up, ffn2_w_down)
    for l in range(params[0].shape[0]):
        x = _layer(x, *[p[l] for p in params])
    return x
```

```python
import functools

import numpy as np
import jax
import jax.numpy as jnp
from jax import lax
from jax.experimental import pallas as pl
from jax.experimental.pallas import tpu as pltpu

D_MODEL = 2048
D_FF = 5504
A_WIDTH = 2048
A_GROUPS = 8
A_CHUNK = 128
N_HEADS = 16
HEAD_DIM = 128
N_KV_GROUPS = 2
HEADS_PER_GROUP = N_HEADS // N_KV_GROUPS
CMP_BLOCK = 32
CMP_STRIDE = 16
CMP_HIDDEN = 256
SEL_BLOCK = 64
SEL_TOP = 16
WINDOW = 512
ROPE_THETA = 500000.0
ROPE_DIM = HEAD_DIM // 4
EPS = 1e-6
NEG = -1e30
FORCE_BONUS = 1e4
LOG2_E = 1.4426950408889634
B_WIDTH = N_HEADS * HEAD_DIM
KV_WIDTH = N_KV_GROUPS * HEAD_DIM

LANES = 128
VMEM_LIMIT = 56 * 1024 * 1024

F32 = jnp.float32
BF16 = jnp.bfloat16
NT_DIMS = (((1,), (1,)), ((), ()))


def _params(*sem):
    return pltpu.CompilerParams(dimension_semantics=sem, vmem_limit_bytes=VMEM_LIMIT)


def _rms(x, g):
    ms = jnp.mean(x * x, axis=-1, keepdims=True)
    return x * lax.rsqrt(ms + EPS) * g


def _ffn_kernel(x_ref, g_ref, wg_ref, wu_ref, wd_ref, o_ref, h_ref, *, nf, tf, f_last):
    j = pl.program_id(1)

    def partial(width):
        h = h_ref[...]
        a = jnp.dot(h, wg_ref[:, :width].astype(BF16), preferred_element_type=F32)
        b = jnp.dot(h, wu_ref[:, :width].astype(BF16), preferred_element_type=F32)
        act = (a * jax.nn.sigmoid(a) * b).astype(BF16)
        return jnp.dot(act, wd_ref[:width, :].astype(BF16), preferred_element_type=F32)

    @pl.when(j == 0)
    def _():
        h_ref[...] = _rms(x_ref[...], g_ref[...]).astype(BF16)
        o_ref[...] = partial(f_last)

    @pl.when((j > 0) & (j < nf - 1))
    def _():
        o_ref[...] += partial(tf)

    @pl.when(j == nf - 1)
    def _():
        o_ref[...] = x_ref[...] + 0.5 * (o_ref[...] + partial(tf))


def _ffn(x, norm, wg, wu, wd, *, tm=1024, tf=256):
    T, D = x.shape
    F = wg.shape[1]
    nf = pl.cdiv(F, tf)
    f_last = F - (nf - 1) * tf
    hid = lambda j: (j + nf - 1) % nf
    return pl.pallas_call(
        functools.partial(_ffn_kernel, nf=nf, tf=tf, f_last=f_last),
        grid=(T // tm, nf),
        in_specs=[
            pl.BlockSpec((tm, D), lambda i, j: (i, 0)),
            pl.BlockSpec((1, D), lambda i, j: (0, 0)),
            pl.BlockSpec((D, tf), lambda i, j: (0, hid(j))),
            pl.BlockSpec((D, tf), lambda i, j: (0, hid(j))),
            pl.BlockSpec((tf, D), lambda i, j: (hid(j), 0)),
        ],
        out_specs=pl.BlockSpec((tm, D), lambda i, j: (i, 0)),
        out_shape=jax.ShapeDtypeStruct((T, D), F32),
        scratch_shapes=[pltpu.VMEM((tm, D), BF16)],
        compiler_params=_params("parallel", "arbitrary"),
        name="ffn",
    )(x, norm, wg, wu, wd)


def _proj_kernel(x_ref, g_ref, win_ref, wgate_ref, whead_ref, uv_ref, gab_ref, qkv_ref, hg_ref,
                 h_ref, *, n_uv, n_gab, n_qkv):
    j = pl.program_id(1)

    @pl.when(j == 0)
    def _():
        h_ref[...] = _rms(x_ref[...], g_ref[...]).astype(BF16)

    def project(w_ref, o_ref):
        o_ref[...] = jnp.dot(h_ref[...], w_ref[...].astype(BF16),
                             preferred_element_type=F32).astype(o_ref.dtype)

    @pl.when(j < n_uv)
    def _():
        project(win_ref, uv_ref)

    @pl.when((j >= n_uv) & (j < n_uv + n_gab))
    def _():
        project(wgate_ref, gab_ref)

    @pl.when((j >= n_uv + n_gab) & (j < n_uv + n_gab + n_qkv))
    def _():
        project(win_ref, qkv_ref)

    @pl.when(j == n_uv + n_gab + n_qkv)
    def _():
        project(whead_ref, hg_ref)


def _proj(x, norm, w_in, w_gate, w_head, *, tm=1024, tn=512):
    T, D = x.shape
    n_uv = 2 * A_WIDTH // tn
    n_gab = w_gate.shape[1] // tn
    qkv_w = B_WIDTH + 6 * KV_WIDTH
    n_qkv = qkv_w // tn
    steps = n_uv + n_gab + n_qkv + 1
    clip = lambda v, lo, hi: jnp.minimum(jnp.maximum(v, lo), hi)
    win_blk = lambda j: jnp.where(j < n_uv + n_gab, jnp.minimum(j, n_uv - 1),
                                  jnp.minimum(j - n_gab, n_uv + n_qkv - 1))
    return pl.pallas_call(
        functools.partial(_proj_kernel, n_uv=n_uv, n_gab=n_gab, n_qkv=n_qkv),
        grid=(T // tm, steps),
        in_specs=[
            pl.BlockSpec((tm, D), lambda i, j: (i, 0)),
            pl.BlockSpec((1, D), lambda i, j: (0, 0)),
            pl.BlockSpec((D, tn), lambda i, j: (0, win_blk(j))),
            pl.BlockSpec((D, tn), lambda i, j: (0, clip(j - n_uv, 0, n_gab - 1))),
            pl.BlockSpec(w_head.shape, lambda i, j: (0, 0)),
        ],
        out_specs=[
            pl.BlockSpec((tm, tn), lambda i, j: (i, jnp.minimum(j, n_uv - 1))),
            pl.BlockSpec((tm, tn), lambda i, j: (i, clip(j - n_uv, 0, n_gab - 1))),
            pl.BlockSpec((tm, tn), lambda i, j: (i, clip(j - n_uv - n_gab, 0, n_qkv - 1))),
            pl.BlockSpec((tm, w_head.shape[1]), lambda i, j: (i, 0)),
        ],
        out_shape=[jax.ShapeDtypeStruct((T, 2 * A_WIDTH), BF16),
                   jax.ShapeDtypeStruct((T, w_gate.shape[1]), BF16),
                   jax.ShapeDtypeStruct((T, qkv_w), F32),
                   jax.ShapeDtypeStruct((T, w_head.shape[1]), F32)],
        scratch_shapes=[pltpu.VMEM((tm, D), BF16)],
        compiler_params=_params("parallel", "arbitrary"),
        name="proj",
    )(x, norm, w_in, w_gate, w_head)


def _gmlp_kernel(u_ref, v_ref, vn_ref, ws_ref, bs_ref, o_ref, *, tr):
    gw = A_WIDTH // A_GROUPS
    v = jax.nn.gelu(v_ref[...].astype(F32))
    vn = _rms(v, vn_ref[...]).astype(BF16)
    row = lax.broadcasted_iota(jnp.int32, (A_CHUNK, A_CHUNK), 0)
    col = lax.broadcasted_iota(jnp.int32, (A_CHUNK, A_CHUNK), 1)
    for g in range(A_GROUPS):
        w = jnp.where(col <= row, ws_ref[g], 0.0).astype(BF16)
        bias = bs_ref[:, g:g + 1]
        for c in range(tr // A_CHUNK):
            rows = slice(c * A_CHUNK, (c + 1) * A_CHUNK)
            cols = slice(g * gw, (g + 1) * gw)
            mixed = jnp.dot(w, vn[rows, cols], preferred_element_type=F32) + bias
            u = jax.nn.gelu(u_ref[rows, cols].astype(F32))
            o_ref[rows, cols] = (u * mixed).astype(o_ref.dtype)


def _gmlp(proj_a, a_v_norm, w_s, b_s_t, *, tr=512):
    T = proj_a.shape[0]
    return pl.pallas_call(
        functools.partial(_gmlp_kernel, tr=tr),
        grid=(T // tr,),
        in_specs=[
            pl.BlockSpec((tr, A_WIDTH), lambda i: (i, 0)),
            pl.BlockSpec((tr, A_WIDTH), lambda i: (i, 1)),
            pl.BlockSpec((1, A_WIDTH), lambda i: (0, 0)),
            pl.BlockSpec((A_GROUPS, A_CHUNK, A_CHUNK), lambda i: (0, 0, 0)),
            pl.BlockSpec((A_CHUNK, A_GROUPS), lambda i: (0, 0)),
        ],
        out_specs=pl.BlockSpec((tr, A_WIDTH), lambda i: (i, 0)),
        out_shape=jax.ShapeDtypeStruct((T, A_WIDTH), BF16),
        compiler_params=_params("parallel"),
        name="gmlp",
    )(proj_a, proj_a, a_v_norm, w_s, b_s_t)


def _rope(y, c, su, sd):
    return y * c + pltpu.roll(y, 16, 1) * su + pltpu.roll(y, HEAD_DIM - 16, 1) * sd


def _norm_rope_mxu(x, g, c, s, w):
    xg = x * g
    lhs = jnp.concatenate([(x * x).astype(BF16), xg.astype(BF16)], axis=1)
    res = jnp.dot(lhs, w, preferred_element_type=F32)
    r = lax.rsqrt(res[:, :HEAD_DIM] * (1.0 / HEAD_DIM) + EPS)
    return (xg * c + res[:, HEAD_DIM:] * s) * r


def _prep_kernel(q_ref, ks_ref, vs_ref, kw_ref, vw_ref, qg_ref, ksg_ref, kwg_ref,
                 cq_ref, sq_ref, ck_ref, sk_ref, w_ref,
                 qn_ref, ksa_ref, vso_ref, kwn_ref, vwo_ref, *, tr, seq):
    cq, sq, ck, sk, w = cq_ref[...], sq_ref[...], ck_ref[...], sk_ref[...], w_ref[...]
    for h in range(N_HEADS):
        cols = slice(h * HEAD_DIM, (h + 1) * HEAD_DIM)
        qn_ref[:, cols] = _norm_rope_mxu(q_ref[:, cols], qg_ref[...], cq, sq, w).astype(BF16)
    t0 = (pl.program_id(0) * tr) % seq
    t = t0 + lax.broadcasted_iota(jnp.int32, (tr, LANES), 0)
    lane = lax.broadcasted_iota(jnp.int32, (tr, LANES), 1)
    onehot = jnp.where((t // SEL_BLOCK) == lane, 1.0, 0.0).astype(BF16)
    ones = jnp.ones((tr, HEAD_DIM), BF16)
    for g in range(N_KV_GROUPS):
        cols = slice(g * HEAD_DIM, (g + 1) * HEAD_DIM)
        lo = slice(2 * g * HEAD_DIM, (2 * g + 1) * HEAD_DIM)
        hi = slice((2 * g + 1) * HEAD_DIM, (2 * g + 2) * HEAD_DIM)
        ksa_ref[:, lo] = _norm_rope_mxu(ks_ref[:, cols], ksg_ref[...], ck, sk, w).astype(BF16)
        ksa_ref[:, hi] = onehot
        kwn_ref[:, cols] = _norm_rope_mxu(kw_ref[:, cols], kwg_ref[...], ck, sk, w).astype(BF16)
        vso_ref[:, lo] = vs_ref[:, cols].astype(BF16)
        vso_ref[:, hi] = ones
        vwo_ref[:, lo] = vw_ref[:, cols].astype(BF16)
        vwo_ref[:, hi] = ones


def _rope_matrix():
    half = ROPE_DIM // 2
    w = np.zeros((2 * HEAD_DIM, 2 * HEAD_DIM), np.float32)
    w[:HEAD_DIM, :HEAD_DIM] = 1.0
    for j in range(half):
        w[HEAD_DIM + j + half, HEAD_DIM + j] = 1.0
        w[HEAD_DIM + j, HEAD_DIM + j + half] = 1.0
    return jnp.asarray(w, dtype=BF16)


def _prep(proj_b, q_norm, k_slc_norm, k_win_norm, rope_c, rope_su, rope_sd, *, seq, tr=512):
    T = proj_b.shape[0]
    kvb = B_WIDTH // KV_WIDTH
    nrt = seq // tr
    q_scale = HEAD_DIM ** -0.5 * LOG2_E
    rope_s = rope_su + rope_sd
    w = _rope_matrix()
    row = lambda i: (i, 0)
    tab = pl.BlockSpec((tr, HEAD_DIM), lambda i: (i % nrt, 0))
    gain = pl.BlockSpec((1, HEAD_DIM), lambda i: (0, 0))
    kv_in = lambda k: pl.BlockSpec((tr, KV_WIDTH), lambda i: (i, kvb + k))
    return pl.pallas_call(
        functools.partial(_prep_kernel, tr=tr, seq=seq),
        grid=(T // tr,),
        in_specs=[pl.BlockSpec((tr, B_WIDTH), row), kv_in(2), kv_in(3), kv_in(4), kv_in(5),
                  gain, gain, gain, tab, tab, tab, tab, pl.BlockSpec(w.shape, lambda i: (0, 0))],
        out_specs=[pl.BlockSpec((tr, B_WIDTH), row), pl.BlockSpec((tr, 2 * KV_WIDTH), row),
                   pl.BlockSpec((tr, 2 * KV_WIDTH), row), pl.BlockSpec((tr, KV_WIDTH), row),
                   pl.BlockSpec((tr, 2 * KV_WIDTH), row)],
        out_shape=[jax.ShapeDtypeStruct((T, B_WIDTH), BF16),
                   jax.ShapeDtypeStruct((T, 2 * KV_WIDTH), BF16),
                   jax.ShapeDtypeStruct((T, 2 * KV_WIDTH), BF16),
                   jax.ShapeDtypeStruct((T, KV_WIDTH), BF16),
                   jax.ShapeDtypeStruct((T, 2 * KV_WIDTH), BF16)],
        compiler_params=_params("parallel"),
        name="qk_prep",
    )(proj_b, proj_b, proj_b, proj_b, proj_b, q_norm, k_slc_norm, k_win_norm,
      rope_c * q_scale, rope_s * q_scale, rope_c, rope_s, w)


def _compress_kernel(zk_ref, zv_ref, pek_ref, pev_ref, w1k_ref, w2k_ref, w1v_ref, w2v_ref,
                     kg_ref, c_ref, su_ref, sd_ref, kc_ref, vc_ref, zr_ref):
    half = CMP_STRIDE * HEAD_DIM
    nb = zr_ref.shape[0]

    def comp(z_ref, pe_ref, w1_ref, w2_ref):
        for l in range(CMP_STRIDE):
            zr_ref[:, l * HEAD_DIM:(l + 1) * HEAD_DIM] = z_ref[pl.ds(l, nb, stride=CMP_STRIDE), :]
        z = zr_ref[...]
        top = jnp.dot((z + pe_ref[:, :half]).astype(BF16), w1_ref[:half, :], preferred_element_type=F32)
        bot = jnp.dot((z + pe_ref[:, half:]).astype(BF16), w1_ref[half:, :], preferred_element_type=F32)
        hid = top + pltpu.roll(bot, nb - 1, 0)
        return jnp.dot(jax.nn.gelu(hid).astype(BF16), w2_ref[...], preferred_element_type=F32)

    k = comp(zk_ref, pek_ref, w1k_ref, w2k_ref)
    kc_ref[...] = _rope(_rms(k, kg_ref[...]), c_ref[...], su_ref[...], sd_ref[...]).astype(BF16)
    vc_ref[...] = comp(zv_ref, pev_ref, w1v_ref, w2v_ref).astype(BF16)


def _compress(proj_b, pek, pev, w1k, w2k, w1v, w2v, k_cmp_norm, cmp_c, cmp_su, cmp_sd, *, batch, seq):
    G = N_KV_GROUPS
    nb = seq // CMP_STRIDE
    kc_blk = B_WIDTH // HEAD_DIM
    vc_blk = kc_blk + G
    ospec = pl.BlockSpec((None, None, nb, HEAD_DIM), lambda b, g: (b, g, 0, 0))
    full = lambda a: pl.BlockSpec(a.shape, lambda b, g: (0,) * a.ndim)
    consts = (pek, pev, w1k, w2k, w1v, w2v, k_cmp_norm, cmp_c, cmp_su, cmp_sd)
    return pl.pallas_call(
        _compress_kernel,
        grid=(batch, G),
        in_specs=[pl.BlockSpec((seq, HEAD_DIM), lambda b, g: (b, kc_blk + g)),
                  pl.BlockSpec((seq, HEAD_DIM), lambda b, g: (b, vc_blk + g))]
                 + [full(a) for a in consts],
        out_specs=[ospec, ospec],
        out_shape=[jax.ShapeDtypeStruct((batch, G, nb, HEAD_DIM), BF16)] * 2,
        scratch_shapes=[pltpu.VMEM((nb, CMP_STRIDE * HEAD_DIM), F32)],
        compiler_params=_params("parallel", "parallel"),
        name="compress",
    )(proj_b, proj_b, *consts)


def _cmp_attn_kernel(q_ref, kc_ref, vc_ref, kw_ref, vw_ref, gate_ref, ovt_ref, oc_ref, bias_ref,
                     qs_ref, *, tq):
    hpg = HEADS_PER_GROUP
    M = hpg * tq
    nb = kc_ref.shape[0]
    n_sel = ovt_ref.shape[0]
    t0 = pl.program_id(2) * tq
    for h in range(hpg):
        qs_ref[h * tq:(h + 1) * tq, :] = q_ref[:, h * HEAD_DIM:(h + 1) * HEAD_DIM]

    wlen = WINDOW + tq
    kstart = pl.multiple_of(jnp.maximum(t0 - WINDOW, 0), tq)
    sw = lax.dot_general(qs_ref[...], kw_ref[pl.ds(kstart, wlen), :], NT_DIMS,
                         preferred_element_type=F32)
    tw = t0 + lax.broadcasted_iota(jnp.int32, (1, tq, wlen), 1)
    kp = kstart + lax.broadcasted_iota(jnp.int32, (1, tq, wlen), 2)
    keep = (tw - kp).astype(jnp.uint32) < WINDOW
    sw = jnp.where(keep, sw.reshape(hpg, tq, wlen), NEG).reshape(M, wlen)
    pw = jnp.exp2(sw - jnp.max(sw, axis=-1, keepdims=True))
    aw = jnp.dot(pw.astype(BF16), vw_ref[pl.ds(kstart, wlen), :], preferred_element_type=F32)
    o_w = aw[:, :HEAD_DIM] * (1.0 / aw[:, HEAD_DIM:])

    s = lax.dot_general(qs_ref[...], kc_ref[...], NT_DIMS, preferred_element_type=F32)
    s = s.reshape(hpg, tq, nb)
    t = t0 + lax.broadcasted_iota(jnp.int32, (1, tq, nb), 1)
    n = lax.broadcasted_iota(jnp.int32, (1, tq, nb), 2)
    mask = n * CMP_STRIDE + (CMP_BLOCK - 1) <= t
    s = jnp.where(mask, s, NEG)
    e = jnp.exp2(s - jnp.max(s, axis=-1, keepdims=True))
    inv = 1.0 / jnp.sum(e, axis=-1, keepdims=True)
    p = jnp.where(mask, e * inv, 0.0)
    o = jnp.dot(p.reshape(hpg * tq, nb).astype(BF16), vc_ref[...], preferred_element_type=F32)
    sig = jax.nn.sigmoid(gate_ref[...])
    for h in range(hpg):
        rows = slice(h * tq, (h + 1) * tq)
        oc_ref[:, h * HEAD_DIM:(h + 1) * HEAD_DIM] = (
            o[rows] * sig[:, h:h + 1]
            + o_w[rows] * sig[:, 2 * hpg + h:2 * hpg + h + 1]).astype(oc_ref.dtype)

    psum = jnp.sum(p, axis=0)
    ovt = ovt_ref[...]
    hi = psum.astype(BF16)
    r1 = psum - hi.astype(F32)
    mid = r1.astype(BF16)
    lo = (r1 - mid.astype(F32)).astype(BF16)
    imp = (lax.dot_general(ovt, hi, NT_DIMS, preferred_element_type=F32)
           + lax.dot_general(ovt, mid, NT_DIMS, preferred_element_type=F32)
           + lax.dot_general(ovt, lo, NT_DIMS, preferred_element_type=F32))
    j = lax.broadcasted_iota(jnp.int32, (n_sel, tq), 0)
    tt = t0 + lax.broadcasted_iota(jnp.int32, (n_sel, tq), 1)
    cur = tt // SEL_BLOCK
    forced = jnp.where((j == 0) | (j == cur) | (j == cur - 1), FORCE_BONUS, 0.0)
    score = jnp.where(j * SEL_BLOCK <= tt, imp + forced, NEG)
    sub = 8
    groups = [score[k * sub:(k + 1) * sub, :] for k in range(n_sel // sub)]
    ranks = [jnp.zeros((sub, tq), F32) for _ in groups]
    jrow = lax.broadcasted_iota(jnp.int32, (sub, tq), 0)
    for i in range(n_sel):
        r = jnp.broadcast_to(score[i:i + 1, :], (sub, tq))
        for k, grp in enumerate(groups):
            if k * sub > i:
                beats = r >= grp
            elif (k + 1) * sub <= i:
                beats = r > grp
            else:
                beats = jnp.where(jrow + k * sub > i, jnp.where(r >= grp, 1.0, 0.0),
                                  jnp.where(r > grp, 1.0, 0.0)) > 0.5
            ranks[k] = ranks[k] + jnp.where(beats, 1.0, 0.0)
    rank = jnp.concatenate(ranks, axis=0)
    bias = jnp.where(rank < SEL_TOP, 0.0, NEG)
    bias = jnp.concatenate([bias, jnp.zeros((LANES - n_sel, tq), F32)], axis=0)
    bias_ref[...] = bias.T.astype(bias_ref.dtype)


def _cmp_attn(qn, kc, vc, kwn, vw, head_gates, ovt, *, batch, seq, tq=256):
    assert WINDOW % tq == 0
    T = qn.shape[0]
    G = N_KV_GROUPS
    nq = seq // tq
    gw = HEADS_PER_GROUP * HEAD_DIM
    nb = kc.shape[2]
    rowg = lambda b, g, i: (b * nq + i, g)
    seqg = lambda b, g, i: (b, g)
    return pl.pallas_call(
        functools.partial(_cmp_attn_kernel, tq=tq),
        grid=(batch, G, nq),
        in_specs=[
            pl.BlockSpec((tq, gw), rowg),
            pl.BlockSpec((None, None, nb, HEAD_DIM), lambda b, g, i: (b, g, 0, 0)),
            pl.BlockSpec((None, None, nb, HEAD_DIM), lambda b, g, i: (b, g, 0, 0)),
            pl.BlockSpec((seq, HEAD_DIM), seqg),
            pl.BlockSpec((seq, 2 * HEAD_DIM), seqg),
            pl.BlockSpec((tq, LANES), rowg),
            pl.BlockSpec(ovt.shape, lambda b, g, i: (0, 0)),
        ],
        out_specs=[pl.BlockSpec((tq, gw), rowg), pl.BlockSpec((tq, LANES), rowg)],
        out_shape=[jax.ShapeDtypeStruct((T, B_WIDTH), BF16),
                   jax.ShapeDtypeStruct((T, G * LANES), BF16)],
        scratch_shapes=[pltpu.VMEM((HEADS_PER_GROUP * tq, HEAD_DIM), BF16)],
        compiler_params=_params("parallel", "parallel", "parallel"),
        name="cmp_win_attn",
    )(qn, kc, vc, kwn, vw, head_gates, ovt)


def _sw_attn_kernel(q_ref, bias_ref, ksa_ref, vs_ref, oc_ref, gate_ref, o_ref,
                    qs_ref, m_ref, acc_ref, ss0_ref, ss1_ref, *, tq, tks):
    hpg = HEADS_PER_GROUP
    M = hpg * tq
    ss_ref = (ss0_ref, ss1_ref)
    t0 = pl.program_id(2) * tq
    bias = bias_ref[...]
    for h in range(hpg):
        qs_ref[h * tq:(h + 1) * tq, :HEAD_DIM] = q_ref[:, h * HEAD_DIM:(h + 1) * HEAD_DIM]
        qs_ref[h * tq:(h + 1) * tq, HEAD_DIM:] = bias

    def reset():
        m_ref[...] = jnp.full_like(m_ref, NEG)
        acc_ref[...] = jnp.zeros_like(acc_ref)

    def step(s, v):
        m_old = m_ref[...]
        m_new = jnp.maximum(m_old, jnp.max(s, axis=-1, keepdims=True))
        alpha = jnp.exp2(m_old - m_new)
        p = jnp.exp2(s - jnp.tile(m_new, (1, s.shape[1] // LANES)))
        acc_ref[...] = (jnp.tile(alpha, (1, 2)) * acc_ref[...]
                        + jnp.dot(p.astype(BF16), v, preferred_element_type=F32))
        m_ref[...] = m_new

    def result():
        return acc_ref[:, :HEAD_DIM] * (1.0 / acc_ref[:, HEAD_DIM:])

    def masked(s, keep):
        tk = s.shape[1]
        return jnp.where(keep, s.reshape(hpg, tq, tk), NEG).reshape(M, tk)

    def positions(tk):
        t = t0 + lax.broadcasted_iota(jnp.int32, (1, tq, tk), 1)
        kpos = lax.broadcasted_iota(jnp.int32, (1, tq, tk), 2)
        return t, kpos

    def sel_base(j):
        return pl.multiple_of(j * tks, tks)

    def sel_scores(j):
        base = sel_base(j)
        k = ksa_ref[pl.ds(base, tks), :]
        s = lax.dot_general(qs_ref[...], k, NT_DIMS, preferred_element_type=F32)
        t, kpos = positions(tks)
        return masked(s, kpos + base <= t)

    sel_last = (t0 + tq - 1) // tks

    def by_parity(n, fn):
        @pl.when(n % 2 == 0)
        def _():
            fn(0, 1)

        @pl.when(n % 2 == 1)
        def _():
            fn(1, 0)

    reset()
    ss_ref[0][...] = sel_scores(0)

    def sel_body(j, carry):
        def run(cur, nxt):
            ss_ref[nxt][...] = sel_scores(j + 1)
            step(ss_ref[cur][...], vs_ref[pl.ds(sel_base(j), tks), :])

        by_parity(j, run)
        return carry

    lax.fori_loop(0, sel_last, sel_body, 0)
    by_parity(sel_last,
              lambda cur, nxt: step(ss_ref[cur][...], vs_ref[pl.ds(sel_base(sel_last), tks), :]))
    o_s = result()

    sig = jax.nn.sigmoid(gate_ref[...])
    for h in range(hpg):
        rows = slice(h * tq, (h + 1) * tq)
        cols = slice(h * HEAD_DIM, (h + 1) * HEAD_DIM)
        o = oc_ref[:, cols].astype(F32) + sig[:, hpg + h:hpg + h + 1] * o_s[rows]
        o_ref[:, cols] = o.astype(o_ref.dtype)


def _sw_attn(qn, bias, ksa, vs, o_c, head_gates, *, batch, seq, tq=256, tks=512):
    T = qn.shape[0]
    G = N_KV_GROUPS
    nq = seq // tq
    gw = HEADS_PER_GROUP * HEAD_DIM
    M = HEADS_PER_GROUP * tq
    rowg = lambda b, g, i: (b * nq + i, g)
    seqg = lambda b, g, i: (b, g)
    return pl.pallas_call(
        functools.partial(_sw_attn_kernel, tq=tq, tks=tks),
        grid=(batch, G, nq),
        in_specs=[
            pl.BlockSpec((tq, gw), rowg),
            pl.BlockSpec((tq, LANES), rowg),
            pl.BlockSpec((seq, 2 * HEAD_DIM), seqg),
            pl.BlockSpec((seq, 2 * HEAD_DIM), seqg),
            pl.BlockSpec((tq, gw), rowg),
            pl.BlockSpec((tq, LANES), rowg),
        ],
        out_specs=pl.BlockSpec((tq, gw), rowg),
        out_shape=jax.ShapeDtypeStruct((T, B_WIDTH), BF16),
        scratch_shapes=[pltpu.VMEM((M, 2 * HEAD_DIM), BF16), pltpu.VMEM((M, LANES), F32),
                        pltpu.VMEM((M, 2 * HEAD_DIM), F32),
                        pltpu.VMEM((M, tks), F32), pltpu.VMEM((M, tks), F32)],
        compiler_params=_params("parallel", "parallel", "parallel"),
        name="sel_attn",
    )(qn, bias, ksa, vs, o_c, head_gates)


def _merge_kernel(oa_ref, ob_ref, ga_ref, gb_ref, wa_ref, wb_ref, o_ref):
    a = jnp.dot(oa_ref[...], wa_ref[...].astype(BF16), preferred_element_type=F32)
    b = jnp.dot(ob_ref[...], wb_ref[...].astype(BF16), preferred_element_type=F32)
    ga = jax.nn.sigmoid(ga_ref[...].astype(F32))
    gb = jax.nn.sigmoid(gb_ref[...].astype(F32))
    o_ref[...] = (ga * a + gb * b).astype(o_ref.dtype)


def _merge(o_a, o_b, gab, wa, wb, *, tm=1024, tn=512):
    T, D = o_a.shape
    N = wa.shape[1]
    nn = N // tn
    return pl.pallas_call(
        _merge_kernel,
        grid=(T // tm, nn),
        in_specs=[
            pl.BlockSpec((tm, D), lambda i, j: (i, 0)),
            pl.BlockSpec((tm, D), lambda i, j: (i, 0)),
            pl.BlockSpec((tm, tn), lambda i, j: (i, j)),
            pl.BlockSpec((tm, tn), lambda i, j: (i, nn + j)),
            pl.BlockSpec((D, tn), lambda i, j: (0, j)),
            pl.BlockSpec((D, tn), lambda i, j: (0, j)),
        ],
        out_specs=pl.BlockSpec((tm, tn), lambda i, j: (i, j)),
        out_shape=jax.ShapeDtypeStruct((T, N), BF16),
        compiler_params=_params("parallel", "arbitrary"),
        name="merge",
    )(o_a, o_b, gab, gab, wa, wb)


def _out_kernel(x_ref, m_ref, w_ref, o_ref):
    o_ref[...] = x_ref[...] + jnp.dot(m_ref[...], w_ref[...].astype(BF16),
                                      preferred_element_type=F32)


def _out_proj(x, merged, w, *, tm=1024, tn=512):
    T, D = merged.shape
    N = w.shape[1]
    return pl.pallas_call(
        _out_kernel,
        grid=(T // tm, N // tn),
        in_specs=[
            pl.BlockSpec((tm, tn), lambda i, j: (i, j)),
            pl.BlockSpec((tm, D), lambda i, j: (i, 0)),
            pl.BlockSpec((D, tn), lambda i, j: (0, j)),
        ],
        out_specs=pl.BlockSpec((tm, tn), lambda i, j: (i, j)),
        out_shape=jax.ShapeDtypeStruct((T, N), F32),
        compiler_params=_params("parallel", "arbitrary"),
        name="out_proj",
    )(x, merged, w)


def _rope_tables(pos):
    half = ROPE_DIM // 2
    inv = ROPE_THETA ** (-2.0 * jnp.arange(half, dtype=F32) / ROPE_DIM)
    ang = pos.astype(F32)[:, None] * inv
    cos, sin = jnp.cos(ang), jnp.sin(ang)
    n = pos.shape[0]
    z16 = jnp.zeros((n, half), F32)
    rest = HEAD_DIM - ROPE_DIM
    c = jnp.concatenate([cos, cos, jnp.ones((n, rest), F32)], axis=1)
    su = jnp.concatenate([z16, sin, jnp.zeros((n, rest), F32)], axis=1)
    sd = jnp.concatenate([-sin, z16, jnp.zeros((n, rest), F32)], axis=1)
    return c, su, sd


def _gate_weights(w_in):
    sizes = [A_WIDTH, A_WIDTH, B_WIDTH] + [KV_WIDTH] * 6 + [3 * N_HEADS, D_MODEL, D_MODEL]
    offs = np.concatenate([[0], np.cumsum(sizes)])
    w_gate = w_in[:, offs[10]:]
    gates = w_in[:, offs[9]:offs[10]]
    blocks = []
    for g in range(N_KV_GROUPS):
        cols = [gates[:, br * N_HEADS + g * HEADS_PER_GROUP:br * N_HEADS + (g + 1) * HEADS_PER_GROUP]
                for br in range(3)]
        pad = jnp.zeros((w_in.shape[0], LANES - 3 * HEADS_PER_GROUP), w_in.dtype)
        blocks.extend(cols + [pad])
    return w_gate, jnp.concatenate(blocks, axis=1)


def _overlap_t(nb_pad, n_sel):
    cmp_start = np.arange(nb_pad) * CMP_STRIDE
    sel_start = np.arange(n_sel) * SEL_BLOCK
    ov = ((cmp_start[None, :] < sel_start[:, None] + SEL_BLOCK)
          & (cmp_start[None, :] + CMP_BLOCK > sel_start[:, None]))
    return jnp.asarray(ov, dtype=BF16)


def _layer(x, ffn1_norm, ffn1_w_gate, ffn1_w_up, ffn1_w_down, mix_norm, w_in,
           a_v_norm, a_w_s, a_b_s, q_norm, k_cmp_norm, k_slc_norm, k_win_norm,
           cmp_k_pe, cmp_k_w1, cmp_k_w2, cmp_v_pe, cmp_v_w1, cmp_v_w2,
           w_branch_a, w_branch_b, w_out, ffn2_norm, ffn2_w_gate, ffn2_w_up, ffn2_w_down):
    B, S, D = x.shape
    T = B * S
    G = N_KV_GROUPS
    row = lambda v: v.reshape(1, -1)
    x0 = x.reshape(T, D)

    x1 = _ffn(x0, row(ffn1_norm), ffn1_w_gate, ffn1_w_up, ffn1_w_down)

    uv, gab, proj_b, head_gates = _proj(x1, row(mix_norm), w_in, *_gate_weights(w_in))

    o_a = _gmlp(uv, row(a_v_norm), a_w_s, a_b_s.T)

    pos = jnp.arange(S)
    qn, ksa, vs, kwn, vw = _prep(proj_b, row(q_norm), row(k_slc_norm), row(k_win_norm),
                                 *_rope_tables(pos), seq=S)

    nb_pad = S // CMP_STRIDE
    cmp_end = jnp.arange(nb_pad) * CMP_STRIDE + (CMP_BLOCK - 1)
    kc, vc = _compress(proj_b, cmp_k_pe.reshape(1, -1), cmp_v_pe.reshape(1, -1),
                       cmp_k_w1.astype(BF16), cmp_k_w2.astype(BF16),
                       cmp_v_w1.astype(BF16), cmp_v_w2.astype(BF16),
                       row(k_cmp_norm), *_rope_tables(cmp_end), batch=B, seq=S)

    o_cw, bias = _cmp_attn(qn, kc, vc, kwn, vw, head_gates, _overlap_t(nb_pad, S // SEL_BLOCK),
                           batch=B, seq=S)
    o_b = _sw_attn(qn, bias, ksa, vs, o_cw, head_gates, batch=B, seq=S)

    merged = _merge(o_a, o_b, gab, w_branch_a, w_branch_b)
    x2 = _out_proj(x1, merged, w_out)

    x3 = _ffn(x2, row(ffn2_norm), ffn2_w_gate, ffn2_w_up, ffn2_w_down)
    return x3.reshape(B, S, D)


def kernel(x, ffn1_norm, ffn1_w_gate, ffn1_w_up, ffn1_w_down, mix_norm, w_in, a_v_norm, a_w_s, a_b_s, q_norm, k_cmp_norm, k_slc_norm, k_win_norm, cmp_k_pe, cmp_k_w1, cmp_k_w2, cmp_v_pe, cmp_v_w1, cmp_v_w2, w_branch_a, w_branch_b, w_out, ffn2_norm, ffn2_w_gate, ffn2_w_up, ffn2_w_down):
    params = (ffn1_norm, ffn1_w_gate, ffn1_w_up, ffn1_w_down, mix_norm, w_in,
              a_v_norm, a_w_s, a_b_s, q_norm, k_cmp_norm, k_slc_norm, k_win_norm,
              cmp_k_pe, cmp_k_w1, cmp_k_w2, cmp_v_pe, cmp_v_w1, cmp_v_w2,
              w_branch_a, w_branch_b, w_out, ffn2_norm, ffn2_w_gate, ffn2_w_up, ffn2_w_down)
    for l in range(params[0].shape[0]):
        x = _layer(x, *[p[l] for p in params])
    return x
```

```python
import functools

import numpy as np
import jax
import jax.numpy as jnp
from jax import lax
from jax.experimental import pallas as pl
from jax.experimental.pallas import tpu as pltpu

D_MODEL = 2048
D_FF = 5504
A_WIDTH = 2048
A_GROUPS = 8
A_CHUNK = 128
N_HEADS = 16
HEAD_DIM = 128
N_KV_GROUPS = 2
HEADS_PER_GROUP = N_HEADS // N_KV_GROUPS
CMP_BLOCK = 32
CMP_STRIDE = 16
CMP_HIDDEN = 256
SEL_BLOCK = 64
SEL_TOP = 16
WINDOW = 512
ROPE_THETA = 500000.0
ROPE_DIM = HEAD_DIM // 4
EPS = 1e-6
NEG = -1e30
FORCE_BONUS = 1e4
LOG2_E = 1.4426950408889634
B_WIDTH = N_HEADS * HEAD_DIM
KV_WIDTH = N_KV_GROUPS * HEAD_DIM

LANES = 128
VMEM_LIMIT = 56 * 1024 * 1024

F32 = jnp.float32
BF16 = jnp.bfloat16
NT_DIMS = (((1,), (1,)), ((), ()))


def _params(*sem):
    return pltpu.CompilerParams(dimension_semantics=sem, vmem_limit_bytes=VMEM_LIMIT)


def _rms(x, g):
    ms = jnp.mean(x * x, axis=-1, keepdims=True)
    return x * lax.rsqrt(ms + EPS) * g


def _ffn_kernel(x_ref, g_ref, wg_ref, wu_ref, wd_ref, o_ref, h_ref, *, nf, tf, f_last):
    j = pl.program_id(1)

    def partial(width):
        h = h_ref[...]
        a = jnp.dot(h, wg_ref[:, :width].astype(BF16), preferred_element_type=F32)
        b = jnp.dot(h, wu_ref[:, :width].astype(BF16), preferred_element_type=F32)
        act = (a * jax.nn.sigmoid(a) * b).astype(BF16)
        return jnp.dot(act, wd_ref[:width, :].astype(BF16), preferred_element_type=F32)

    @pl.when(j == 0)
    def _():
        h_ref[...] = _rms(x_ref[...], g_ref[...]).astype(BF16)
        o_ref[...] = partial(f_last)

    @pl.when((j > 0) & (j < nf - 1))
    def _():
        o_ref[...] += partial(tf)

    @pl.when(j == nf - 1)
    def _():
        o_ref[...] = x_ref[...] + 0.5 * (o_ref[...] + partial(tf))


def _ffn(x, norm, wg, wu, wd, *, tm=1024, tf=256):
    T, D = x.shape
    F = wg.shape[1]
    nf = pl.cdiv(F, tf)
    f_last = F - (nf - 1) * tf
    hid = lambda j: (j + nf - 1) % nf
    return pl.pallas_call(
        functools.partial(_ffn_kernel, nf=nf, tf=tf, f_last=f_last),
        grid=(T // tm, nf),
        in_specs=[
            pl.BlockSpec((tm, D), lambda i, j: (i, 0)),
            pl.BlockSpec((1, D), lambda i, j: (0, 0)),
            pl.BlockSpec((D, tf), lambda i, j: (0, hid(j))),
            pl.BlockSpec((D, tf), lambda i, j: (0, hid(j))),
            pl.BlockSpec((tf, D), lambda i, j: (hid(j), 0)),
        ],
        out_specs=pl.BlockSpec((tm, D), lambda i, j: (i, 0)),
        out_shape=jax.ShapeDtypeStruct((T, D), F32),
        scratch_shapes=[pltpu.VMEM((tm, D), BF16)],
        compiler_params=_params("parallel", "arbitrary"),
        name="ffn",
    )(x, norm, wg, wu, wd)


def _proj_kernel(x_ref, g_ref, win_ref, wgate_ref, whead_ref, uv_ref, gab_ref, qkv_ref, hg_ref,
                 h_ref, *, n_uv, n_gab, n_qkv):
    j = pl.program_id(1)

    @pl.when(j == 0)
    def _():
        h_ref[...] = _rms(x_ref[...], g_ref[...]).astype(BF16)

    def project(w_ref, o_ref):
        o_ref[...] = jnp.dot(h_ref[...], w_ref[...],
                             preferred_element_type=F32).astype(o_ref.dtype)

    @pl.when(j < n_uv)
    def _():
        project(win_ref, uv_ref)

    @pl.when((j >= n_uv) & (j < n_uv + n_gab))
    def _():
        project(wgate_ref, gab_ref)

    @pl.when((j >= n_uv + n_gab) & (j < n_uv + n_gab + n_qkv))
    def _():
        project(win_ref, qkv_ref)

    @pl.when(j == n_uv + n_gab + n_qkv)
    def _():
        project(whead_ref, hg_ref)


def _proj(x, norm, w_in, w_gate, w_head, *, tm=1024, tn=512):
    T, D = x.shape
    n_uv = 2 * A_WIDTH // tn
    n_gab = w_gate.shape[1] // tn
    qkv_w = B_WIDTH + 6 * KV_WIDTH
    n_qkv = qkv_w // tn
    steps = n_uv + n_gab + n_qkv + 1
    clip = lambda v, lo, hi: jnp.minimum(jnp.maximum(v, lo), hi)
    win_blk = lambda j: jnp.where(j < n_uv + n_gab, jnp.minimum(j, n_uv - 1),
                                  jnp.minimum(j - n_gab, n_uv + n_qkv - 1))
    return pl.pallas_call(
        functools.partial(_proj_kernel, n_uv=n_uv, n_gab=n_gab, n_qkv=n_qkv),
        grid=(T // tm, steps),
        in_specs=[
            pl.BlockSpec((tm, D), lambda i, j: (i, 0)),
            pl.BlockSpec((1, D), lambda i, j: (0, 0)),
            pl.BlockSpec((D, tn), lambda i, j: (0, win_blk(j))),
            pl.BlockSpec((D, tn), lambda i, j: (0, clip(j - n_uv, 0, n_gab - 1))),
            pl.BlockSpec(w_head.shape, lambda i, j: (0, 0)),
        ],
        out_specs=[
            pl.BlockSpec((tm, tn), lambda i, j: (i, jnp.minimum(j, n_uv - 1))),
            pl.BlockSpec((tm, tn), lambda i, j: (i, clip(j - n_uv, 0, n_gab - 1))),
            pl.BlockSpec((tm, tn), lambda i, j: (i, clip(j - n_uv - n_gab, 0, n_qkv - 1))),
            pl.BlockSpec((tm, w_head.shape[1]), lambda i, j: (i, 0)),
        ],
        out_shape=[jax.ShapeDtypeStruct((T, 2 * A_WIDTH), BF16),
                   jax.ShapeDtypeStruct((T, w_gate.shape[1]), BF16),
                   jax.ShapeDtypeStruct((T, qkv_w), F32),
                   jax.ShapeDtypeStruct((T, w_head.shape[1]), F32)],
        scratch_shapes=[pltpu.VMEM((tm, D), BF16)],
        compiler_params=_params("parallel", "arbitrary"),
        name="proj",
    )(x, norm, w_in, w_gate, w_head)


def _gmlp_kernel(u_ref, v_ref, vn_ref, ws_ref, bs_ref, o_ref, *, tr):
    gw = A_WIDTH // A_GROUPS
    v = jax.nn.gelu(v_ref[...].astype(F32))
    vn = _rms(v, vn_ref[...]).astype(BF16)
    row = lax.broadcasted_iota(jnp.int32, (A_CHUNK, A_CHUNK), 0)
    col = lax.broadcasted_iota(jnp.int32, (A_CHUNK, A_CHUNK), 1)
    for g in range(A_GROUPS):
        w = jnp.where(col <= row, ws_ref[g], 0.0).astype(BF16)
        bias = bs_ref[:, g:g + 1]
        for c in range(tr // A_CHUNK):
            rows = slice(c * A_CHUNK, (c + 1) * A_CHUNK)
            cols = slice(g * gw, (g + 1) * gw)
            mixed = jnp.dot(w, vn[rows, cols], preferred_element_type=F32) + bias
            u = jax.nn.gelu(u_ref[rows, cols].astype(F32))
            o_ref[rows, cols] = (u * mixed).astype(o_ref.dtype)


def _gmlp(proj_a, a_v_norm, w_s, b_s_t, *, tr=512):
    T = proj_a.shape[0]
    return pl.pallas_call(
        functools.partial(_gmlp_kernel, tr=tr),
        grid=(T // tr,),
        in_specs=[
            pl.BlockSpec((tr, A_WIDTH), lambda i: (i, 0)),
            pl.BlockSpec((tr, A_WIDTH), lambda i: (i, 1)),
            pl.BlockSpec((1, A_WIDTH), lambda i: (0, 0)),
            pl.BlockSpec((A_GROUPS, A_CHUNK, A_CHUNK), lambda i: (0, 0, 0)),
            pl.BlockSpec((A_CHUNK, A_GROUPS), lambda i: (0, 0)),
        ],
        out_specs=pl.BlockSpec((tr, A_WIDTH), lambda i: (i, 0)),
        out_shape=jax.ShapeDtypeStruct((T, A_WIDTH), BF16),
        compiler_params=_params("parallel"),
        name="gmlp",
    )(proj_a, proj_a, a_v_norm, w_s, b_s_t)


def _rope(y, c, su, sd):
    return y * c + pltpu.roll(y, 16, 1) * su + pltpu.roll(y, HEAD_DIM - 16, 1) * sd


def _norm_rope_mxu(x, g, c, s, w):
    xg = x * g
    lhs = jnp.concatenate([(x * x).astype(BF16), xg.astype(BF16)], axis=1)
    res = jnp.dot(lhs, w, preferred_element_type=F32)
    r = lax.rsqrt(res[:, :HEAD_DIM] * (1.0 / HEAD_DIM) + EPS)
    return (xg * c + res[:, HEAD_DIM:] * s) * r


def _prep_kernel(q_ref, ks_ref, vs_ref, kw_ref, vw_ref, qg_ref, ksg_ref, kwg_ref,
                 cq_ref, sq_ref, ck_ref, sk_ref, w_ref,
                 qn_ref, ksa_ref, vso_ref, kwn_ref, vwo_ref, *, tr, seq):
    cq, sq, ck, sk, w = cq_ref[...], sq_ref[...], ck_ref[...], sk_ref[...], w_ref[...]
    for h in range(N_HEADS):
        cols = slice(h * HEAD_DIM, (h + 1) * HEAD_DIM)
        qn_ref[:, cols] = _norm_rope_mxu(q_ref[:, cols], qg_ref[...], cq, sq, w).astype(BF16)
    t0 = (pl.program_id(0) * tr) % seq
    t = t0 + lax.broadcasted_iota(jnp.int32, (tr, LANES), 0)
    lane = lax.broadcasted_iota(jnp.int32, (tr, LANES), 1)
    onehot = jnp.where((t // SEL_BLOCK) == lane, 1.0, 0.0).astype(BF16)
    ones = jnp.ones((tr, HEAD_DIM), BF16)
    for g in range(N_KV_GROUPS):
        cols = slice(g * HEAD_DIM, (g + 1) * HEAD_DIM)
        lo = slice(2 * g * HEAD_DIM, (2 * g + 1) * HEAD_DIM)
        hi = slice((2 * g + 1) * HEAD_DIM, (2 * g + 2) * HEAD_DIM)
        ksa_ref[:, lo] = _norm_rope_mxu(ks_ref[:, cols], ksg_ref[...], ck, sk, w).astype(BF16)
        ksa_ref[:, hi] = onehot
        kwn_ref[:, cols] = _norm_rope_mxu(kw_ref[:, cols], kwg_ref[...], ck, sk, w).astype(BF16)
        vso_ref[:, lo] = vs_ref[:, cols].astype(BF16)
        vso_ref[:, hi] = ones
        vwo_ref[:, lo] = vw_ref[:, cols].astype(BF16)
        vwo_ref[:, hi] = ones


def _rope_matrix():
    half = ROPE_DIM // 2
    w = np.zeros((2 * HEAD_DIM, 2 * HEAD_DIM), np.float32)
    w[:HEAD_DIM, :HEAD_DIM] = 1.0
    for j in range(half):
        w[HEAD_DIM + j + half, HEAD_DIM + j] = 1.0
        w[HEAD_DIM + j, HEAD_DIM + j + half] = 1.0
    return jnp.asarray(w, dtype=BF16)


def _prep(proj_b, q_norm, k_slc_norm, k_win_norm, rope_c, rope_su, rope_sd, *, seq, tr=512):
    T = proj_b.shape[0]
    kvb = B_WIDTH // KV_WIDTH
    nrt = seq // tr
    q_scale = HEAD_DIM ** -0.5 * LOG2_E
    rope_s = rope_su + rope_sd
    w = _rope_matrix()
    row = lambda i: (i, 0)
    tab = pl.BlockSpec((tr, HEAD_DIM), lambda i: (i % nrt, 0))
    gain = pl.BlockSpec((1, HEAD_DIM), lambda i: (0, 0))
    kv_in = lambda k: pl.BlockSpec((tr, KV_WIDTH), lambda i: (i, kvb + k))
    return pl.pallas_call(
        functools.partial(_prep_kernel, tr=tr, seq=seq),
        grid=(T // tr,),
        in_specs=[pl.BlockSpec((tr, B_WIDTH), row), kv_in(2), kv_in(3), kv_in(4), kv_in(5),
                  gain, gain, gain, tab, tab, tab, tab, pl.BlockSpec(w.shape, lambda i: (0, 0))],
        out_specs=[pl.BlockSpec((tr, B_WIDTH), row), pl.BlockSpec((tr, 2 * KV_WIDTH), row),
                   pl.BlockSpec((tr, 2 * KV_WIDTH), row), pl.BlockSpec((tr, KV_WIDTH), row),
                   pl.BlockSpec((tr, 2 * KV_WIDTH), row)],
        out_shape=[jax.ShapeDtypeStruct((T, B_WIDTH), BF16),
                   jax.ShapeDtypeStruct((T, 2 * KV_WIDTH), BF16),
                   jax.ShapeDtypeStruct((T, 2 * KV_WIDTH), BF16),
                   jax.ShapeDtypeStruct((T, KV_WIDTH), BF16),
                   jax.ShapeDtypeStruct((T, 2 * KV_WIDTH), BF16)],
        compiler_params=_params("parallel"),
        name="qk_prep",
    )(proj_b, proj_b, proj_b, proj_b, proj_b, q_norm, k_slc_norm, k_win_norm,
      rope_c * q_scale, rope_s * q_scale, rope_c, rope_s, w)


def _compress_kernel(zk_ref, zv_ref, pek_ref, pev_ref, w1k_ref, w2k_ref, w1v_ref, w2v_ref,
                     kg_ref, c_ref, su_ref, sd_ref, kc_ref, vc_ref, zr_ref):
    half = CMP_STRIDE * HEAD_DIM
    nb = zr_ref.shape[0]

    def comp(z_ref, pe_ref, w1_ref, w2_ref):
        for l in range(CMP_STRIDE):
            zr_ref[:, l * HEAD_DIM:(l + 1) * HEAD_DIM] = z_ref[pl.ds(l, nb, stride=CMP_STRIDE), :]
        z = zr_ref[...]
        top = jnp.dot((z + pe_ref[:, :half]).astype(BF16), w1_ref[:half, :], preferred_element_type=F32)
        bot = jnp.dot((z + pe_ref[:, half:]).astype(BF16), w1_ref[half:, :], preferred_element_type=F32)
        hid = top + pltpu.roll(bot, nb - 1, 0)
        return jnp.dot(jax.nn.gelu(hid).astype(BF16), w2_ref[...], preferred_element_type=F32)

    k = comp(zk_ref, pek_ref, w1k_ref, w2k_ref)
    kc_ref[...] = _rope(_rms(k, kg_ref[...]), c_ref[...], su_ref[...], sd_ref[...]).astype(BF16)
    vc_ref[...] = comp(zv_ref, pev_ref, w1v_ref, w2v_ref).astype(BF16)


def _compress(proj_b, pek, pev, w1k, w2k, w1v, w2v, k_cmp_norm, cmp_c, cmp_su, cmp_sd, *, batch, seq):
    G = N_KV_GROUPS
    nb = seq // CMP_STRIDE
    kc_blk = B_WIDTH // HEAD_DIM
    vc_blk = kc_blk + G
    ospec = pl.BlockSpec((None, None, nb, HEAD_DIM), lambda b, g: (b, g, 0, 0))
    full = lambda a: pl.BlockSpec(a.shape, lambda b, g: (0,) * a.ndim)
    consts = (pek, pev, w1k, w2k, w1v, w2v, k_cmp_norm, cmp_c, cmp_su, cmp_sd)
    return pl.pallas_call(
        _compress_kernel,
        grid=(batch, G),
        in_specs=[pl.BlockSpec((seq, HEAD_DIM), lambda b, g: (b, kc_blk + g)),
                  pl.BlockSpec((seq, HEAD_DIM), lambda b, g: (b, vc_blk + g))]
                 + [full(a) for a in consts],
        out_specs=[ospec, ospec],
        out_shape=[jax.ShapeDtypeStruct((batch, G, nb, HEAD_DIM), BF16)] * 2,
        scratch_shapes=[pltpu.VMEM((nb, CMP_STRIDE * HEAD_DIM), F32)],
        compiler_params=_params("parallel", "parallel"),
        name="compress",
    )(proj_b, proj_b, *consts)


def _cmp_attn_kernel(q_ref, kc_ref, vc_ref, kw_ref, vw_ref, gate_ref, ovt_ref, oc_ref, bias_ref,
                     qs_ref, *, tq):
    hpg = HEADS_PER_GROUP
    M = hpg * tq
    nb = kc_ref.shape[0]
    n_sel = ovt_ref.shape[0]
    t0 = pl.program_id(2) * tq
    for h in range(hpg):
        qs_ref[h * tq:(h + 1) * tq, :] = q_ref[:, h * HEAD_DIM:(h + 1) * HEAD_DIM]

    wlen = WINDOW + tq
    kstart = pl.multiple_of(jnp.maximum(t0 - WINDOW, 0), tq)
    sw = lax.dot_general(qs_ref[...], kw_ref[pl.ds(kstart, wlen), :], NT_DIMS,
                         preferred_element_type=F32)
    tw = t0 + lax.broadcasted_iota(jnp.int32, (1, tq, wlen), 1)
    kp = kstart + lax.broadcasted_iota(jnp.int32, (1, tq, wlen), 2)
    keep = (tw - kp).astype(jnp.uint32) < WINDOW
    sw = jnp.where(keep, sw.reshape(hpg, tq, wlen), NEG).reshape(M, wlen)
    pw = jnp.exp2(sw - jnp.max(sw, axis=-1, keepdims=True))
    aw = jnp.dot(pw.astype(BF16), vw_ref[pl.ds(kstart, wlen), :], preferred_element_type=F32)
    o_w = aw[:, :HEAD_DIM] * (1.0 / aw[:, HEAD_DIM:])

    s = lax.dot_general(qs_ref[...], kc_ref[...], NT_DIMS, preferred_element_type=F32)
    s = s.reshape(hpg, tq, nb)
    t = t0 + lax.broadcasted_iota(jnp.int32, (1, tq, nb), 1)
    n = lax.broadcasted_iota(jnp.int32, (1, tq, nb), 2)
    mask = n * CMP_STRIDE + (CMP_BLOCK - 1) <= t
    s = jnp.where(mask, s, NEG)
    e = jnp.exp2(s - jnp.max(s, axis=-1, keepdims=True))
    inv = 1.0 / jnp.sum(e, axis=-1, keepdims=True)
    p = jnp.where(mask, e * inv, 0.0)
    o = jnp.dot(p.reshape(hpg * tq, nb).astype(BF16), vc_ref[...], preferred_element_type=F32)
    sig = jax.nn.sigmoid(gate_ref[...])
    for h in range(hpg):
        rows = slice(h * tq, (h + 1) * tq)
        oc_ref[:, h * HEAD_DIM:(h + 1) * HEAD_DIM] = (
            o[rows] * sig[:, h:h + 1]
            + o_w[rows] * sig[:, 2 * hpg + h:2 * hpg + h + 1]).astype(oc_ref.dtype)

    psum = jnp.sum(p, axis=0)
    ovt = ovt_ref[...]
    hi = psum.astype(BF16)
    r1 = psum - hi.astype(F32)
    mid = r1.astype(BF16)
    lo = (r1 - mid.astype(F32)).astype(BF16)
    imp = (lax.dot_general(ovt, hi, NT_DIMS, preferred_element_type=F32)
           + lax.dot_general(ovt, mid, NT_DIMS, preferred_element_type=F32)
           + lax.dot_general(ovt, lo, NT_DIMS, preferred_element_type=F32))
    j = lax.broadcasted_iota(jnp.int32, (n_sel, tq), 0)
    tt = t0 + lax.broadcasted_iota(jnp.int32, (n_sel, tq), 1)
    cur = tt // SEL_BLOCK
    forced = jnp.where((j == 0) | (j == cur) | (j == cur - 1), FORCE_BONUS, 0.0)
    score = jnp.where(j * SEL_BLOCK <= tt, imp + forced, NEG)
    sub = 8
    groups = [score[k * sub:(k + 1) * sub, :] for k in range(n_sel // sub)]
    ranks = [jnp.zeros((sub, tq), F32) for _ in groups]
    jrow = lax.broadcasted_iota(jnp.int32, (sub, tq), 0)
    for i in range(n_sel):
        r = jnp.broadcast_to(score[i:i + 1, :], (sub, tq))
        for k, grp in enumerate(groups):
            if k * sub > i:
                beats = r >= grp
            elif (k + 1) * sub <= i:
                beats = r > grp
            else:
                beats = jnp.where(jrow + k * sub > i, jnp.where(r >= grp, 1.0, 0.0),
                                  jnp.where(r > grp, 1.0, 0.0)) > 0.5
            ranks[k] = ranks[k] + jnp.where(beats, 1.0, 0.0)
    rank = jnp.concatenate(ranks, axis=0)
    bias = jnp.where(rank < SEL_TOP, 0.0, NEG)
    bias = jnp.concatenate([bias, jnp.zeros((LANES - n_sel, tq), F32)], axis=0)
    bias_ref[...] = bias.T.astype(bias_ref.dtype)


def _cmp_attn(qn, kc, vc, kwn, vw, head_gates, ovt, *, batch, seq, tq=256):
    assert WINDOW % tq == 0
    T = qn.shape[0]
    G = N_KV_GROUPS
    nq = seq // tq
    gw = HEADS_PER_GROUP * HEAD_DIM
    nb = kc.shape[2]
    rowg = lambda b, g, i: (b * nq + i, g)
    seqg = lambda b, g, i: (b, g)
    return pl.pallas_call(
        functools.partial(_cmp_attn_kernel, tq=tq),
        grid=(batch, G, nq),
        in_specs=[
            pl.BlockSpec((tq, gw), rowg),
            pl.BlockSpec((None, None, nb, HEAD_DIM), lambda b, g, i: (b, g, 0, 0)),
            pl.BlockSpec((None, None, nb, HEAD_DIM), lambda b, g, i: (b, g, 0, 0)),
            pl.BlockSpec((seq, HEAD_DIM), seqg),
            pl.BlockSpec((seq, 2 * HEAD_DIM), seqg),
            pl.BlockSpec((tq, LANES), rowg),
            pl.BlockSpec(ovt.shape, lambda b, g, i: (0, 0)),
        ],
        out_specs=[pl.BlockSpec((tq, gw), rowg), pl.BlockSpec((tq, LANES), rowg)],
        out_shape=[jax.ShapeDtypeStruct((T, B_WIDTH), BF16),
                   jax.ShapeDtypeStruct((T, G * LANES), BF16)],
        scratch_shapes=[pltpu.VMEM((HEADS_PER_GROUP * tq, HEAD_DIM), BF16)],
        compiler_params=_params("parallel", "parallel", "parallel"),
        name="cmp_win_attn",
    )(qn, kc, vc, kwn, vw, head_gates, ovt)


def _sw_attn_kernel(q_ref, bias_ref, ksa_ref, vs_ref, oc_ref, gate_ref, o_ref,
                    qs_ref, m_ref, acc_ref, ss0_ref, ss1_ref, *, tq, tks):
    hpg = HEADS_PER_GROUP
    M = hpg * tq
    ss_ref = (ss0_ref, ss1_ref)
    t0 = pl.program_id(2) * tq
    bias = bias_ref[...]
    for h in range(hpg):
        qs_ref[h * tq:(h + 1) * tq, :HEAD_DIM] = q_ref[:, h * HEAD_DIM:(h + 1) * HEAD_DIM]
        qs_ref[h * tq:(h + 1) * tq, HEAD_DIM:] = bias

    def reset():
        m_ref[...] = jnp.full_like(m_ref, NEG)
        acc_ref[...] = jnp.zeros_like(acc_ref)

    def step(s, v):
        m_old = m_ref[...]
        m_new = jnp.maximum(m_old, jnp.max(s, axis=-1, keepdims=True))
        alpha = jnp.exp2(m_old - m_new)
        p = jnp.exp2(s - jnp.tile(m_new, (1, s.shape[1] // LANES)))
        acc_ref[...] = (jnp.tile(alpha, (1, 2)) * acc_ref[...]
                        + jnp.dot(p.astype(BF16), v, preferred_element_type=F32))
        m_ref[...] = m_new

    def result():
        return acc_ref[:, :HEAD_DIM] * (1.0 / acc_ref[:, HEAD_DIM:])

    def masked(s, keep):
        tk = s.shape[1]
        return jnp.where(keep, s.reshape(hpg, tq, tk), NEG).reshape(M, tk)

    def positions(tk):
        t = t0 + lax.broadcasted_iota(jnp.int32, (1, tq, tk), 1)
        kpos = lax.broadcasted_iota(jnp.int32, (1, tq, tk), 2)
        return t, kpos

    def sel_base(j):
        return pl.multiple_of(j * tks, tks)

    def sel_scores(j):
        base = sel_base(j)
        k = ksa_ref[pl.ds(base, tks), :]
        s = lax.dot_general(qs_ref[...], k, NT_DIMS, preferred_element_type=F32)
        t, kpos = positions(tks)
        return masked(s, kpos + base <= t)

    sel_last = (t0 + tq - 1) // tks

    def by_parity(n, fn):
        @pl.when(n % 2 == 0)
        def _():
            fn(0, 1)

        @pl.when(n % 2 == 1)
        def _():
            fn(1, 0)

    reset()
    ss_ref[0][...] = sel_scores(0)

    def sel_body(j, carry):
        def run(cur, nxt):
            ss_ref[nxt][...] = sel_scores(j + 1)
            step(ss_ref[cur][...], vs_ref[pl.ds(sel_base(j), tks), :])

        by_parity(j, run)
        return carry

    lax.fori_loop(0, sel_last, sel_body, 0)
    by_parity(sel_last,
              lambda cur, nxt: step(ss_ref[cur][...], vs_ref[pl.ds(sel_base(sel_last), tks), :]))
    o_s = result()

    sig = jax.nn.sigmoid(gate_ref[...])
    for h in range(hpg):
        rows = slice(h * tq, (h + 1) * tq)
        cols = slice(h * HEAD_DIM, (h + 1) * HEAD_DIM)
        o = oc_ref[:, cols].astype(F32) + sig[:, hpg + h:hpg + h + 1] * o_s[rows]
        o_ref[:, cols] = o.astype(o_ref.dtype)


def _sw_attn(qn, bias, ksa, vs, o_c, head_gates, *, batch, seq, tq=256, tks=512):
    T = qn.shape[0]
    G = N_KV_GROUPS
    nq = seq // tq
    gw = HEADS_PER_GROUP * HEAD_DIM
    M = HEADS_PER_GROUP * tq
    rowg = lambda b, g, i: (b * nq + i, g)
    seqg = lambda b, g, i: (b, g)
    return pl.pallas_call(
        functools.partial(_sw_attn_kernel, tq=tq, tks=tks),
        grid=(batch, G, nq),
        in_specs=[
            pl.BlockSpec((tq, gw), rowg),
            pl.BlockSpec((tq, LANES), rowg),
            pl.BlockSpec((seq, 2 * HEAD_DIM), seqg),
            pl.BlockSpec((seq, 2 * HEAD_DIM), seqg),
            pl.BlockSpec((tq, gw), rowg),
            pl.BlockSpec((tq, LANES), rowg),
        ],
        out_specs=pl.BlockSpec((tq, gw), rowg),
        out_shape=jax.ShapeDtypeStruct((T, B_WIDTH), BF16),
        scratch_shapes=[pltpu.VMEM((M, 2 * HEAD_DIM), BF16), pltpu.VMEM((M, LANES), F32),
                        pltpu.VMEM((M, 2 * HEAD_DIM), F32),
                        pltpu.VMEM((M, tks), F32), pltpu.VMEM((M, tks), F32)],
        compiler_params=_params("parallel", "parallel", "parallel"),
        name="sel_attn",
    )(qn, bias, ksa, vs, o_c, head_gates)


def _merge_kernel(oa_ref, ob_ref, ga_ref, gb_ref, wa_ref, wb_ref, o_ref):
    a = jnp.dot(oa_ref[...], wa_ref[...], preferred_element_type=F32)
    b = jnp.dot(ob_ref[...], wb_ref[...], preferred_element_type=F32)
    ga = jax.nn.sigmoid(ga_ref[...].astype(F32))
    gb = jax.nn.sigmoid(gb_ref[...].astype(F32))
    o_ref[...] = (ga * a + gb * b).astype(o_ref.dtype)


def _merge(o_a, o_b, gab, wa, wb, *, tm=1024, tn=512):
    T, D = o_a.shape
    N = wa.shape[1]
    nn = N // tn
    return pl.pallas_call(
        _merge_kernel,
        grid=(T // tm, nn),
        in_specs=[
            pl.BlockSpec((tm, D), lambda i, j: (i, 0)),
            pl.BlockSpec((tm, D), lambda i, j: (i, 0)),
            pl.BlockSpec((tm, tn), lambda i, j: (i, j)),
            pl.BlockSpec((tm, tn), lambda i, j: (i, nn + j)),
            pl.BlockSpec((D, tn), lambda i, j: (0, j)),
            pl.BlockSpec((D, tn), lambda i, j: (0, j)),
        ],
        out_specs=pl.BlockSpec((tm, tn), lambda i, j: (i, j)),
        out_shape=jax.ShapeDtypeStruct((T, N), BF16),
        compiler_params=_params("parallel", "arbitrary"),
        name="merge",
    )(o_a, o_b, gab, gab, wa, wb)


def _out_kernel(x_ref, m_ref, w_ref, o_ref):
    o_ref[...] = x_ref[...] + jnp.dot(m_ref[...], w_ref[...], preferred_element_type=F32)


def _out_proj(x, merged, w, *, tm=1024, tn=512):
    T, D = merged.shape
    N = w.shape[1]
    return pl.pallas_call(
        _out_kernel,
        grid=(T // tm, N // tn),
        in_specs=[
            pl.BlockSpec((tm, tn), lambda i, j: (i, j)),
            pl.BlockSpec((tm, D), lambda i, j: (i, 0)),
            pl.BlockSpec((D, tn), lambda i, j: (0, j)),
        ],
        out_specs=pl.BlockSpec((tm, tn), lambda i, j: (i, j)),
        out_shape=jax.ShapeDtypeStruct((T, N), F32),
        compiler_params=_params("parallel", "arbitrary"),
        name="out_proj",
    )(x, merged, w)


def _rope_tables(pos):
    half = ROPE_DIM // 2
    inv = ROPE_THETA ** (-2.0 * jnp.arange(half, dtype=F32) / ROPE_DIM)
    ang = pos.astype(F32)[:, None] * inv
    cos, sin = jnp.cos(ang), jnp.sin(ang)
    n = pos.shape[0]
    z16 = jnp.zeros((n, half), F32)
    rest = HEAD_DIM - ROPE_DIM
    c = jnp.concatenate([cos, cos, jnp.ones((n, rest), F32)], axis=1)
    su = jnp.concatenate([z16, sin, jnp.zeros((n, rest), F32)], axis=1)
    sd = jnp.concatenate([-sin, z16, jnp.zeros((n, rest), F32)], axis=1)
    return c, su, sd


def _proj_weights(w_in):
    sizes = [A_WIDTH, A_WIDTH, B_WIDTH] + [KV_WIDTH] * 6 + [3 * N_HEADS, D_MODEL, D_MODEL]
    offs = np.concatenate([[0], np.cumsum(sizes)])
    w = w_in.astype(BF16)
    w_gate = w[:, offs[10]:]
    gates = w[:, offs[9]:offs[10]]
    blocks = []
    for g in range(N_KV_GROUPS):
        cols = [gates[:, br * N_HEADS + g * HEADS_PER_GROUP:br * N_HEADS + (g + 1) * HEADS_PER_GROUP]
                for br in range(3)]
        pad = jnp.zeros((w.shape[0], LANES - 3 * HEADS_PER_GROUP), w.dtype)
        blocks.extend(cols + [pad])
    return w, w_gate, jnp.concatenate(blocks, axis=1)


def _overlap_t(nb_pad, n_sel):
    cmp_start = np.arange(nb_pad) * CMP_STRIDE
    sel_start = np.arange(n_sel) * SEL_BLOCK
    ov = ((cmp_start[None, :] < sel_start[:, None] + SEL_BLOCK)
          & (cmp_start[None, :] + CMP_BLOCK > sel_start[:, None]))
    return jnp.asarray(ov, dtype=BF16)


def _layer(x, ffn1_norm, ffn1_w_gate, ffn1_w_up, ffn1_w_down, mix_norm, w_in,
           a_v_norm, a_w_s, a_b_s, q_norm, k_cmp_norm, k_slc_norm, k_win_norm,
           cmp_k_pe, cmp_k_w1, cmp_k_w2, cmp_v_pe, cmp_v_w1, cmp_v_w2,
           w_branch_a, w_branch_b, w_out, ffn2_norm, ffn2_w_gate, ffn2_w_up, ffn2_w_down):
    B, S, D = x.shape
    T = B * S
    G = N_KV_GROUPS
    row = lambda v: v.reshape(1, -1)
    x0 = x.reshape(T, D)

    x1 = _ffn(x0, row(ffn1_norm), ffn1_w_gate, ffn1_w_up, ffn1_w_down)

    uv, gab, proj_b, head_gates = _proj(x1, row(mix_norm), *_proj_weights(w_in))

    o_a = _gmlp(uv, row(a_v_norm), a_w_s, a_b_s.T)

    pos = jnp.arange(S)
    qn, ksa, vs, kwn, vw = _prep(proj_b, row(q_norm), row(k_slc_norm), row(k_win_norm),
                                 *_rope_tables(pos), seq=S)

    nb_pad = S // CMP_STRIDE
    cmp_end = jnp.arange(nb_pad) * CMP_STRIDE + (CMP_BLOCK - 1)
    kc, vc = _compress(proj_b, cmp_k_pe.reshape(1, -1), cmp_v_pe.reshape(1, -1),
                       cmp_k_w1.astype(BF16), cmp_k_w2.astype(BF16),
                       cmp_v_w1.astype(BF16), cmp_v_w2.astype(BF16),
                       row(k_cmp_norm), *_rope_tables(cmp_end), batch=B, seq=S)

    o_cw, bias = _cmp_attn(qn, kc, vc, kwn, vw, head_gates, _overlap_t(nb_pad, S // SEL_BLOCK),
                           batch=B, seq=S)
    o_b = _sw_attn(qn, bias, ksa, vs, o_cw, head_gates, batch=B, seq=S)

    merged = _merge(o_a, o_b, gab, w_branch_a.astype(BF16), w_branch_b.astype(BF16))
    x2 = _out_proj(x1, merged, w_out.astype(BF16))

    x3 = _ffn(x2, row(ffn2_norm), ffn2_w_gate, ffn2_w_up, ffn2_w_down)
    return x3.reshape(B, S, D)


def kernel(x, ffn1_norm, ffn1_w_gate, ffn1_w_up, ffn1_w_down, mix_norm, w_in, a_v_norm, a_w_s, a_b_s, q_norm, k_cmp_norm, k_slc_norm, k_win_norm, cmp_k_pe, cmp_k_w1, cmp_k_w2, cmp_v_pe, cmp_v_w1, cmp_v_w2, w_branch_a, w_branch_b, w_out, ffn2_norm, ffn2_w_gate, ffn2_w_up, ffn2_w_down):
    params = (ffn1_norm, ffn1_w_gate, ffn1_w_up, ffn1_w_down, mix_norm, w_in,
              a_v_norm, a_w_s, a_b_s, q_norm, k_cmp_norm, k_slc_norm, k_win_norm,
              cmp_k_pe, cmp_k_w1, cmp_k_w2, cmp_v_pe, cmp_v_w1, cmp_v_w2,
              w_branch_a, w_branch_b, w_out, ffn2_norm, ffn2_w_gate, ffn2_w_up, ffn2_w_down)
    for l in range(params[0].shape[0]):
        x = _layer(x, *[p[l] for p in params])
    return x
```

```python
import functools

import numpy as np
import jax
import jax.numpy as jnp
from jax import lax
from jax.experimental import pallas as pl
from jax.experimental.pallas import tpu as pltpu

D_MODEL = 2048
D_FF = 5504
A_WIDTH = 2048
A_GROUPS = 8
A_CHUNK = 128
N_HEADS = 16
HEAD_DIM = 128
N_KV_GROUPS = 2
HEADS_PER_GROUP = N_HEADS // N_KV_GROUPS
CMP_BLOCK = 32
CMP_STRIDE = 16
CMP_HIDDEN = 256
SEL_BLOCK = 64
SEL_TOP = 16
WINDOW = 512
ROPE_THETA = 500000.0
ROPE_DIM = HEAD_DIM // 4
EPS = 1e-6
NEG = -1e30
FORCE_BONUS = 1e4
LOG2_E = 1.4426950408889634
B_WIDTH = N_HEADS * HEAD_DIM
KV_WIDTH = N_KV_GROUPS * HEAD_DIM

LANES = 128
HEAD_GATE_BLK = (B_WIDTH + 6 * KV_WIDTH) // LANES
VMEM_LIMIT = 56 * 1024 * 1024

F32 = jnp.float32
BF16 = jnp.bfloat16
NT_DIMS = (((1,), (1,)), ((), ()))


def _params(*sem):
    return pltpu.CompilerParams(dimension_semantics=sem, vmem_limit_bytes=VMEM_LIMIT)


def _rms(x, g):
    ms = jnp.mean(x * x, axis=-1, keepdims=True)
    return x * lax.rsqrt(ms + EPS) * g


def _ffn_kernel(x_ref, g_ref, wg_ref, wu_ref, wd_ref, o_ref, h_ref, *, nf, tf, f_last):
    j = pl.program_id(1)

    def partial(width):
        h = h_ref[...]
        a = jnp.dot(h, wg_ref[:, :width].astype(BF16), preferred_element_type=F32)
        b = jnp.dot(h, wu_ref[:, :width].astype(BF16), preferred_element_type=F32)
        act = (a * jax.nn.sigmoid(a) * b).astype(BF16)
        return jnp.dot(act, wd_ref[:width, :].astype(BF16), preferred_element_type=F32)

    @pl.when(j == 0)
    def _():
        h_ref[...] = _rms(x_ref[...], g_ref[...]).astype(BF16)
        o_ref[...] = partial(f_last)

    @pl.when((j > 0) & (j < nf - 1))
    def _():
        o_ref[...] += partial(tf)

    @pl.when(j == nf - 1)
    def _():
        o_ref[...] = x_ref[...] + 0.5 * (o_ref[...] + partial(tf))


def _ffn(x, norm, wg, wu, wd, *, tm=1024, tf=256):
    T, D = x.shape
    F = wg.shape[1]
    nf = pl.cdiv(F, tf)
    f_last = F - (nf - 1) * tf
    hid = lambda j: (j + nf - 1) % nf
    return pl.pallas_call(
        functools.partial(_ffn_kernel, nf=nf, tf=tf, f_last=f_last),
        grid=(T // tm, nf),
        in_specs=[
            pl.BlockSpec((tm, D), lambda i, j: (i, 0)),
            pl.BlockSpec((1, D), lambda i, j: (0, 0)),
            pl.BlockSpec((D, tf), lambda i, j: (0, hid(j))),
            pl.BlockSpec((D, tf), lambda i, j: (0, hid(j))),
            pl.BlockSpec((tf, D), lambda i, j: (hid(j), 0)),
        ],
        out_specs=pl.BlockSpec((tm, D), lambda i, j: (i, 0)),
        out_shape=jax.ShapeDtypeStruct((T, D), F32),
        scratch_shapes=[pltpu.VMEM((tm, D), BF16)],
        compiler_params=_params("parallel", "arbitrary"),
        name="ffn",
    )(x, norm, wg, wu, wd)


def _proj_kernel(x_ref, g_ref, w_ref, o_ref, h_ref):
    @pl.when(pl.program_id(1) == 0)
    def _():
        h_ref[...] = _rms(x_ref[...], g_ref[...]).astype(BF16)

    o_ref[...] = jnp.dot(h_ref[...], w_ref[...], preferred_element_type=F32).astype(o_ref.dtype)


def _proj(x, norm, w, out_dtype, *, tm, tn):
    T, D = x.shape
    N = w.shape[1]
    return pl.pallas_call(
        _proj_kernel,
        grid=(T // tm, N // tn),
        in_specs=[
            pl.BlockSpec((tm, D), lambda i, j: (i, 0)),
            pl.BlockSpec((1, D), lambda i, j: (0, 0)),
            pl.BlockSpec((D, tn), lambda i, j: (0, j)),
        ],
        out_specs=pl.BlockSpec((tm, tn), lambda i, j: (i, j)),
        out_shape=jax.ShapeDtypeStruct((T, N), out_dtype),
        scratch_shapes=[pltpu.VMEM((tm, D), BF16)],
        compiler_params=_params("parallel", "arbitrary"),
        name="proj",
    )(x, norm, w)


def _gmlp_kernel(u_ref, v_ref, vn_ref, ws_ref, bs_ref, o_ref, *, tr):
    gw = A_WIDTH // A_GROUPS
    v = jax.nn.gelu(v_ref[...].astype(F32))
    vn = _rms(v, vn_ref[...]).astype(BF16)
    row = lax.broadcasted_iota(jnp.int32, (A_CHUNK, A_CHUNK), 0)
    col = lax.broadcasted_iota(jnp.int32, (A_CHUNK, A_CHUNK), 1)
    for g in range(A_GROUPS):
        w = jnp.where(col <= row, ws_ref[g], 0.0).astype(BF16)
        bias = bs_ref[:, g:g + 1]
        for c in range(tr // A_CHUNK):
            rows = slice(c * A_CHUNK, (c + 1) * A_CHUNK)
            cols = slice(g * gw, (g + 1) * gw)
            mixed = jnp.dot(w, vn[rows, cols], preferred_element_type=F32) + bias
            u = jax.nn.gelu(u_ref[rows, cols].astype(F32))
            o_ref[rows, cols] = (u * mixed).astype(o_ref.dtype)


def _gmlp(proj_a, a_v_norm, w_s, b_s_t, *, tr=512):
    T = proj_a.shape[0]
    return pl.pallas_call(
        functools.partial(_gmlp_kernel, tr=tr),
        grid=(T // tr,),
        in_specs=[
            pl.BlockSpec((tr, A_WIDTH), lambda i: (i, 0)),
            pl.BlockSpec((tr, A_WIDTH), lambda i: (i, 1)),
            pl.BlockSpec((1, A_WIDTH), lambda i: (0, 0)),
            pl.BlockSpec((A_GROUPS, A_CHUNK, A_CHUNK), lambda i: (0, 0, 0)),
            pl.BlockSpec((A_CHUNK, A_GROUPS), lambda i: (0, 0)),
        ],
        out_specs=pl.BlockSpec((tr, A_WIDTH), lambda i: (i, 0)),
        out_shape=jax.ShapeDtypeStruct((T, A_WIDTH), BF16),
        compiler_params=_params("parallel"),
        name="gmlp",
    )(proj_a, proj_a, a_v_norm, w_s, b_s_t)


def _rope(y, c, su, sd):
    return y * c + pltpu.roll(y, 16, 1) * su + pltpu.roll(y, HEAD_DIM - 16, 1) * sd


def _norm_rope_mxu(x, g, c, s, w):
    xg = x * g
    lhs = jnp.concatenate([(x * x).astype(BF16), xg.astype(BF16)], axis=1)
    res = jnp.dot(lhs, w, preferred_element_type=F32)
    r = lax.rsqrt(res[:, :HEAD_DIM] * (1.0 / HEAD_DIM) + EPS)
    return (xg * c + res[:, HEAD_DIM:] * s) * r


def _prep_kernel(q_ref, ks_ref, vs_ref, kw_ref, vw_ref, qg_ref, ksg_ref, kwg_ref,
                 cq_ref, sq_ref, ck_ref, sk_ref, w_ref,
                 qn_ref, ksa_ref, vso_ref, kwn_ref, vwo_ref, *, tr, seq):
    cq, sq, ck, sk, w = cq_ref[...], sq_ref[...], ck_ref[...], sk_ref[...], w_ref[...]
    for h in range(N_HEADS):
        cols = slice(h * HEAD_DIM, (h + 1) * HEAD_DIM)
        qn_ref[:, cols] = _norm_rope_mxu(q_ref[:, cols], qg_ref[...], cq, sq, w).astype(BF16)
    t0 = (pl.program_id(0) * tr) % seq
    t = t0 + lax.broadcasted_iota(jnp.int32, (tr, LANES), 0)
    lane = lax.broadcasted_iota(jnp.int32, (tr, LANES), 1)
    onehot = jnp.where((t // SEL_BLOCK) == lane, 1.0, 0.0).astype(BF16)
    ones = jnp.ones((tr, HEAD_DIM), BF16)
    for g in range(N_KV_GROUPS):
        cols = slice(g * HEAD_DIM, (g + 1) * HEAD_DIM)
        lo = slice(2 * g * HEAD_DIM, (2 * g + 1) * HEAD_DIM)
        hi = slice((2 * g + 1) * HEAD_DIM, (2 * g + 2) * HEAD_DIM)
        ksa_ref[:, lo] = _norm_rope_mxu(ks_ref[:, cols], ksg_ref[...], ck, sk, w).astype(BF16)
        ksa_ref[:, hi] = onehot
        kwn_ref[:, cols] = _norm_rope_mxu(kw_ref[:, cols], kwg_ref[...], ck, sk, w).astype(BF16)
        vso_ref[:, lo] = vs_ref[:, cols].astype(BF16)
        vso_ref[:, hi] = ones
        vwo_ref[:, lo] = vw_ref[:, cols].astype(BF16)
        vwo_ref[:, hi] = ones


def _rope_matrix():
    half = ROPE_DIM // 2
    w = np.zeros((2 * HEAD_DIM, 2 * HEAD_DIM), np.float32)
    w[:HEAD_DIM, :HEAD_DIM] = 1.0
    for j in range(half):
        w[HEAD_DIM + j + half, HEAD_DIM + j] = 1.0
        w[HEAD_DIM + j, HEAD_DIM + j + half] = 1.0
    return jnp.asarray(w, dtype=BF16)


def _prep(proj_b, q_norm, k_slc_norm, k_win_norm, rope_c, rope_su, rope_sd, *, seq, tr=512):
    T = proj_b.shape[0]
    kvb = B_WIDTH // KV_WIDTH
    nrt = seq // tr
    q_scale = HEAD_DIM ** -0.5 * LOG2_E
    rope_s = rope_su + rope_sd
    w = _rope_matrix()
    row = lambda i: (i, 0)
    tab = pl.BlockSpec((tr, HEAD_DIM), lambda i: (i % nrt, 0))
    gain = pl.BlockSpec((1, HEAD_DIM), lambda i: (0, 0))
    kv_in = lambda k: pl.BlockSpec((tr, KV_WIDTH), lambda i: (i, kvb + k))
    return pl.pallas_call(
        functools.partial(_prep_kernel, tr=tr, seq=seq),
        grid=(T // tr,),
        in_specs=[pl.BlockSpec((tr, B_WIDTH), row), kv_in(2), kv_in(3), kv_in(4), kv_in(5),
                  gain, gain, gain, tab, tab, tab, tab, pl.BlockSpec(w.shape, lambda i: (0, 0))],
        out_specs=[pl.BlockSpec((tr, B_WIDTH), row), pl.BlockSpec((tr, 2 * KV_WIDTH), row),
                   pl.BlockSpec((tr, 2 * KV_WIDTH), row), pl.BlockSpec((tr, KV_WIDTH), row),
                   pl.BlockSpec((tr, 2 * KV_WIDTH), row)],
        out_shape=[jax.ShapeDtypeStruct((T, B_WIDTH), BF16),
                   jax.ShapeDtypeStruct((T, 2 * KV_WIDTH), BF16),
                   jax.ShapeDtypeStruct((T, 2 * KV_WIDTH), BF16),
                   jax.ShapeDtypeStruct((T, KV_WIDTH), BF16),
                   jax.ShapeDtypeStruct((T, 2 * KV_WIDTH), BF16)],
        compiler_params=_params("parallel"),
        name="qk_prep",
    )(proj_b, proj_b, proj_b, proj_b, proj_b, q_norm, k_slc_norm, k_win_norm,
      rope_c * q_scale, rope_s * q_scale, rope_c, rope_s, w)


def _compress_kernel(zk_ref, zv_ref, pek_ref, pev_ref, w1k_ref, w2k_ref, w1v_ref, w2v_ref,
                     kg_ref, c_ref, su_ref, sd_ref, kc_ref, vc_ref, zr_ref):
    half = CMP_STRIDE * HEAD_DIM
    nb = zr_ref.shape[0]

    def comp(z_ref, pe_ref, w1_ref, w2_ref):
        for l in range(CMP_STRIDE):
            zr_ref[:, l * HEAD_DIM:(l + 1) * HEAD_DIM] = z_ref[pl.ds(l, nb, stride=CMP_STRIDE), :]
        z = zr_ref[...]
        top = jnp.dot((z + pe_ref[:, :half]).astype(BF16), w1_ref[:half, :], preferred_element_type=F32)
        bot = jnp.dot((z + pe_ref[:, half:]).astype(BF16), w1_ref[half:, :], preferred_element_type=F32)
        hid = top + pltpu.roll(bot, nb - 1, 0)
        return jnp.dot(jax.nn.gelu(hid).astype(BF16), w2_ref[...], preferred_element_type=F32)

    k = comp(zk_ref, pek_ref, w1k_ref, w2k_ref)
    kc_ref[...] = _rope(_rms(k, kg_ref[...]), c_ref[...], su_ref[...], sd_ref[...]).astype(BF16)
    vc_ref[...] = comp(zv_ref, pev_ref, w1v_ref, w2v_ref).astype(BF16)


def _compress(proj_b, pek, pev, w1k, w2k, w1v, w2v, k_cmp_norm, cmp_c, cmp_su, cmp_sd, *, batch, seq):
    G = N_KV_GROUPS
    nb = seq // CMP_STRIDE
    kc_blk = B_WIDTH // HEAD_DIM
    vc_blk = kc_blk + G
    ospec = pl.BlockSpec((None, None, nb, HEAD_DIM), lambda b, g: (b, g, 0, 0))
    full = lambda a: pl.BlockSpec(a.shape, lambda b, g: (0,) * a.ndim)
    consts = (pek, pev, w1k, w2k, w1v, w2v, k_cmp_norm, cmp_c, cmp_su, cmp_sd)
    return pl.pallas_call(
        _compress_kernel,
        grid=(batch, G),
        in_specs=[pl.BlockSpec((seq, HEAD_DIM), lambda b, g: (b, kc_blk + g)),
                  pl.BlockSpec((seq, HEAD_DIM), lambda b, g: (b, vc_blk + g))]
                 + [full(a) for a in consts],
        out_specs=[ospec, ospec],
        out_shape=[jax.ShapeDtypeStruct((batch, G, nb, HEAD_DIM), BF16)] * 2,
        scratch_shapes=[pltpu.VMEM((nb, CMP_STRIDE * HEAD_DIM), F32)],
        compiler_params=_params("parallel", "parallel"),
        name="compress",
    )(proj_b, proj_b, *consts)


def _cmp_attn_kernel(q_ref, kc_ref, vc_ref, kw_ref, vw_ref, gate_ref, ovt_ref, oc_ref, bias_ref,
                     qs_ref, *, tq):
    hpg = HEADS_PER_GROUP
    M = hpg * tq
    nb = kc_ref.shape[0]
    n_sel = ovt_ref.shape[0]
    t0 = pl.program_id(2) * tq
    for h in range(hpg):
        qs_ref[h * tq:(h + 1) * tq, :] = q_ref[:, h * HEAD_DIM:(h + 1) * HEAD_DIM]

    wlen = WINDOW + tq
    kstart = pl.multiple_of(jnp.maximum(t0 - WINDOW, 0), tq)
    sw = lax.dot_general(qs_ref[...], kw_ref[pl.ds(kstart, wlen), :], NT_DIMS,
                         preferred_element_type=F32)
    tw = t0 + lax.broadcasted_iota(jnp.int32, (1, tq, wlen), 1)
    kp = kstart + lax.broadcasted_iota(jnp.int32, (1, tq, wlen), 2)
    keep = (tw - kp).astype(jnp.uint32) < WINDOW
    sw = jnp.where(keep, sw.reshape(hpg, tq, wlen), NEG).reshape(M, wlen)
    pw = jnp.exp2(sw - jnp.max(sw, axis=-1, keepdims=True))
    aw = jnp.dot(pw.astype(BF16), vw_ref[pl.ds(kstart, wlen), :], preferred_element_type=F32)
    o_w = aw[:, :HEAD_DIM] * (1.0 / aw[:, HEAD_DIM:])

    s = lax.dot_general(qs_ref[...], kc_ref[...], NT_DIMS, preferred_element_type=F32)
    s = s.reshape(hpg, tq, nb)
    t = t0 + lax.broadcasted_iota(jnp.int32, (1, tq, nb), 1)
    n = lax.broadcasted_iota(jnp.int32, (1, tq, nb), 2)
    mask = n * CMP_STRIDE + (CMP_BLOCK - 1) <= t
    s = jnp.where(mask, s, NEG)
    e = jnp.exp2(s - jnp.max(s, axis=-1, keepdims=True))
    inv = 1.0 / jnp.sum(e, axis=-1, keepdims=True)
    p = jnp.where(mask, e * inv, 0.0)
    o = jnp.dot(p.reshape(hpg * tq, nb).astype(BF16), vc_ref[...], preferred_element_type=F32)
    sig = jax.nn.sigmoid(gate_ref[...])
    for h in range(hpg):
        rows = slice(h * tq, (h + 1) * tq)
        oc_ref[:, h * HEAD_DIM:(h + 1) * HEAD_DIM] = (
            o[rows] * sig[:, h:h + 1]
            + o_w[rows] * sig[:, 2 * hpg + h:2 * hpg + h + 1]).astype(oc_ref.dtype)

    psum = jnp.sum(p, axis=0)
    ovt = ovt_ref[...]
    hi = psum.astype(BF16)
    r1 = psum - hi.astype(F32)
    mid = r1.astype(BF16)
    lo = (r1 - mid.astype(F32)).astype(BF16)
    imp = (lax.dot_general(ovt, hi, NT_DIMS, preferred_element_type=F32)
           + lax.dot_general(ovt, mid, NT_DIMS, preferred_element_type=F32)
           + lax.dot_general(ovt, lo, NT_DIMS, preferred_element_type=F32))
    j = lax.broadcasted_iota(jnp.int32, (n_sel, tq), 0)
    tt = t0 + lax.broadcasted_iota(jnp.int32, (n_sel, tq), 1)
    cur = tt // SEL_BLOCK
    forced = jnp.where((j == 0) | (j == cur) | (j == cur - 1), FORCE_BONUS, 0.0)
    score = jnp.where(j * SEL_BLOCK <= tt, imp + forced, NEG)
    jrow = j.astype(F32)
    below_neg = 3.0 * NEG
    chosen = jnp.zeros((n_sel, tq), F32)
    for _ in range(SEL_TOP):
        top = jnp.max(score, axis=0, keepdims=True)
        idx = jnp.min(jnp.where(score == top, jrow, float(n_sel)), axis=0, keepdims=True)
        hit = jrow == idx
        chosen = jnp.where(hit, 1.0, chosen)
        score = jnp.where(hit, below_neg, score)
    bias = jnp.where(chosen > 0.5, 0.0, NEG)
    bias = jnp.concatenate([bias, jnp.zeros((LANES - n_sel, tq), F32)], axis=0)
    bias_ref[...] = bias.T.astype(bias_ref.dtype)


def _cmp_attn(qn, kc, vc, kwn, vw, head_gates, ovt, *, batch, seq, tq=256):
    assert WINDOW % tq == 0
    T = qn.shape[0]
    G = N_KV_GROUPS
    nq = seq // tq
    gw = HEADS_PER_GROUP * HEAD_DIM
    nb = kc.shape[2]
    rowg = lambda b, g, i: (b * nq + i, g)
    seqg = lambda b, g, i: (b, g)
    return pl.pallas_call(
        functools.partial(_cmp_attn_kernel, tq=tq),
        grid=(batch, G, nq),
        in_specs=[
            pl.BlockSpec((tq, gw), rowg),
            pl.BlockSpec((None, None, nb, HEAD_DIM), lambda b, g, i: (b, g, 0, 0)),
            pl.BlockSpec((None, None, nb, HEAD_DIM), lambda b, g, i: (b, g, 0, 0)),
            pl.BlockSpec((seq, HEAD_DIM), seqg),
            pl.BlockSpec((seq, 2 * HEAD_DIM), seqg),
            pl.BlockSpec((tq, LANES), lambda b, g, i: (b * nq + i, HEAD_GATE_BLK + g)),
            pl.BlockSpec(ovt.shape, lambda b, g, i: (0, 0)),
        ],
        out_specs=[pl.BlockSpec((tq, gw), rowg), pl.BlockSpec((tq, LANES), rowg)],
        out_shape=[jax.ShapeDtypeStruct((T, B_WIDTH), BF16),
                   jax.ShapeDtypeStruct((T, G * LANES), BF16)],
        scratch_shapes=[pltpu.VMEM((HEADS_PER_GROUP * tq, HEAD_DIM), BF16)],
        compiler_params=_params("parallel", "parallel", "parallel"),
        name="cmp_win_attn",
    )(qn, kc, vc, kwn, vw, head_gates, ovt)


def _sw_attn_kernel(q_ref, bias_ref, ksa_ref, vs_ref, oc_ref, gate_ref, o_ref,
                    qs_ref, m_ref, acc_ref, ss0_ref, ss1_ref, *, tq, tks):
    hpg = HEADS_PER_GROUP
    M = hpg * tq
    ss_ref = (ss0_ref, ss1_ref)
    t0 = pl.program_id(2) * tq
    bias = bias_ref[...]
    for h in range(hpg):
        qs_ref[h * tq:(h + 1) * tq, :HEAD_DIM] = q_ref[:, h * HEAD_DIM:(h + 1) * HEAD_DIM]
        qs_ref[h * tq:(h + 1) * tq, HEAD_DIM:] = bias

    def reset():
        m_ref[...] = jnp.full_like(m_ref, NEG)
        acc_ref[...] = jnp.zeros_like(acc_ref)

    def step(s, v):
        m_old = m_ref[...]
        m_new = jnp.maximum(m_old, jnp.max(s, axis=-1, keepdims=True))
        alpha = jnp.exp2(m_old - m_new)
        p = jnp.exp2(s - jnp.tile(m_new, (1, s.shape[1] // LANES)))
        acc_ref[...] = (jnp.tile(alpha, (1, 2)) * acc_ref[...]
                        + jnp.dot(p.astype(BF16), v, preferred_element_type=F32))
        m_ref[...] = m_new

    def result():
        return acc_ref[:, :HEAD_DIM] * (1.0 / acc_ref[:, HEAD_DIM:])

    def masked(s, keep):
        tk = s.shape[1]
        return jnp.where(keep, s.reshape(hpg, tq, tk), NEG).reshape(M, tk)

    def positions(tk):
        t = t0 + lax.broadcasted_iota(jnp.int32, (1, tq, tk), 1)
        kpos = lax.broadcasted_iota(jnp.int32, (1, tq, tk), 2)
        return t, kpos

    def sel_base(j):
        return pl.multiple_of(j * tks, tks)

    def sel_scores(j):
        base = sel_base(j)
        k = ksa_ref[pl.ds(base, tks), :]
        s = lax.dot_general(qs_ref[...], k, NT_DIMS, preferred_element_type=F32)
        t, kpos = positions(tks)
        return masked(s, kpos + base <= t)

    sel_last = (t0 + tq - 1) // tks

    def by_parity(n, fn):
        @pl.when(n % 2 == 0)
        def _():
            fn(0, 1)

        @pl.when(n % 2 == 1)
        def _():
            fn(1, 0)

    reset()
    ss_ref[0][...] = sel_scores(0)

    def sel_body(j, carry):
        def run(cur, nxt):
            ss_ref[nxt][...] = sel_scores(j + 1)
            step(ss_ref[cur][...], vs_ref[pl.ds(sel_base(j), tks), :])

        by_parity(j, run)
        return carry

    lax.fori_loop(0, sel_last, sel_body, 0)
    by_parity(sel_last,
              lambda cur, nxt: step(ss_ref[cur][...], vs_ref[pl.ds(sel_base(sel_last), tks), :]))
    o_s = result()

    sig = jax.nn.sigmoid(gate_ref[...])
    for h in range(hpg):
        rows = slice(h * tq, (h + 1) * tq)
        cols = slice(h * HEAD_DIM, (h + 1) * HEAD_DIM)
        o = oc_ref[:, cols].astype(F32) + sig[:, hpg + h:hpg + h + 1] * o_s[rows]
        o_ref[:, cols] = o.astype(o_ref.dtype)


def _sw_attn(qn, bias, ksa, vs, o_c, head_gates, *, batch, seq, tq=256, tks=512):
    T = qn.shape[0]
    G = N_KV_GROUPS
    nq = seq // tq
    gw = HEADS_PER_GROUP * HEAD_DIM
    M = HEADS_PER_GROUP * tq
    rowg = lambda b, g, i: (b * nq + i, g)
    seqg = lambda b, g, i: (b, g)
    return pl.pallas_call(
        functools.partial(_sw_attn_kernel, tq=tq, tks=tks),
        grid=(batch, G, nq),
        in_specs=[
            pl.BlockSpec((tq, gw), rowg),
            pl.BlockSpec((tq, LANES), rowg),
            pl.BlockSpec((seq, 2 * HEAD_DIM), seqg),
            pl.BlockSpec((seq, 2 * HEAD_DIM), seqg),
            pl.BlockSpec((tq, gw), rowg),
            pl.BlockSpec((tq, LANES), lambda b, g, i: (b * nq + i, HEAD_GATE_BLK + g)),
        ],
        out_specs=pl.BlockSpec((tq, gw), rowg),
        out_shape=jax.ShapeDtypeStruct((T, B_WIDTH), BF16),
        scratch_shapes=[pltpu.VMEM((M, 2 * HEAD_DIM), BF16), pltpu.VMEM((M, LANES), F32),
                        pltpu.VMEM((M, 2 * HEAD_DIM), F32),
                        pltpu.VMEM((M, tks), F32), pltpu.VMEM((M, tks), F32)],
        compiler_params=_params("parallel", "parallel", "parallel"),
        name="sel_attn",
    )(qn, bias, ksa, vs, o_c, head_gates)


def _merge_kernel(oa_ref, ob_ref, ga_ref, gb_ref, wa_ref, wb_ref, o_ref):
    a = jnp.dot(oa_ref[...], wa_ref[...], preferred_element_type=F32)
    b = jnp.dot(ob_ref[...], wb_ref[...], preferred_element_type=F32)
    ga = jax.nn.sigmoid(ga_ref[...].astype(F32))
    gb = jax.nn.sigmoid(gb_ref[...].astype(F32))
    o_ref[...] = (ga * a + gb * b).astype(o_ref.dtype)


def _merge(o_a, o_b, gab, wa, wb, *, tm=1024, tn=512):
    T, D = o_a.shape
    N = wa.shape[1]
    nn = N // tn
    ga_blk = 2 * A_WIDTH // tn
    return pl.pallas_call(
        _merge_kernel,
        grid=(T // tm, nn),
        in_specs=[
            pl.BlockSpec((tm, D), lambda i, j: (i, 0)),
            pl.BlockSpec((tm, D), lambda i, j: (i, 0)),
            pl.BlockSpec((tm, tn), lambda i, j: (i, ga_blk + j)),
            pl.BlockSpec((tm, tn), lambda i, j: (i, ga_blk + nn + j)),
            pl.BlockSpec((D, tn), lambda i, j: (0, j)),
            pl.BlockSpec((D, tn), lambda i, j: (0, j)),
        ],
        out_specs=pl.BlockSpec((tm, tn), lambda i, j: (i, j)),
        out_shape=jax.ShapeDtypeStruct((T, N), BF16),
        compiler_params=_params("parallel", "arbitrary"),
        name="merge",
    )(o_a, o_b, gab, gab, wa, wb)


def _out_kernel(x_ref, m_ref, w_ref, o_ref):
    o_ref[...] = x_ref[...] + jnp.dot(m_ref[...], w_ref[...], preferred_element_type=F32)


def _out_proj(x, merged, w, *, tm=1024, tn=512):
    T, D = merged.shape
    N = w.shape[1]
    return pl.pallas_call(
        _out_kernel,
        grid=(T // tm, N // tn),
        in_specs=[
            pl.BlockSpec((tm, tn), lambda i, j: (i, j)),
            pl.BlockSpec((tm, D), lambda i, j: (i, 0)),
            pl.BlockSpec((D, tn), lambda i, j: (0, j)),
        ],
        out_specs=pl.BlockSpec((tm, tn), lambda i, j: (i, j)),
        out_shape=jax.ShapeDtypeStruct((T, N), F32),
        compiler_params=_params("parallel", "arbitrary"),
        name="out_proj",
    )(x, merged, w)


def _rope_tables(pos):
    half = ROPE_DIM // 2
    inv = ROPE_THETA ** (-2.0 * jnp.arange(half, dtype=F32) / ROPE_DIM)
    ang = pos.astype(F32)[:, None] * inv
    cos, sin = jnp.cos(ang), jnp.sin(ang)
    n = pos.shape[0]
    z16 = jnp.zeros((n, half), F32)
    rest = HEAD_DIM - ROPE_DIM
    c = jnp.concatenate([cos, cos, jnp.ones((n, rest), F32)], axis=1)
    su = jnp.concatenate([z16, sin, jnp.zeros((n, rest), F32)], axis=1)
    sd = jnp.concatenate([-sin, z16, jnp.zeros((n, rest), F32)], axis=1)
    return c, su, sd


def _split_w_in(w_in):
    sizes = [A_WIDTH, A_WIDTH, B_WIDTH] + [KV_WIDTH] * 6 + [3 * N_HEADS, D_MODEL, D_MODEL]
    offs = np.concatenate([[0], np.cumsum(sizes)])
    w = w_in.astype(BF16)
    w_a = jnp.concatenate([w[:, :offs[2]], w[:, offs[10]:]], axis=1)
    gates = w[:, offs[9]:offs[10]]
    blocks = []
    for g in range(N_KV_GROUPS):
        cols = [gates[:, br * N_HEADS + g * HEADS_PER_GROUP:br * N_HEADS + (g + 1) * HEADS_PER_GROUP]
                for br in range(3)]
        pad = jnp.zeros((w.shape[0], LANES - 3 * HEADS_PER_GROUP), w.dtype)
        blocks.extend(cols + [pad])
    w_b = jnp.concatenate([w[:, offs[2]:offs[9]]] + blocks, axis=1)
    return w_a, w_b


def _overlap_t(nb_pad, n_sel):
    cmp_start = np.arange(nb_pad) * CMP_STRIDE
    sel_start = np.arange(n_sel) * SEL_BLOCK
    ov = ((cmp_start[None, :] < sel_start[:, None] + SEL_BLOCK)
          & (cmp_start[None, :] + CMP_BLOCK > sel_start[:, None]))
    return jnp.asarray(ov, dtype=BF16)


def _layer(x, ffn1_norm, ffn1_w_gate, ffn1_w_up, ffn1_w_down, mix_norm, w_in,
           a_v_norm, a_w_s, a_b_s, q_norm, k_cmp_norm, k_slc_norm, k_win_norm,
           cmp_k_pe, cmp_k_w1, cmp_k_w2, cmp_v_pe, cmp_v_w1, cmp_v_w2,
           w_branch_a, w_branch_b, w_out, ffn2_norm, ffn2_w_gate, ffn2_w_up, ffn2_w_down):
    B, S, D = x.shape
    T = B * S
    G = N_KV_GROUPS
    row = lambda v: v.reshape(1, -1)
    x0 = x.reshape(T, D)

    x1 = _ffn(x0, row(ffn1_norm), ffn1_w_gate, ffn1_w_up, ffn1_w_down)

    w_a, w_b = _split_w_in(w_in)
    proj_a = _proj(x1, row(mix_norm), w_a, BF16, tm=1024, tn=2048)
    proj_b = _proj(x1, row(mix_norm), w_b, F32, tm=1024, tn=1280)

    o_a = _gmlp(proj_a, row(a_v_norm), a_w_s, a_b_s.T)

    pos = jnp.arange(S)
    qn, ksa, vs, kwn, vw = _prep(proj_b, row(q_norm), row(k_slc_norm), row(k_win_norm),
                                 *_rope_tables(pos), seq=S)

    nb_pad = S // CMP_STRIDE
    cmp_end = jnp.arange(nb_pad) * CMP_STRIDE + (CMP_BLOCK - 1)
    kc, vc = _compress(proj_b, cmp_k_pe.reshape(1, -1), cmp_v_pe.reshape(1, -1),
                       cmp_k_w1.astype(BF16), cmp_k_w2.astype(BF16),
                       cmp_v_w1.astype(BF16), cmp_v_w2.astype(BF16),
                       row(k_cmp_norm), *_rope_tables(cmp_end), batch=B, seq=S)

    o_cw, bias = _cmp_attn(qn, kc, vc, kwn, vw, proj_b, _overlap_t(nb_pad, S // SEL_BLOCK),
                           batch=B, seq=S)
    o_b = _sw_attn(qn, bias, ksa, vs, o_cw, proj_b, batch=B, seq=S)

    merged = _merge(o_a, o_b, proj_a, w_branch_a.astype(BF16), w_branch_b.astype(BF16))
    x2 = _out_proj(x1, merged, w_out.astype(BF16))

    x3 = _ffn(x2, row(ffn2_norm), ffn2_w_gate, ffn2_w_up, ffn2_w_down)
    return x3.reshape(B, S, D)


def kernel(x, ffn1_norm, ffn1_w_gate, ffn1_w_up, ffn1_w_down, mix_norm, w_in, a_v_norm, a_w_s, a_b_s, q_norm, k_cmp_norm, k_slc_norm, k_win_norm, cmp_k_pe, cmp_k_w1, cmp_k_w2, cmp_v_pe, cmp_v_w1, cmp_v_w2, w_branch_a, w_branch_b, w_out, ffn2_norm, ffn2_w_gate, ffn2_w_up, ffn2_w_down):
    params = (ffn1_norm, ffn1_w_gate, ffn1_w_up, ffn1_w_down, mix_norm, w_in,
              a_v_norm, a_w_s, a_b_s, q_norm, k_cmp_norm, k_slc_norm, k_win_norm,
              cmp_k_pe, cmp_k_w1, cmp_k_w2, cmp_v_pe, cmp_v_w1, cmp_v_w2,
              w_branch_a, w_branch_b, w_out, ffn2_norm, ffn2_w_gate, ffn2_w_up, ffn2_w_down)
    for l in range(params[0].shape[0]):
        x = _layer(x, *[p[l] for p in params])
    return x
```

```python
import functools

import numpy as np
import jax
import jax.numpy as jnp
from jax import lax
from jax.experimental import pallas as pl
from jax.experimental.pallas import tpu as pltpu

D_MODEL = 2048
D_FF = 5504
A_WIDTH = 2048
A_GROUPS = 8
A_CHUNK = 128
N_HEADS = 16
HEAD_DIM = 128
N_KV_GROUPS = 2
HEADS_PER_GROUP = N_HEADS // N_KV_GROUPS
CMP_BLOCK = 32
CMP_STRIDE = 16
CMP_HIDDEN = 256
SEL_BLOCK = 64
SEL_TOP = 16
WINDOW = 512
ROPE_THETA = 500000.0
ROPE_DIM = HEAD_DIM // 4
EPS = 1e-6
NEG = -1e30
FORCE_BONUS = 1e4
LOG2_E = 1.4426950408889634
B_WIDTH = N_HEADS * HEAD_DIM
KV_WIDTH = N_KV_GROUPS * HEAD_DIM

LANES = 128
HEAD_GATE_BLK = (B_WIDTH + 6 * KV_WIDTH) // LANES
VMEM_LIMIT = 56 * 1024 * 1024

ROW_TILE = 1024
FFN_HIDDEN_TILE = 256
PROJ_A_TILE = 2048
PROJ_B_TILE = 1280
MERGE_TILE = 1024
POINTWISE_ROWS = 512
ATTN_Q_TILE = 256
SEL_KEY_TILE = 512

F32 = jnp.float32
BF16 = jnp.bfloat16
NT_DIMS = (((1,), (1,)), ((), ()))


def _params(*sem):
    return pltpu.CompilerParams(dimension_semantics=sem, vmem_limit_bytes=VMEM_LIMIT)


def _rms(x, g):
    ms = jnp.mean(x * x, axis=-1, keepdims=True)
    return x * lax.rsqrt(ms + EPS) * g


def _ffn_kernel(x_ref, g_ref, wg_ref, wu_ref, wd_ref, o_ref, h_ref, *, nf, tf, f_last):
    j = pl.program_id(1)

    def partial(width):
        h = h_ref[...]
        a = jnp.dot(h, wg_ref[:, :width].astype(BF16), preferred_element_type=F32)
        b = jnp.dot(h, wu_ref[:, :width].astype(BF16), preferred_element_type=F32)
        act = (a * jax.nn.sigmoid(a) * b).astype(BF16)
        return jnp.dot(act, wd_ref[:width, :].astype(BF16), preferred_element_type=F32)

    @pl.when(j == 0)
    def _():
        h_ref[...] = _rms(x_ref[...], g_ref[...]).astype(BF16)
        o_ref[...] = partial(f_last)

    @pl.when((j > 0) & (j < nf - 1))
    def _():
        o_ref[...] += partial(tf)

    @pl.when(j == nf - 1)
    def _():
        o_ref[...] = x_ref[...] + 0.5 * (o_ref[...] + partial(tf))


def _ffn(x, norm, wg, wu, wd, *, tm=ROW_TILE, tf=FFN_HIDDEN_TILE):
    T, D = x.shape
    F = wg.shape[1]
    nf = pl.cdiv(F, tf)
    f_last = F - (nf - 1) * tf
    hid = lambda j: (j + nf - 1) % nf
    return pl.pallas_call(
        functools.partial(_ffn_kernel, nf=nf, tf=tf, f_last=f_last),
        grid=(T // tm, nf),
        in_specs=[
            pl.BlockSpec((tm, D), lambda i, j: (i, 0)),
            pl.BlockSpec((1, D), lambda i, j: (0, 0)),
            pl.BlockSpec((D, tf), lambda i, j: (0, hid(j))),
            pl.BlockSpec((D, tf), lambda i, j: (0, hid(j))),
            pl.BlockSpec((tf, D), lambda i, j: (hid(j), 0)),
        ],
        out_specs=pl.BlockSpec((tm, D), lambda i, j: (i, 0)),
        out_shape=jax.ShapeDtypeStruct((T, D), F32),
        scratch_shapes=[pltpu.VMEM((tm, D), BF16)],
        compiler_params=_params("parallel", "arbitrary"),
        name="ffn",
    )(x, norm, wg, wu, wd)


def _proj_kernel(x_ref, g_ref, w_ref, o_ref, h_ref):
    @pl.when(pl.program_id(1) == 0)
    def _():
        h_ref[...] = _rms(x_ref[...], g_ref[...]).astype(BF16)

    o_ref[...] = jnp.dot(h_ref[...], w_ref[...], preferred_element_type=F32).astype(o_ref.dtype)


def _proj(x, norm, w, out_dtype, *, tm, tn):
    T, D = x.shape
    N = w.shape[1]
    return pl.pallas_call(
        _proj_kernel,
        grid=(T // tm, N // tn),
        in_specs=[
            pl.BlockSpec((tm, D), lambda i, j: (i, 0)),
            pl.BlockSpec((1, D), lambda i, j: (0, 0)),
            pl.BlockSpec((D, tn), lambda i, j: (0, j)),
        ],
        out_specs=pl.BlockSpec((tm, tn), lambda i, j: (i, j)),
        out_shape=jax.ShapeDtypeStruct((T, N), out_dtype),
        scratch_shapes=[pltpu.VMEM((tm, D), BF16)],
        compiler_params=_params("parallel", "arbitrary"),
        name="proj",
    )(x, norm, w)


def _gmlp_kernel(u_ref, v_ref, vn_ref, ws_ref, bs_ref, o_ref, *, tr):
    gw = A_WIDTH // A_GROUPS
    v = jax.nn.gelu(v_ref[...].astype(F32))
    vn = _rms(v, vn_ref[...]).astype(BF16)
    row = lax.broadcasted_iota(jnp.int32, (A_CHUNK, A_CHUNK), 0)
    col = lax.broadcasted_iota(jnp.int32, (A_CHUNK, A_CHUNK), 1)
    for g in range(A_GROUPS):
        w = jnp.where(col <= row, ws_ref[g], 0.0).astype(BF16)
        bias = bs_ref[:, g:g + 1]
        for c in range(tr // A_CHUNK):
            rows = slice(c * A_CHUNK, (c + 1) * A_CHUNK)
            cols = slice(g * gw, (g + 1) * gw)
            mixed = jnp.dot(w, vn[rows, cols], preferred_element_type=F32) + bias
            u = jax.nn.gelu(u_ref[rows, cols].astype(F32))
            o_ref[rows, cols] = (u * mixed).astype(o_ref.dtype)


def _gmlp(proj_a, a_v_norm, w_s, b_s_t, *, tr=POINTWISE_ROWS):
    T = proj_a.shape[0]
    return pl.pallas_call(
        functools.partial(_gmlp_kernel, tr=tr),
        grid=(T // tr,),
        in_specs=[
            pl.BlockSpec((tr, A_WIDTH), lambda i: (i, 0)),
            pl.BlockSpec((tr, A_WIDTH), lambda i: (i, 1)),
            pl.BlockSpec((1, A_WIDTH), lambda i: (0, 0)),
            pl.BlockSpec((A_GROUPS, A_CHUNK, A_CHUNK), lambda i: (0, 0, 0)),
            pl.BlockSpec((A_CHUNK, A_GROUPS), lambda i: (0, 0)),
        ],
        out_specs=pl.BlockSpec((tr, A_WIDTH), lambda i: (i, 0)),
        out_shape=jax.ShapeDtypeStruct((T, A_WIDTH), BF16),
        compiler_params=_params("parallel"),
        name="gmlp",
    )(proj_a, proj_a, a_v_norm, w_s, b_s_t)


def _rope(y, c, su, sd):
    return y * c + pltpu.roll(y, 16, 1) * su + pltpu.roll(y, HEAD_DIM - 16, 1) * sd


def _norm_rope_mxu(x, g, c, s, w):
    xg = x * g
    lhs = jnp.concatenate([(x * x).astype(BF16), xg.astype(BF16)], axis=1)
    res = jnp.dot(lhs, w, preferred_element_type=F32)
    r = lax.rsqrt(res[:, :HEAD_DIM] * (1.0 / HEAD_DIM) + EPS)
    return (xg * c + res[:, HEAD_DIM:] * s) * r


def _prep_kernel(q_ref, ks_ref, vs_ref, kw_ref, vw_ref, qg_ref, ksg_ref, kwg_ref,
                 cq_ref, sq_ref, ck_ref, sk_ref, w_ref,
                 qn_ref, ksa_ref, vso_ref, kwn_ref, vwo_ref, *, tr, seq):
    cq, sq, ck, sk, w = cq_ref[...], sq_ref[...], ck_ref[...], sk_ref[...], w_ref[...]
    for h in range(N_HEADS):
        cols = slice(h * HEAD_DIM, (h + 1) * HEAD_DIM)
        qn_ref[:, cols] = _norm_rope_mxu(q_ref[:, cols], qg_ref[...], cq, sq, w).astype(BF16)
    t0 = (pl.program_id(0) * tr) % seq
    t = t0 + lax.broadcasted_iota(jnp.int32, (tr, LANES), 0)
    lane = lax.broadcasted_iota(jnp.int32, (tr, LANES), 1)
    onehot = jnp.where((t // SEL_BLOCK) == lane, 1.0, 0.0).astype(BF16)
    ones = jnp.ones((tr, HEAD_DIM), BF16)
    for g in range(N_KV_GROUPS):
        cols = slice(g * HEAD_DIM, (g + 1) * HEAD_DIM)
        lo = slice(2 * g * HEAD_DIM, (2 * g + 1) * HEAD_DIM)
        hi = slice((2 * g + 1) * HEAD_DIM, (2 * g + 2) * HEAD_DIM)
        ksa_ref[:, lo] = _norm_rope_mxu(ks_ref[:, cols], ksg_ref[...], ck, sk, w).astype(BF16)
        ksa_ref[:, hi] = onehot
        kwn_ref[:, cols] = _norm_rope_mxu(kw_ref[:, cols], kwg_ref[...], ck, sk, w).astype(BF16)
        vso_ref[:, lo] = vs_ref[:, cols].astype(BF16)
        vso_ref[:, hi] = ones
        vwo_ref[:, lo] = vw_ref[:, cols].astype(BF16)
        vwo_ref[:, hi] = ones


def _rope_matrix():
    half = ROPE_DIM // 2
    w = np.zeros((2 * HEAD_DIM, 2 * HEAD_DIM), np.float32)
    w[:HEAD_DIM, :HEAD_DIM] = 1.0
    for j in range(half):
        w[HEAD_DIM + j + half, HEAD_DIM + j] = 1.0
        w[HEAD_DIM + j, HEAD_DIM + j + half] = 1.0
    return jnp.asarray(w, dtype=BF16)


def _prep(proj_b, q_norm, k_slc_norm, k_win_norm, rope_c, rope_su, rope_sd, *, seq,
          tr=POINTWISE_ROWS):
    T = proj_b.shape[0]
    kvb = B_WIDTH // KV_WIDTH
    nrt = seq // tr
    q_scale = HEAD_DIM ** -0.5 * LOG2_E
    rope_s = rope_su + rope_sd
    w = _rope_matrix()
    row = lambda i: (i, 0)
    tab = pl.BlockSpec((tr, HEAD_DIM), lambda i: (i % nrt, 0))
    gain = pl.BlockSpec((1, HEAD_DIM), lambda i: (0, 0))
    kv_in = lambda k: pl.BlockSpec((tr, KV_WIDTH), lambda i: (i, kvb + k))
    return pl.pallas_call(
        functools.partial(_prep_kernel, tr=tr, seq=seq),
        grid=(T // tr,),
        in_specs=[pl.BlockSpec((tr, B_WIDTH), row), kv_in(2), kv_in(3), kv_in(4), kv_in(5),
                  gain, gain, gain, tab, tab, tab, tab, pl.BlockSpec(w.shape, lambda i: (0, 0))],
        out_specs=[pl.BlockSpec((tr, B_WIDTH), row), pl.BlockSpec((tr, 2 * KV_WIDTH), row),
                   pl.BlockSpec((tr, 2 * KV_WIDTH), row), pl.BlockSpec((tr, KV_WIDTH), row),
                   pl.BlockSpec((tr, 2 * KV_WIDTH), row)],
        out_shape=[jax.ShapeDtypeStruct((T, B_WIDTH), BF16),
                   jax.ShapeDtypeStruct((T, 2 * KV_WIDTH), BF16),
                   jax.ShapeDtypeStruct((T, 2 * KV_WIDTH), BF16),
                   jax.ShapeDtypeStruct((T, KV_WIDTH), BF16),
                   jax.ShapeDtypeStruct((T, 2 * KV_WIDTH), BF16)],
        compiler_params=_params("parallel"),
        name="qk_prep",
    )(proj_b, proj_b, proj_b, proj_b, proj_b, q_norm, k_slc_norm, k_win_norm,
      rope_c * q_scale, rope_s * q_scale, rope_c, rope_s, w)


def _compress_kernel(zk_ref, zv_ref, pek_ref, pev_ref, w1k_ref, w2k_ref, w1v_ref, w2v_ref,
                     kg_ref, c_ref, su_ref, sd_ref, kc_ref, vc_ref, zr_ref):
    half = CMP_STRIDE * HEAD_DIM
    nb = zr_ref.shape[0]

    def comp(z_ref, pe_ref, w1_ref, w2_ref):
        for l in range(CMP_STRIDE):
            zr_ref[:, l * HEAD_DIM:(l + 1) * HEAD_DIM] = z_ref[pl.ds(l, nb, stride=CMP_STRIDE), :]
        z = zr_ref[...]
        top = jnp.dot((z + pe_ref[:, :half]).astype(BF16), w1_ref[:half, :], preferred_element_type=F32)
        bot = jnp.dot((z + pe_ref[:, half:]).astype(BF16), w1_ref[half:, :], preferred_element_type=F32)
        hid = top + pltpu.roll(bot, nb - 1, 0)
        return jnp.dot(jax.nn.gelu(hid).astype(BF16), w2_ref[...], preferred_element_type=F32)

    k = comp(zk_ref, pek_ref, w1k_ref, w2k_ref)
    kc_ref[...] = _rope(_rms(k, kg_ref[...]), c_ref[...], su_ref[...], sd_ref[...]).astype(BF16)
    vc_ref[...] = comp(zv_ref, pev_ref, w1v_ref, w2v_ref).astype(BF16)


def _compress(proj_b, pek, pev, w1k, w2k, w1v, w2v, k_cmp_norm, cmp_c, cmp_su, cmp_sd, *, batch, seq):
    G = N_KV_GROUPS
    nb = seq // CMP_STRIDE
    kc_blk = B_WIDTH // HEAD_DIM
    vc_blk = kc_blk + G
    ospec = pl.BlockSpec((None, None, nb, HEAD_DIM), lambda b, g: (b, g, 0, 0))
    full = lambda a: pl.BlockSpec(a.shape, lambda b, g: (0,) * a.ndim)
    consts = (pek, pev, w1k, w2k, w1v, w2v, k_cmp_norm, cmp_c, cmp_su, cmp_sd)
    return pl.pallas_call(
        _compress_kernel,
        grid=(batch, G),
        in_specs=[pl.BlockSpec((seq, HEAD_DIM), lambda b, g: (b, kc_blk + g)),
                  pl.BlockSpec((seq, HEAD_DIM), lambda b, g: (b, vc_blk + g))]
                 + [full(a) for a in consts],
        out_specs=[ospec, ospec],
        out_shape=[jax.ShapeDtypeStruct((batch, G, nb, HEAD_DIM), BF16)] * 2,
        scratch_shapes=[pltpu.VMEM((nb, CMP_STRIDE * HEAD_DIM), F32)],
        compiler_params=_params("parallel", "parallel"),
        name="compress",
    )(proj_b, proj_b, *consts)


def _cmp_attn_kernel(q_ref, kc_ref, vc_ref, kw_ref, vw_ref, gate_ref, ovt_ref, oc_ref, bias_ref,
                     qs_ref, *, tq):
    hpg = HEADS_PER_GROUP
    M = hpg * tq
    nb = kc_ref.shape[0]
    n_sel = ovt_ref.shape[0]
    t0 = pl.program_id(2) * tq
    for h in range(hpg):
        qs_ref[h * tq:(h + 1) * tq, :] = q_ref[:, h * HEAD_DIM:(h + 1) * HEAD_DIM]

    wlen = WINDOW + tq
    kstart = pl.multiple_of(jnp.maximum(t0 - WINDOW, 0), tq)
    sw = lax.dot_general(qs_ref[...], kw_ref[pl.ds(kstart, wlen), :], NT_DIMS,
                         preferred_element_type=F32)
    tw = t0 + lax.broadcasted_iota(jnp.int32, (1, tq, wlen), 1)
    kp = kstart + lax.broadcasted_iota(jnp.int32, (1, tq, wlen), 2)
    keep = (tw - kp).astype(jnp.uint32) < WINDOW
    sw = jnp.where(keep, sw.reshape(hpg, tq, wlen), NEG).reshape(M, wlen)
    pw = jnp.exp2(sw - jnp.max(sw, axis=-1, keepdims=True))
    aw = jnp.dot(pw.astype(BF16), vw_ref[pl.ds(kstart, wlen), :], preferred_element_type=F32)
    o_w = aw[:, :HEAD_DIM] * (1.0 / aw[:, HEAD_DIM:])

    s = lax.dot_general(qs_ref[...], kc_ref[...], NT_DIMS, preferred_element_type=F32)
    s = s.reshape(hpg, tq, nb)
    t = t0 + lax.broadcasted_iota(jnp.int32, (1, tq, nb), 1)
    n = lax.broadcasted_iota(jnp.int32, (1, tq, nb), 2)
    mask = n * CMP_STRIDE + (CMP_BLOCK - 1) <= t
    s = jnp.where(mask, s, NEG)
    e = jnp.exp2(s - jnp.max(s, axis=-1, keepdims=True))
    inv = 1.0 / jnp.sum(e, axis=-1, keepdims=True)
    p = jnp.where(mask, e * inv, 0.0)
    o = jnp.dot(p.reshape(hpg * tq, nb).astype(BF16), vc_ref[...], preferred_element_type=F32)
    sig = jax.nn.sigmoid(gate_ref[...])
    for h in range(hpg):
        rows = slice(h * tq, (h + 1) * tq)
        oc_ref[:, h * HEAD_DIM:(h + 1) * HEAD_DIM] = (
            o[rows] * sig[:, h:h + 1]
            + o_w[rows] * sig[:, 2 * hpg + h:2 * hpg + h + 1]).astype(oc_ref.dtype)

    psum = jnp.sum(p, axis=0)
    ovt = ovt_ref[...]
    hi = psum.astype(BF16)
    r1 = psum - hi.astype(F32)
    mid = r1.astype(BF16)
    lo = (r1 - mid.astype(F32)).astype(BF16)
    imp = (lax.dot_general(ovt, hi, NT_DIMS, preferred_element_type=F32)
           + lax.dot_general(ovt, mid, NT_DIMS, preferred_element_type=F32)
           + lax.dot_general(ovt, lo, NT_DIMS, preferred_element_type=F32))
    j = lax.broadcasted_iota(jnp.int32, (n_sel, tq), 0)
    tt = t0 + lax.broadcasted_iota(jnp.int32, (n_sel, tq), 1)
    cur = tt // SEL_BLOCK
    forced = jnp.where((j == 0) | (j == cur) | (j == cur - 1), FORCE_BONUS, 0.0)
    score = jnp.where(j * SEL_BLOCK <= tt, imp + forced, NEG)
    jrow = j.astype(F32)
    below_neg = 3.0 * NEG
    chosen = jnp.zeros((n_sel, tq), F32)
    for _ in range(SEL_TOP):
        top = jnp.max(score, axis=0, keepdims=True)
        idx = jnp.min(jnp.where(score == top, jrow, float(n_sel)), axis=0, keepdims=True)
        hit = jrow == idx
        chosen = jnp.where(hit, 1.0, chosen)
        score = jnp.where(hit, below_neg, score)
    bias = jnp.where(chosen > 0.5, 0.0, NEG)
    bias = jnp.concatenate([bias, jnp.zeros((LANES - n_sel, tq), F32)], axis=0)
    bias_ref[...] = bias.T.astype(bias_ref.dtype)


def _cmp_attn(qn, kc, vc, kwn, vw, proj_b, ovt, *, batch, seq, tq=ATTN_Q_TILE):
    assert WINDOW % tq == 0
    T = qn.shape[0]
    G = N_KV_GROUPS
    nq = seq // tq
    gw = HEADS_PER_GROUP * HEAD_DIM
    nb = kc.shape[2]
    rowg = lambda b, g, i: (b * nq + i, g)
    seqg = lambda b, g, i: (b, g)
    return pl.pallas_call(
        functools.partial(_cmp_attn_kernel, tq=tq),
        grid=(batch, G, nq),
        in_specs=[
            pl.BlockSpec((tq, gw), rowg),
            pl.BlockSpec((None, None, nb, HEAD_DIM), lambda b, g, i: (b, g, 0, 0)),
            pl.BlockSpec((None, None, nb, HEAD_DIM), lambda b, g, i: (b, g, 0, 0)),
            pl.BlockSpec((seq, HEAD_DIM), seqg),
            pl.BlockSpec((seq, 2 * HEAD_DIM), seqg),
            pl.BlockSpec((tq, LANES), lambda b, g, i: (b * nq + i, HEAD_GATE_BLK + g)),
            pl.BlockSpec(ovt.shape, lambda b, g, i: (0, 0)),
        ],
        out_specs=[pl.BlockSpec((tq, gw), rowg), pl.BlockSpec((tq, LANES), rowg)],
        out_shape=[jax.ShapeDtypeStruct((T, B_WIDTH), BF16),
                   jax.ShapeDtypeStruct((T, G * LANES), BF16)],
        scratch_shapes=[pltpu.VMEM((HEADS_PER_GROUP * tq, HEAD_DIM), BF16)],
        compiler_params=_params("parallel", "parallel", "parallel"),
        name="cmp_win_attn",
    )(qn, kc, vc, kwn, vw, proj_b, ovt)


def _sw_attn_kernel(q_ref, bias_ref, ksa_ref, vs_ref, oc_ref, gate_ref, o_ref,
                    qs_ref, m_ref, acc_ref, ss0_ref, ss1_ref, *, tq, tks):
    hpg = HEADS_PER_GROUP
    M = hpg * tq
    ss_ref = (ss0_ref, ss1_ref)
    t0 = pl.program_id(2) * tq
    bias = bias_ref[...]
    for h in range(hpg):
        qs_ref[h * tq:(h + 1) * tq, :HEAD_DIM] = q_ref[:, h * HEAD_DIM:(h + 1) * HEAD_DIM]
        qs_ref[h * tq:(h + 1) * tq, HEAD_DIM:] = bias

    def reset():
        m_ref[...] = jnp.full_like(m_ref, NEG)
        acc_ref[...] = jnp.zeros_like(acc_ref)

    def step(s, v):
        m_old = m_ref[...]
        m_new = jnp.maximum(m_old, jnp.max(s, axis=-1, keepdims=True))
        alpha = jnp.exp2(m_old - m_new)
        p = jnp.exp2(s - jnp.tile(m_new, (1, s.shape[1] // LANES)))
        acc_ref[...] = (jnp.tile(alpha, (1, 2)) * acc_ref[...]
                        + jnp.dot(p.astype(BF16), v, preferred_element_type=F32))
        m_ref[...] = m_new

    def result():
        return acc_ref[:, :HEAD_DIM] * (1.0 / acc_ref[:, HEAD_DIM:])

    def masked(s, keep):
        tk = s.shape[1]
        return jnp.where(keep, s.reshape(hpg, tq, tk), NEG).reshape(M, tk)

    def positions(tk):
        t = t0 + lax.broadcasted_iota(jnp.int32, (1, tq, tk), 1)
        kpos = lax.broadcasted_iota(jnp.int32, (1, tq, tk), 2)
        return t, kpos

    def sel_base(j):
        return pl.multiple_of(j * tks, tks)

    def sel_scores(j):
        base = sel_base(j)
        k = ksa_ref[pl.ds(base, tks), :]
        s = lax.dot_general(qs_ref[...], k, NT_DIMS, preferred_element_type=F32)
        t, kpos = positions(tks)
        return masked(s, kpos + base <= t)

    sel_last = (t0 + tq - 1) // tks

    def by_parity(n, fn):
        @pl.when(n % 2 == 0)
        def _():
            fn(0, 1)

        @pl.when(n % 2 == 1)
        def _():
            fn(1, 0)

    reset()
    ss_ref[0][...] = sel_scores(0)

    def sel_body(j, carry):
        def run(cur, nxt):
            ss_ref[nxt][...] = sel_scores(j + 1)
            step(ss_ref[cur][...], vs_ref[pl.ds(sel_base(j), tks), :])

        by_parity(j, run)
        return carry

    lax.fori_loop(0, sel_last, sel_body, 0)
    by_parity(sel_last,
              lambda cur, nxt: step(ss_ref[cur][...], vs_ref[pl.ds(sel_base(sel_last), tks), :]))
    o_s = result()

    sig = jax.nn.sigmoid(gate_ref[...])
    for h in range(hpg):
        rows = slice(h * tq, (h + 1) * tq)
        cols = slice(h * HEAD_DIM, (h + 1) * HEAD_DIM)
        o = oc_ref[:, cols].astype(F32) + sig[:, hpg + h:hpg + h + 1] * o_s[rows]
        o_ref[:, cols] = o.astype(o_ref.dtype)


def _sw_attn(qn, bias, ksa, vs, o_c, proj_b, *, batch, seq, tq=ATTN_Q_TILE,
             tks=SEL_KEY_TILE):
    T = qn.shape[0]
    G = N_KV_GROUPS
    nq = seq // tq
    gw = HEADS_PER_GROUP * HEAD_DIM
    M = HEADS_PER_GROUP * tq
    rowg = lambda b, g, i: (b * nq + i, g)
    seqg = lambda b, g, i: (b, g)
    return pl.pallas_call(
        functools.partial(_sw_attn_kernel, tq=tq, tks=tks),
        grid=(batch, G, nq),
        in_specs=[
            pl.BlockSpec((tq, gw), rowg),
            pl.BlockSpec((tq, LANES), rowg),
            pl.BlockSpec((seq, 2 * HEAD_DIM), seqg),
            pl.BlockSpec((seq, 2 * HEAD_DIM), seqg),
            pl.BlockSpec((tq, gw), rowg),
            pl.BlockSpec((tq, LANES), lambda b, g, i: (b * nq + i, HEAD_GATE_BLK + g)),
        ],
        out_specs=pl.BlockSpec((tq, gw), rowg),
        out_shape=jax.ShapeDtypeStruct((T, B_WIDTH), BF16),
        scratch_shapes=[pltpu.VMEM((M, 2 * HEAD_DIM), BF16), pltpu.VMEM((M, LANES), F32),
                        pltpu.VMEM((M, 2 * HEAD_DIM), F32),
                        pltpu.VMEM((M, tks), F32), pltpu.VMEM((M, tks), F32)],
        compiler_params=_params("parallel", "parallel", "parallel"),
        name="sel_attn",
    )(qn, bias, ksa, vs, o_c, proj_b)


def _merge_kernel(oa_ref, ob_ref, ga_ref, gb_ref, wa_ref, wb_ref, o_ref):
    a = jnp.dot(oa_ref[...], wa_ref[...], preferred_element_type=F32)
    b = jnp.dot(ob_ref[...], wb_ref[...], preferred_element_type=F32)
    ga = jax.nn.sigmoid(ga_ref[...].astype(F32))
    gb = jax.nn.sigmoid(gb_ref[...].astype(F32))
    o_ref[...] = (ga * a + gb * b).astype(o_ref.dtype)


def _merge(o_a, o_b, proj_a, wa, wb, *, tm=ROW_TILE, tn=MERGE_TILE):
    T, D = o_a.shape
    N = wa.shape[1]
    nn = N // tn
    ga_blk = 2 * A_WIDTH // tn
    return pl.pallas_call(
        _merge_kernel,
        grid=(T // tm, nn),
        in_specs=[
            pl.BlockSpec((tm, D), lambda i, j: (i, 0)),
            pl.BlockSpec((tm, D), lambda i, j: (i, 0)),
            pl.BlockSpec((tm, tn), lambda i, j: (i, ga_blk + j)),
            pl.BlockSpec((tm, tn), lambda i, j: (i, ga_blk + nn + j)),
            pl.BlockSpec((D, tn), lambda i, j: (0, j)),
            pl.BlockSpec((D, tn), lambda i, j: (0, j)),
        ],
        out_specs=pl.BlockSpec((tm, tn), lambda i, j: (i, j)),
        out_shape=jax.ShapeDtypeStruct((T, N), BF16),
        compiler_params=_params("parallel", "arbitrary"),
        name="merge",
    )(o_a, o_b, proj_a, proj_a, wa, wb)


def _out_kernel(x_ref, m_ref, w_ref, o_ref):
    o_ref[...] = x_ref[...] + jnp.dot(m_ref[...], w_ref[...], preferred_element_type=F32)


def _out_proj(x, merged, w, *, tm=ROW_TILE, tn=MERGE_TILE):
    T, D = merged.shape
    N = w.shape[1]
    return pl.pallas_call(
        _out_kernel,
        grid=(T // tm, N // tn),
        in_specs=[
            pl.BlockSpec((tm, tn), lambda i, j: (i, j)),
            pl.BlockSpec((tm, D), lambda i, j: (i, 0)),
            pl.BlockSpec((D, tn), lambda i, j: (0, j)),
        ],
        out_specs=pl.BlockSpec((tm, tn), lambda i, j: (i, j)),
        out_shape=jax.ShapeDtypeStruct((T, N), F32),
        compiler_params=_params("parallel", "arbitrary"),
        name="out_proj",
    )(x, merged, w)


def _rope_tables(pos):
    half = ROPE_DIM // 2
    inv = ROPE_THETA ** (-2.0 * jnp.arange(half, dtype=F32) / ROPE_DIM)
    ang = pos.astype(F32)[:, None] * inv
    cos, sin = jnp.cos(ang), jnp.sin(ang)
    n = pos.shape[0]
    z16 = jnp.zeros((n, half), F32)
    rest = HEAD_DIM - ROPE_DIM
    c = jnp.concatenate([cos, cos, jnp.ones((n, rest), F32)], axis=1)
    su = jnp.concatenate([z16, sin, jnp.zeros((n, rest), F32)], axis=1)
    sd = jnp.concatenate([-sin, z16, jnp.zeros((n, rest), F32)], axis=1)
    return c, su, sd


def _split_w_in(w_in):
    sizes = [A_WIDTH, A_WIDTH, B_WIDTH] + [KV_WIDTH] * 6 + [3 * N_HEADS, D_MODEL, D_MODEL]
    offs = np.concatenate([[0], np.cumsum(sizes)])
    w = w_in.astype(BF16)
    w_a = jnp.concatenate([w[:, :offs[2]], w[:, offs[10]:]], axis=1)
    gates = w[:, offs[9]:offs[10]]
    blocks = []
    for g in range(N_KV_GROUPS):
        cols = [gates[:, br * N_HEADS + g * HEADS_PER_GROUP:br * N_HEADS + (g + 1) * HEADS_PER_GROUP]
                for br in range(3)]
        pad = jnp.zeros((w.shape[0], LANES - 3 * HEADS_PER_GROUP), w.dtype)
        blocks.extend(cols + [pad])
    w_b = jnp.concatenate([w[:, offs[2]:offs[9]]] + blocks, axis=1)
    return w_a, w_b


def _overlap_t(nb_pad, n_sel):
    cmp_start = np.arange(nb_pad) * CMP_STRIDE
    sel_start = np.arange(n_sel) * SEL_BLOCK
    ov = ((cmp_start[None, :] < sel_start[:, None] + SEL_BLOCK)
          & (cmp_start[None, :] + CMP_BLOCK > sel_start[:, None]))
    return jnp.asarray(ov, dtype=BF16)


def _layer(x, ffn1_norm, ffn1_w_gate, ffn1_w_up, ffn1_w_down, mix_norm, w_in,
           a_v_norm, a_w_s, a_b_s, q_norm, k_cmp_norm, k_slc_norm, k_win_norm,
           cmp_k_pe, cmp_k_w1, cmp_k_w2, cmp_v_pe, cmp_v_w1, cmp_v_w2,
           w_branch_a, w_branch_b, w_out, ffn2_norm, ffn2_w_gate, ffn2_w_up, ffn2_w_down):
    B, S, D = x.shape
    T = B * S
    G = N_KV_GROUPS
    row = lambda v: v.reshape(1, -1)
    x0 = x.reshape(T, D)

    x1 = _ffn(x0, row(ffn1_norm), ffn1_w_gate, ffn1_w_up, ffn1_w_down)

    w_a, w_b = _split_w_in(w_in)
    proj_a = _proj(x1, row(mix_norm), w_a, BF16, tm=ROW_TILE, tn=PROJ_A_TILE)
    proj_b = _proj(x1, row(mix_norm), w_b, F32, tm=ROW_TILE, tn=PROJ_B_TILE)

    o_a = _gmlp(proj_a, row(a_v_norm), a_w_s, a_b_s.T)

    pos = jnp.arange(S)
    qn, ksa, vs, kwn, vw = _prep(proj_b, row(q_norm), row(k_slc_norm), row(k_win_norm),
                                 *_rope_tables(pos), seq=S)

    nb_pad = S // CMP_STRIDE
    cmp_end = jnp.arange(nb_pad) * CMP_STRIDE + (CMP_BLOCK - 1)
    kc, vc = _compress(proj_b, cmp_k_pe.reshape(1, -1), cmp_v_pe.reshape(1, -1),
                       cmp_k_w1.astype(BF16), cmp_k_w2.astype(BF16),
                       cmp_v_w1.astype(BF16), cmp_v_w2.astype(BF16),
                       row(k_cmp_norm), *_rope_tables(cmp_end), batch=B, seq=S)

    o_cw, bias = _cmp_attn(qn, kc, vc, kwn, vw, proj_b, _overlap_t(nb_pad, S // SEL_BLOCK),
                           batch=B, seq=S)
    o_b = _sw_attn(qn, bias, ksa, vs, o_cw, proj_b, batch=B, seq=S)

    merged = _merge(o_a, o_b, proj_a, w_branch_a.astype(BF16), w_branch_b.astype(BF16))
    x2 = _out_proj(x1, merged, w_out.astype(BF16))

    x3 = _ffn(x2, row(ffn2_norm), ffn2_w_gate, ffn2_w_up, ffn2_w_down)
    return x3.reshape(B, S, D)


def kernel(x, ffn1_norm, ffn1_w_gate, ffn1_w_up, ffn1_w_down, mix_norm, w_in, a_v_norm, a_w_s, a_b_s, q_norm, k_cmp_norm, k_slc_norm, k_win_norm, cmp_k_pe, cmp_k_w1, cmp_k_w2, cmp_v_pe, cmp_v_w1, cmp_v_w2, w_branch_a, w_branch_b, w_out, ffn2_norm, ffn2_w_gate, ffn2_w_up, ffn2_w_down):
    params = (ffn1_norm, ffn1_w_gate, ffn1_w_up, ffn1_w_down, mix_norm, w_in,
              a_v_norm, a_w_s, a_b_s, q_norm, k_cmp_norm, k_slc_norm, k_win_norm,
              cmp_k_pe, cmp_k_w1, cmp_k_w2, cmp_v_pe, cmp_v_w1, cmp_v_w2,
              w_branch_a, w_branch_b, w_out, ffn2_norm, ffn2_w_gate, ffn2_w_up, ffn2_w_down)
    for l in range(params[0].shape[0]):
        x = _layer(x, *[p[l] for p in params])
    return x
```

```python
import functools

import numpy as np
import jax
import jax.numpy as jnp
from jax import lax
from jax.experimental import pallas as pl
from jax.experimental.pallas import tpu as pltpu

D_MODEL = 2048
D_FF = 5504
A_WIDTH = 2048
A_GROUPS = 8
A_CHUNK = 128
N_HEADS = 16
HEAD_DIM = 128
N_KV_GROUPS = 2
HEADS_PER_GROUP = N_HEADS // N_KV_GROUPS
CMP_BLOCK = 32
CMP_STRIDE = 16
CMP_HIDDEN = 256
SEL_BLOCK = 64
SEL_TOP = 16
WINDOW = 512
ROPE_THETA = 500000.0
ROPE_DIM = HEAD_DIM // 4
EPS = 1e-6
NEG = -1e30
FORCE_BONUS = 1e4
LOG2_E = 1.4426950408889634
B_WIDTH = N_HEADS * HEAD_DIM
KV_WIDTH = N_KV_GROUPS * HEAD_DIM

LANES = 128
HEAD_GATE_BLK = (B_WIDTH + 6 * KV_WIDTH) // LANES
VMEM_LIMIT = 56 * 1024 * 1024

ROW_TILE = 1024
FFN_HIDDEN_TILE = 256
PROJ_A_TILE = 2048
PROJ_B_TILE = 1920
MERGE_TILE = 1024
POINTWISE_ROWS = 512
ATTN_Q_TILE = 256
SEL_KEY_TILE = 512

F32 = jnp.float32
BF16 = jnp.bfloat16
NT_DIMS = (((1,), (1,)), ((), ()))


def _params(*sem):
    return pltpu.CompilerParams(dimension_semantics=sem, vmem_limit_bytes=VMEM_LIMIT)


def _rms(x, g):
    ms = jnp.mean(x * x, axis=-1, keepdims=True)
    return x * lax.rsqrt(ms + EPS) * g


def _ffn_kernel(x_ref, g_ref, wg_ref, wu_ref, wd_ref, o_ref, h_ref, *, nf, tf, f_last):
    j = pl.program_id(1)

    def partial(width):
        h = h_ref[...]
        a = jnp.dot(h, wg_ref[:, :width].astype(BF16), preferred_element_type=F32)
        b = jnp.dot(h, wu_ref[:, :width].astype(BF16), preferred_element_type=F32)
        act = (a * jax.nn.sigmoid(a) * b).astype(BF16)
        return jnp.dot(act, wd_ref[:width, :].astype(BF16), preferred_element_type=F32)

    @pl.when(j == 0)
    def _():
        h_ref[...] = _rms(x_ref[...], g_ref[...]).astype(BF16)
        o_ref[...] = partial(f_last)

    @pl.when((j > 0) & (j < nf - 1))
    def _():
        o_ref[...] += partial(tf)

    @pl.when(j == nf - 1)
    def _():
        o_ref[...] = x_ref[...] + 0.5 * (o_ref[...] + partial(tf))


def _ffn(x, norm, wg, wu, wd, *, tm=ROW_TILE, tf=FFN_HIDDEN_TILE):
    T, D = x.shape
    F = wg.shape[1]
    nf = pl.cdiv(F, tf)
    f_last = F - (nf - 1) * tf
    hid = lambda j: (j + nf - 1) % nf
    return pl.pallas_call(
        functools.partial(_ffn_kernel, nf=nf, tf=tf, f_last=f_last),
        grid=(T // tm, nf),
        in_specs=[
            pl.BlockSpec((tm, D), lambda i, j: (i, 0)),
            pl.BlockSpec((1, D), lambda i, j: (0, 0)),
            pl.BlockSpec((D, tf), lambda i, j: (0, hid(j))),
            pl.BlockSpec((D, tf), lambda i, j: (0, hid(j))),
            pl.BlockSpec((tf, D), lambda i, j: (hid(j), 0)),
        ],
        out_specs=pl.BlockSpec((tm, D), lambda i, j: (i, 0)),
        out_shape=jax.ShapeDtypeStruct((T, D), F32),
        scratch_shapes=[pltpu.VMEM((tm, D), BF16)],
        compiler_params=_params("parallel", "arbitrary"),
        name="ffn",
    )(x, norm, wg, wu, wd)


def _proj_kernel(x_ref, g_ref, w_ref, o_ref, h_ref):
    @pl.when(pl.program_id(1) == 0)
    def _():
        h_ref[...] = _rms(x_ref[...], g_ref[...]).astype(BF16)

    o_ref[...] = jnp.dot(h_ref[...], w_ref[...], preferred_element_type=F32).astype(o_ref.dtype)


def _proj(x, norm, w, out_dtype, *, tm, tn):
    T, D = x.shape
    N = w.shape[1]
    return pl.pallas_call(
        _proj_kernel,
        grid=(T // tm, N // tn),
        in_specs=[
            pl.BlockSpec((tm, D), lambda i, j: (i, 0)),
            pl.BlockSpec((1, D), lambda i, j: (0, 0)),
            pl.BlockSpec((D, tn), lambda i, j: (0, j)),
        ],
        out_specs=pl.BlockSpec((tm, tn), lambda i, j: (i, j)),
        out_shape=jax.ShapeDtypeStruct((T, N), out_dtype),
        scratch_shapes=[pltpu.VMEM((tm, D), BF16)],
        compiler_params=_params("parallel", "arbitrary"),
        name="proj",
    )(x, norm, w)


def _gmlp_kernel(u_ref, v_ref, vn_ref, ws_ref, bs_ref, o_ref, *, tr):
    gw = A_WIDTH // A_GROUPS
    v = jax.nn.gelu(v_ref[...].astype(F32))
    vn = _rms(v, vn_ref[...]).astype(BF16)
    row = lax.broadcasted_iota(jnp.int32, (A_CHUNK, A_CHUNK), 0)
    col = lax.broadcasted_iota(jnp.int32, (A_CHUNK, A_CHUNK), 1)
    for g in range(A_GROUPS):
        w = jnp.where(col <= row, ws_ref[g], 0.0).astype(BF16)
        bias = bs_ref[:, g:g + 1]
        for c in range(tr // A_CHUNK):
            rows = slice(c * A_CHUNK, (c + 1) * A_CHUNK)
            cols = slice(g * gw, (g + 1) * gw)
            mixed = jnp.dot(w, vn[rows, cols], preferred_element_type=F32) + bias
            u = jax.nn.gelu(u_ref[rows, cols].astype(F32))
            o_ref[rows, cols] = (u * mixed).astype(o_ref.dtype)


def _gmlp(proj_a, a_v_norm, w_s, b_s_t, *, tr=POINTWISE_ROWS):
    T = proj_a.shape[0]
    return pl.pallas_call(
        functools.partial(_gmlp_kernel, tr=tr),
        grid=(T // tr,),
        in_specs=[
            pl.BlockSpec((tr, A_WIDTH), lambda i: (i, 0)),
            pl.BlockSpec((tr, A_WIDTH), lambda i: (i, 1)),
            pl.BlockSpec((1, A_WIDTH), lambda i: (0, 0)),
            pl.BlockSpec((A_GROUPS, A_CHUNK, A_CHUNK), lambda i: (0, 0, 0)),
            pl.BlockSpec((A_CHUNK, A_GROUPS), lambda i: (0, 0)),
        ],
        out_specs=pl.BlockSpec((tr, A_WIDTH), lambda i: (i, 0)),
        out_shape=jax.ShapeDtypeStruct((T, A_WIDTH), BF16),
        compiler_params=_params("parallel"),
        name="gmlp",
    )(proj_a, proj_a, a_v_norm, w_s, b_s_t)


def _rope(y, c, su, sd):
    half = ROPE_DIM // 2
    return y * c + pltpu.roll(y, half, 1) * su + pltpu.roll(y, HEAD_DIM - half, 1) * sd


def _norm_rope_mxu(x, g, c, s, w):
    xg = x * g
    lhs = jnp.concatenate([(x * x).astype(BF16), xg.astype(BF16)], axis=1)
    res = jnp.dot(lhs, w, preferred_element_type=F32)
    r = lax.rsqrt(res[:, :HEAD_DIM] * (1.0 / HEAD_DIM) + EPS)
    return (xg * c + res[:, HEAD_DIM:] * s) * r


def _prep_kernel(q_ref, ks_ref, vs_ref, kw_ref, vw_ref, qg_ref, ksg_ref, kwg_ref,
                 cq_ref, sq_ref, ck_ref, sk_ref, w_ref,
                 qn_ref, ksa_ref, vso_ref, kwn_ref, vwo_ref, *, tr, seq):
    cq, sq, ck, sk, w = cq_ref[...], sq_ref[...], ck_ref[...], sk_ref[...], w_ref[...]
    for h in range(N_HEADS):
        cols = slice(h * HEAD_DIM, (h + 1) * HEAD_DIM)
        qn_ref[:, cols] = _norm_rope_mxu(q_ref[:, cols], qg_ref[...], cq, sq, w).astype(BF16)
    t0 = (pl.program_id(0) * tr) % seq
    t = t0 + lax.broadcasted_iota(jnp.int32, (tr, LANES), 0)
    lane = lax.broadcasted_iota(jnp.int32, (tr, LANES), 1)
    onehot = jnp.where((t // SEL_BLOCK) == lane, 1.0, 0.0).astype(BF16)
    ones = jnp.ones((tr, HEAD_DIM), BF16)
    for g in range(N_KV_GROUPS):
        cols = slice(g * HEAD_DIM, (g + 1) * HEAD_DIM)
        lo = slice(2 * g * HEAD_DIM, (2 * g + 1) * HEAD_DIM)
        hi = slice((2 * g + 1) * HEAD_DIM, (2 * g + 2) * HEAD_DIM)
        ksa_ref[:, lo] = _norm_rope_mxu(ks_ref[:, cols], ksg_ref[...], ck, sk, w).astype(BF16)
        ksa_ref[:, hi] = onehot
        kwn_ref[:, cols] = _norm_rope_mxu(kw_ref[:, cols], kwg_ref[...], ck, sk, w).astype(BF16)
        vso_ref[:, lo] = vs_ref[:, cols].astype(BF16)
        vso_ref[:, hi] = ones
        vwo_ref[:, lo] = vw_ref[:, cols].astype(BF16)
        vwo_ref[:, hi] = ones


def _rope_matrix():
    half = ROPE_DIM // 2
    w = np.zeros((2 * HEAD_DIM, 2 * HEAD_DIM), np.float32)
    w[:HEAD_DIM, :HEAD_DIM] = 1.0
    for j in range(half):
        w[HEAD_DIM + j + half, HEAD_DIM + j] = 1.0
        w[HEAD_DIM + j, HEAD_DIM + j + half] = 1.0
    return jnp.asarray(w, dtype=BF16)


def _prep(proj_b, q_norm, k_slc_norm, k_win_norm, rope_c, rope_su, rope_sd, *, seq,
          tr=POINTWISE_ROWS):
    T = proj_b.shape[0]
    kvb = B_WIDTH // KV_WIDTH
    nrt = seq // tr
    q_scale = HEAD_DIM ** -0.5 * LOG2_E
    rope_s = rope_su + rope_sd
    w = _rope_matrix()
    row = lambda i: (i, 0)
    tab = pl.BlockSpec((tr, HEAD_DIM), lambda i: (i % nrt, 0))
    gain = pl.BlockSpec((1, HEAD_DIM), lambda i: (0, 0))
    kv_in = lambda k: pl.BlockSpec((tr, KV_WIDTH), lambda i: (i, kvb + k))
    return pl.pallas_call(
        functools.partial(_prep_kernel, tr=tr, seq=seq),
        grid=(T // tr,),
        in_specs=[pl.BlockSpec((tr, B_WIDTH), row), kv_in(2), kv_in(3), kv_in(4), kv_in(5),
                  gain, gain, gain, tab, tab, tab, tab, pl.BlockSpec(w.shape, lambda i: (0, 0))],
        out_specs=[pl.BlockSpec((tr, B_WIDTH), row), pl.BlockSpec((tr, 2 * KV_WIDTH), row),
                   pl.BlockSpec((tr, 2 * KV_WIDTH), row), pl.BlockSpec((tr, KV_WIDTH), row),
                   pl.BlockSpec((tr, 2 * KV_WIDTH), row)],
        out_shape=[jax.ShapeDtypeStruct((T, B_WIDTH), BF16),
                   jax.ShapeDtypeStruct((T, 2 * KV_WIDTH), BF16),
                   jax.ShapeDtypeStruct((T, 2 * KV_WIDTH), BF16),
                   jax.ShapeDtypeStruct((T, KV_WIDTH), BF16),
                   jax.ShapeDtypeStruct((T, 2 * KV_WIDTH), BF16)],
        compiler_params=_params("parallel"),
        name="qk_prep",
    )(proj_b, proj_b, proj_b, proj_b, proj_b, q_norm, k_slc_norm, k_win_norm,
      rope_c * q_scale, rope_s * q_scale, rope_c, rope_s, w)


def _compress_kernel(zk_ref, zv_ref, pek_ref, pev_ref, w1k_ref, w2k_ref, w1v_ref, w2v_ref,
                     kg_ref, c_ref, su_ref, sd_ref, kc_ref, vc_ref, zr_ref):
    half = CMP_STRIDE * HEAD_DIM
    nb = zr_ref.shape[0]

    def comp(z_ref, pe_ref, w1_ref, w2_ref):
        for l in range(CMP_STRIDE):
            zr_ref[:, l * HEAD_DIM:(l + 1) * HEAD_DIM] = z_ref[pl.ds(l, nb, stride=CMP_STRIDE), :]
        z = zr_ref[...]
        top = jnp.dot((z + pe_ref[:, :half]).astype(BF16), w1_ref[:half, :], preferred_element_type=F32)
        bot = jnp.dot((z + pe_ref[:, half:]).astype(BF16), w1_ref[half:, :], preferred_element_type=F32)
        hid = top + pltpu.roll(bot, nb - 1, 0)
        return jnp.dot(jax.nn.gelu(hid).astype(BF16), w2_ref[...], preferred_element_type=F32)

    k = comp(zk_ref, pek_ref, w1k_ref, w2k_ref)
    kc_ref[...] = _rope(_rms(k, kg_ref[...]), c_ref[...], su_ref[...], sd_ref[...]).astype(BF16)
    vc_ref[...] = comp(zv_ref, pev_ref, w1v_ref, w2v_ref).astype(BF16)


def _compress(proj_b, pek, pev, w1k, w2k, w1v, w2v, k_cmp_norm, cmp_c, cmp_su, cmp_sd, *, batch, seq):
    G = N_KV_GROUPS
    nb = seq // CMP_STRIDE
    kc_blk = B_WIDTH // HEAD_DIM
    vc_blk = kc_blk + G
    ospec = pl.BlockSpec((None, None, nb, HEAD_DIM), lambda b, g: (b, g, 0, 0))
    full = lambda a: pl.BlockSpec(a.shape, lambda b, g: (0,) * a.ndim)
    consts = (pek, pev, w1k, w2k, w1v, w2v, k_cmp_norm, cmp_c, cmp_su, cmp_sd)
    return pl.pallas_call(
        _compress_kernel,
        grid=(batch, G),
        in_specs=[pl.BlockSpec((seq, HEAD_DIM), lambda b, g: (b, kc_blk + g)),
                  pl.BlockSpec((seq, HEAD_DIM), lambda b, g: (b, vc_blk + g))]
                 + [full(a) for a in consts],
        out_specs=[ospec, ospec],
        out_shape=[jax.ShapeDtypeStruct((batch, G, nb, HEAD_DIM), BF16)] * 2,
        scratch_shapes=[pltpu.VMEM((nb, CMP_STRIDE * HEAD_DIM), F32)],
        compiler_params=_params("parallel", "parallel"),
        name="compress",
    )(proj_b, proj_b, *consts)


def _cmp_attn_kernel(q_ref, kc_ref, vc_ref, kw_ref, vw_ref, gate_ref, ovt_ref, oc_ref, bias_ref,
                     qs_ref, *, tq):
    hpg = HEADS_PER_GROUP
    M = hpg * tq
    nb = kc_ref.shape[0]
    n_sel = ovt_ref.shape[0]
    t0 = pl.program_id(2) * tq
    for h in range(hpg):
        qs_ref[h * tq:(h + 1) * tq, :] = q_ref[:, h * HEAD_DIM:(h + 1) * HEAD_DIM]

    wlen = WINDOW + tq
    kstart = pl.multiple_of(jnp.maximum(t0 - WINDOW, 0), tq)
    sw = lax.dot_general(qs_ref[...], kw_ref[pl.ds(kstart, wlen), :], NT_DIMS,
                         preferred_element_type=F32)
    tw = t0 + lax.broadcasted_iota(jnp.int32, (1, tq, wlen), 1)
    kp = kstart + lax.broadcasted_iota(jnp.int32, (1, tq, wlen), 2)
    keep = (tw - kp).astype(jnp.uint32) < WINDOW
    sw = jnp.where(keep, sw.reshape(hpg, tq, wlen), NEG).reshape(M, wlen)
    pw = jnp.exp2(sw - jnp.max(sw, axis=-1, keepdims=True))
    aw = jnp.dot(pw.astype(BF16), vw_ref[pl.ds(kstart, wlen), :], preferred_element_type=F32)
    o_w = aw[:, :HEAD_DIM] * (1.0 / aw[:, HEAD_DIM:])

    s = lax.dot_general(qs_ref[...], kc_ref[...], NT_DIMS, preferred_element_type=F32)
    s = s.reshape(hpg, tq, nb)
    t = t0 + lax.broadcasted_iota(jnp.int32, (1, tq, nb), 1)
    n = lax.broadcasted_iota(jnp.int32, (1, tq, nb), 2)
    mask = n * CMP_STRIDE + (CMP_BLOCK - 1) <= t
    s = jnp.where(mask, s, NEG)
    e = jnp.exp2(s - jnp.max(s, axis=-1, keepdims=True))
    inv = 1.0 / jnp.sum(e, axis=-1, keepdims=True)
    p = jnp.where(mask, e * inv, 0.0)
    o = jnp.dot(p.reshape(hpg * tq, nb).astype(BF16), vc_ref[...], preferred_element_type=F32)
    sig = jax.nn.sigmoid(gate_ref[...])
    for h in range(hpg):
        rows = slice(h * tq, (h + 1) * tq)
        oc_ref[:, h * HEAD_DIM:(h + 1) * HEAD_DIM] = (
            o[rows] * sig[:, h:h + 1]
            + o_w[rows] * sig[:, 2 * hpg + h:2 * hpg + h + 1]).astype(oc_ref.dtype)

    psum = jnp.sum(p, axis=0)
    ovt = ovt_ref[...]
    hi = psum.astype(BF16)
    r1 = psum - hi.astype(F32)
    mid = r1.astype(BF16)
    lo = (r1 - mid.astype(F32)).astype(BF16)
    imp = (lax.dot_general(ovt, hi, NT_DIMS, preferred_element_type=F32)
           + lax.dot_general(ovt, mid, NT_DIMS, preferred_element_type=F32)
           + lax.dot_general(ovt, lo, NT_DIMS, preferred_element_type=F32))
    j = lax.broadcasted_iota(jnp.int32, (n_sel, tq), 0)
    tt = t0 + lax.broadcasted_iota(jnp.int32, (n_sel, tq), 1)
    cur = tt // SEL_BLOCK
    forced = jnp.where((j == 0) | (j == cur) | (j == cur - 1), FORCE_BONUS, 0.0)
    score = jnp.where(j * SEL_BLOCK <= tt, imp + forced, NEG)
    jrow = j.astype(F32)
    below_neg = 3.0 * NEG
    chosen = jnp.zeros((n_sel, tq), F32)
    for _ in range(SEL_TOP):
        top = jnp.max(score, axis=0, keepdims=True)
        idx = jnp.min(jnp.where(score == top, jrow, float(n_sel)), axis=0, keepdims=True)
        hit = jrow == idx
        chosen = jnp.where(hit, 1.0, chosen)
        score = jnp.where(hit, below_neg, score)
    bias = jnp.where(chosen > 0.5, 0.0, NEG)
    bias = jnp.concatenate([bias, jnp.zeros((LANES - n_sel, tq), F32)], axis=0)
    bias_ref[...] = bias.T.astype(bias_ref.dtype)


def _cmp_attn(qn, kc, vc, kwn, vw, proj_b, ovt, *, batch, seq, tq=ATTN_Q_TILE):
    assert WINDOW % tq == 0
    T = qn.shape[0]
    G = N_KV_GROUPS
    nq = seq // tq
    gw = HEADS_PER_GROUP * HEAD_DIM
    nb = kc.shape[2]
    rowg = lambda b, g, i: (b * nq + i, g)
    seqg = lambda b, g, i: (b, g)
    return pl.pallas_call(
        functools.partial(_cmp_attn_kernel, tq=tq),
        grid=(batch, G, nq),
        in_specs=[
            pl.BlockSpec((tq, gw), rowg),
            pl.BlockSpec((None, None, nb, HEAD_DIM), lambda b, g, i: (b, g, 0, 0)),
            pl.BlockSpec((None, None, nb, HEAD_DIM), lambda b, g, i: (b, g, 0, 0)),
            pl.BlockSpec((seq, HEAD_DIM), seqg),
            pl.BlockSpec((seq, 2 * HEAD_DIM), seqg),
            pl.BlockSpec((tq, LANES), lambda b, g, i: (b * nq + i, HEAD_GATE_BLK + g)),
            pl.BlockSpec(ovt.shape, lambda b, g, i: (0, 0)),
        ],
        out_specs=[pl.BlockSpec((tq, gw), rowg), pl.BlockSpec((tq, LANES), rowg)],
        out_shape=[jax.ShapeDtypeStruct((T, B_WIDTH), BF16),
                   jax.ShapeDtypeStruct((T, G * LANES), BF16)],
        scratch_shapes=[pltpu.VMEM((HEADS_PER_GROUP * tq, HEAD_DIM), BF16)],
        compiler_params=_params("parallel", "parallel", "parallel"),
        name="cmp_win_attn",
    )(qn, kc, vc, kwn, vw, proj_b, ovt)


def _sw_attn_kernel(q_ref, bias_ref, ksa_ref, vs_ref, oc_ref, gate_ref, o_ref,
                    qs_ref, m_ref, acc_ref, ss0_ref, ss1_ref, *, tq, tks):
    hpg = HEADS_PER_GROUP
    M = hpg * tq
    ss_ref = (ss0_ref, ss1_ref)
    t0 = pl.program_id(2) * tq
    bias = bias_ref[...]
    for h in range(hpg):
        qs_ref[h * tq:(h + 1) * tq, :HEAD_DIM] = q_ref[:, h * HEAD_DIM:(h + 1) * HEAD_DIM]
        qs_ref[h * tq:(h + 1) * tq, HEAD_DIM:] = bias

    def reset():
        m_ref[...] = jnp.full_like(m_ref, NEG)
        acc_ref[...] = jnp.zeros_like(acc_ref)

    def step(s, v):
        m_old = m_ref[...]
        m_new = jnp.maximum(m_old, jnp.max(s, axis=-1, keepdims=True))
        alpha = jnp.exp2(m_old - m_new)
        p = jnp.exp2(s - jnp.tile(m_new, (1, s.shape[1] // LANES)))
        acc_ref[...] = (jnp.tile(alpha, (1, 2)) * acc_ref[...]
                        + jnp.dot(p.astype(BF16), v, preferred_element_type=F32))
        m_ref[...] = m_new

    def result():
        return acc_ref[:, :HEAD_DIM] * (1.0 / acc_ref[:, HEAD_DIM:])

    def masked(s, keep):
        tk = s.shape[1]
        return jnp.where(keep, s.reshape(hpg, tq, tk), NEG).reshape(M, tk)

    def positions(tk):
        t = t0 + lax.broadcasted_iota(jnp.int32, (1, tq, tk), 1)
        kpos = lax.broadcasted_iota(jnp.int32, (1, tq, tk), 2)
        return t, kpos

    def sel_base(j):
        return pl.multiple_of(j * tks, tks)

    def sel_scores(j):
        base = sel_base(j)
        k = ksa_ref[pl.ds(base, tks), :]
        s = lax.dot_general(qs_ref[...], k, NT_DIMS, preferred_element_type=F32)
        t, kpos = positions(tks)
        return masked(s, kpos + base <= t)

    sel_last = (t0 + tq - 1) // tks

    def by_parity(n, fn):
        @pl.when(n % 2 == 0)
        def _():
            fn(0, 1)

        @pl.when(n % 2 == 1)
        def _():
            fn(1, 0)

    reset()
    ss_ref[0][...] = sel_scores(0)

    def sel_body(j, carry):
        def run(cur, nxt):
            ss_ref[nxt][...] = sel_scores(j + 1)
            step(ss_ref[cur][...], vs_ref[pl.ds(sel_base(j), tks), :])

        by_parity(j, run)
        return carry

    lax.fori_loop(0, sel_last, sel_body, 0)
    by_parity(sel_last,
              lambda cur, nxt: step(ss_ref[cur][...], vs_ref[pl.ds(sel_base(sel_last), tks), :]))
    o_s = result()

    sig = jax.nn.sigmoid(gate_ref[...])
    for h in range(hpg):
        rows = slice(h * tq, (h + 1) * tq)
        cols = slice(h * HEAD_DIM, (h + 1) * HEAD_DIM)
        o = oc_ref[:, cols].astype(F32) + sig[:, hpg + h:hpg + h + 1] * o_s[rows]
        o_ref[:, cols] = o.astype(o_ref.dtype)


def _sw_attn(qn, bias, ksa, vs, o_c, proj_b, *, batch, seq, tq=ATTN_Q_TILE,
             tks=SEL_KEY_TILE):
    T = qn.shape[0]
    G = N_KV_GROUPS
    nq = seq // tq
    gw = HEADS_PER_GROUP * HEAD_DIM
    M = HEADS_PER_GROUP * tq
    rowg = lambda b, g, i: (b * nq + i, g)
    seqg = lambda b, g, i: (b, g)
    return pl.pallas_call(
        functools.partial(_sw_attn_kernel, tq=tq, tks=tks),
        grid=(batch, G, nq),
        in_specs=[
            pl.BlockSpec((tq, gw), rowg),
            pl.BlockSpec((tq, LANES), rowg),
            pl.BlockSpec((seq, 2 * HEAD_DIM), seqg),
            pl.BlockSpec((seq, 2 * HEAD_DIM), seqg),
            pl.BlockSpec((tq, gw), rowg),
            pl.BlockSpec((tq, LANES), lambda b, g, i: (b * nq + i, HEAD_GATE_BLK + g)),
        ],
        out_specs=pl.BlockSpec((tq, gw), rowg),
        out_shape=jax.ShapeDtypeStruct((T, B_WIDTH), BF16),
        scratch_shapes=[pltpu.VMEM((M, 2 * HEAD_DIM), BF16), pltpu.VMEM((M, LANES), F32),
                        pltpu.VMEM((M, 2 * HEAD_DIM), F32),
                        pltpu.VMEM((M, tks), F32), pltpu.VMEM((M, tks), F32)],
        compiler_params=_params("parallel", "parallel", "parallel"),
        name="sel_attn",
    )(qn, bias, ksa, vs, o_c, proj_b)


def _merge_kernel(oa_ref, ob_ref, ga_ref, gb_ref, wa_ref, wb_ref, o_ref):
    a = jnp.dot(oa_ref[...], wa_ref[...], preferred_element_type=F32)
    b = jnp.dot(ob_ref[...], wb_ref[...], preferred_element_type=F32)
    ga = jax.nn.sigmoid(ga_ref[...].astype(F32))
    gb = jax.nn.sigmoid(gb_ref[...].astype(F32))
    o_ref[...] = (ga * a + gb * b).astype(o_ref.dtype)


def _merge(o_a, o_b, proj_a, wa, wb, *, tm=ROW_TILE, tn=MERGE_TILE):
    T, D = o_a.shape
    N = wa.shape[1]
    nn = N // tn
    ga_blk = 2 * A_WIDTH // tn
    return pl.pallas_call(
        _merge_kernel,
        grid=(T // tm, nn),
        in_specs=[
            pl.BlockSpec((tm, D), lambda i, j: (i, 0)),
            pl.BlockSpec((tm, D), lambda i, j: (i, 0)),
            pl.BlockSpec((tm, tn), lambda i, j: (i, ga_blk + j)),
            pl.BlockSpec((tm, tn), lambda i, j: (i, ga_blk + nn + j)),
            pl.BlockSpec((D, tn), lambda i, j: (0, j)),
            pl.BlockSpec((D, tn), lambda i, j: (0, j)),
        ],
        out_specs=pl.BlockSpec((tm, tn), lambda i, j: (i, j)),
        out_shape=jax.ShapeDtypeStruct((T, N), BF16),
        compiler_params=_params("parallel", "arbitrary"),
        name="merge",
    )(o_a, o_b, proj_a, proj_a, wa, wb)


def _out_kernel(x_ref, m_ref, w_ref, o_ref):
    o_ref[...] = x_ref[...] + jnp.dot(m_ref[...], w_ref[...], preferred_element_type=F32)


def _out_proj(x, merged, w, *, tm=ROW_TILE, tn=MERGE_TILE):
    T, D = merged.shape
    N = w.shape[1]
    return pl.pallas_call(
        _out_kernel,
        grid=(T // tm, N // tn),
        in_specs=[
            pl.BlockSpec((tm, tn), lambda i, j: (i, j)),
            pl.BlockSpec((tm, D), lambda i, j: (i, 0)),
            pl.BlockSpec((D, tn), lambda i, j: (0, j)),
        ],
        out_specs=pl.BlockSpec((tm, tn), lambda i, j: (i, j)),
        out_shape=jax.ShapeDtypeStruct((T, N), F32),
        compiler_params=_params("parallel", "arbitrary"),
        name="out_proj",
    )(x, merged, w)


def _rope_tables(pos):
    half = ROPE_DIM // 2
    inv = ROPE_THETA ** (-2.0 * jnp.arange(half, dtype=F32) / ROPE_DIM)
    ang = pos.astype(F32)[:, None] * inv
    cos, sin = jnp.cos(ang), jnp.sin(ang)
    n = pos.shape[0]
    z16 = jnp.zeros((n, half), F32)
    rest = HEAD_DIM - ROPE_DIM
    c = jnp.concatenate([cos, cos, jnp.ones((n, rest), F32)], axis=1)
    su = jnp.concatenate([z16, sin, jnp.zeros((n, rest), F32)], axis=1)
    sd = jnp.concatenate([-sin, z16, jnp.zeros((n, rest), F32)], axis=1)
    return c, su, sd


def _split_w_in(w_in):
    sizes = [A_WIDTH, A_WIDTH, B_WIDTH] + [KV_WIDTH] * 6 + [3 * N_HEADS, D_MODEL, D_MODEL]
    offs = np.concatenate([[0], np.cumsum(sizes)])
    w = w_in.astype(BF16)
    w_a = jnp.concatenate([w[:, :offs[2]], w[:, offs[10]:]], axis=1)
    gates = w[:, offs[9]:offs[10]]
    blocks = []
    for g in range(N_KV_GROUPS):
        cols = [gates[:, br * N_HEADS + g * HEADS_PER_GROUP:br * N_HEADS + (g + 1) * HEADS_PER_GROUP]
                for br in range(3)]
        pad = jnp.zeros((w.shape[0], LANES - 3 * HEADS_PER_GROUP), w.dtype)
        blocks.extend(cols + [pad])
    w_b = jnp.concatenate([w[:, offs[2]:offs[9]]] + blocks, axis=1)
    return w_a, w_b


def _overlap_t(nb_pad, n_sel):
    cmp_start = np.arange(nb_pad) * CMP_STRIDE
    sel_start = np.arange(n_sel) * SEL_BLOCK
    ov = ((cmp_start[None, :] < sel_start[:, None] + SEL_BLOCK)
          & (cmp_start[None, :] + CMP_BLOCK > sel_start[:, None]))
    return jnp.asarray(ov, dtype=BF16)


def _layer(x, ffn1_norm, ffn1_w_gate, ffn1_w_up, ffn1_w_down, mix_norm, w_in,
           a_v_norm, a_w_s, a_b_s, q_norm, k_cmp_norm, k_slc_norm, k_win_norm,
           cmp_k_pe, cmp_k_w1, cmp_k_w2, cmp_v_pe, cmp_v_w1, cmp_v_w2,
           w_branch_a, w_branch_b, w_out, ffn2_norm, ffn2_w_gate, ffn2_w_up, ffn2_w_down):
    B, S, D = x.shape
    T = B * S
    G = N_KV_GROUPS
    row = lambda v: v.reshape(1, -1)
    x0 = x.reshape(T, D)

    x1 = _ffn(x0, row(ffn1_norm), ffn1_w_gate, ffn1_w_up, ffn1_w_down)

    w_a, w_b = _split_w_in(w_in)
    proj_a = _proj(x1, row(mix_norm), w_a, BF16, tm=ROW_TILE, tn=PROJ_A_TILE)
    proj_b = _proj(x1, row(mix_norm), w_b, F32, tm=ROW_TILE, tn=PROJ_B_TILE)

    o_a = _gmlp(proj_a, row(a_v_norm), a_w_s, a_b_s.T)

    pos = jnp.arange(S)
    qn, ksa, vs, kwn, vw = _prep(proj_b, row(q_norm), row(k_slc_norm), row(k_win_norm),
                                 *_rope_tables(pos), seq=S)

    nb_pad = S // CMP_STRIDE
    cmp_end = jnp.arange(nb_pad) * CMP_STRIDE + (CMP_BLOCK - 1)
    kc, vc = _compress(proj_b, cmp_k_pe.reshape(1, -1), cmp_v_pe.reshape(1, -1),
                       cmp_k_w1.astype(BF16), cmp_k_w2.astype(BF16),
                       cmp_v_w1.astype(BF16), cmp_v_w2.astype(BF16),
                       row(k_cmp_norm), *_rope_tables(cmp_end), batch=B, seq=S)

    o_cw, bias = _cmp_attn(qn, kc, vc, kwn, vw, proj_b, _overlap_t(nb_pad, S // SEL_BLOCK),
                           batch=B, seq=S)
    o_b = _sw_attn(qn, bias, ksa, vs, o_cw, proj_b, batch=B, seq=S)

    merged = _merge(o_a, o_b, proj_a, w_branch_a.astype(BF16), w_branch_b.astype(BF16))
    x2 = _out_proj(x1, merged, w_out.astype(BF16))

    x3 = _ffn(x2, row(ffn2_norm), ffn2_w_gate, ffn2_w_up, ffn2_w_down)
    return x3.reshape(B, S, D)


def kernel(x, ffn1_norm, ffn1_w_gate, ffn1_w_up, ffn1_w_down, mix_norm, w_in, a_v_norm, a_w_s, a_b_s, q_norm, k_cmp_norm, k_slc_norm, k_win_norm, cmp_k_pe, cmp_k_w1, cmp_k_w2, cmp_v_pe, cmp_v_w1, cmp_v_w2, w_branch_a, w_branch_b, w_out, ffn2_norm, ffn2_w_gate, ffn2_w_up, ffn2_w_down):
    params = (ffn1_norm, ffn1_w_gate, ffn1_w_up, ffn1_w_down, mix_norm, w_in,
              a_v_norm, a_w_s, a_b_s, q_norm, k_cmp_norm, k_slc_norm, k_win_norm,
              cmp_k_pe, cmp_k_w1, cmp_k_w2, cmp_v_pe, cmp_v_w1, cmp_v_w2,
              w_branch_a, w_branch_b, w_out, ffn2_norm, ffn2_w_gate, ffn2_w_up, ffn2_w_down)
    for l in range(params[0].shape[0]):
        x = _layer(x, *[p[l] for p in params])
    return x
```

```python
import functools

import numpy as np
import jax
import jax.numpy as jnp
from jax import lax
from jax.experimental import pallas as pl
from jax.experimental.pallas import tpu as pltpu

D_MODEL = 2048
D_FF = 5504
A_WIDTH = 2048
A_GROUPS = 8
A_CHUNK = 128
N_HEADS = 16
HEAD_DIM = 128
N_KV_GROUPS = 2
HEADS_PER_GROUP = N_HEADS // N_KV_GROUPS
CMP_BLOCK = 32
CMP_STRIDE = 16
CMP_HIDDEN = 256
SEL_BLOCK = 64
SEL_TOP = 16
WINDOW = 512
ROPE_THETA = 500000.0
ROPE_DIM = HEAD_DIM // 4
EPS = 1e-6
NEG = -1e30
FORCE_BONUS = 1e4
LOG2_E = 1.4426950408889634
B_WIDTH = N_HEADS * HEAD_DIM
KV_WIDTH = N_KV_GROUPS * HEAD_DIM

LANES = 128
HEAD_GATE_BLK = (B_WIDTH + 6 * KV_WIDTH) // LANES
VMEM_LIMIT = 56 * 1024 * 1024

ROW_TILE = 1024
FFN_HIDDEN_TILE = 256
PROJ_A_TILE = 2048
PROJ_B_TILE = 1920
PROJ_B_COLS = B_WIDTH + 6 * KV_WIDTH + N_KV_GROUPS * LANES
PROJ_A_COL0 = -(-PROJ_B_COLS // PROJ_A_TILE) * PROJ_A_TILE
MERGE_TILE = 1024
POINTWISE_ROWS = 512
ATTN_Q_TILE = 256
SEL_KEY_TILE = 512

F32 = jnp.float32
BF16 = jnp.bfloat16
NT_DIMS = (((1,), (1,)), ((), ()))


def _params(*sem):
    return pltpu.CompilerParams(dimension_semantics=sem, vmem_limit_bytes=VMEM_LIMIT)


def _rms(x, g):
    ms = jnp.mean(x * x, axis=-1, keepdims=True)
    return x * lax.rsqrt(ms + EPS) * g


def _ffn_kernel(x_ref, g_ref, wg_ref, wu_ref, wd_ref, o_ref, h_ref, *, nf, tf, f_last):
    j = pl.program_id(1)

    def partial(width):
        h = h_ref[...]
        a = jnp.dot(h, wg_ref[:, :width].astype(BF16), preferred_element_type=F32)
        b = jnp.dot(h, wu_ref[:, :width].astype(BF16), preferred_element_type=F32)
        act = (a * jax.nn.sigmoid(a) * b).astype(BF16)
        return jnp.dot(act, wd_ref[:width, :].astype(BF16), preferred_element_type=F32)

    @pl.when(j == 0)
    def _():
        h_ref[...] = _rms(x_ref[...], g_ref[...]).astype(BF16)
        o_ref[...] = partial(f_last)

    @pl.when((j > 0) & (j < nf - 1))
    def _():
        o_ref[...] += partial(tf)

    @pl.when(j == nf - 1)
    def _():
        o_ref[...] = x_ref[...] + 0.5 * (o_ref[...] + partial(tf))


def _ffn(x, norm, wg, wu, wd, *, tm=ROW_TILE, tf=FFN_HIDDEN_TILE):
    T, D = x.shape
    F = wg.shape[1]
    nf = pl.cdiv(F, tf)
    f_last = F - (nf - 1) * tf
    hid = lambda j: (j + nf - 1) % nf
    return pl.pallas_call(
        functools.partial(_ffn_kernel, nf=nf, tf=tf, f_last=f_last),
        grid=(T // tm, nf),
        in_specs=[
            pl.BlockSpec((tm, D), lambda i, j: (i, 0)),
            pl.BlockSpec((1, D), lambda i, j: (0, 0)),
            pl.BlockSpec((D, tf), lambda i, j: (0, hid(j))),
            pl.BlockSpec((D, tf), lambda i, j: (0, hid(j))),
            pl.BlockSpec((tf, D), lambda i, j: (hid(j), 0)),
        ],
        out_specs=pl.BlockSpec((tm, D), lambda i, j: (i, 0)),
        out_shape=jax.ShapeDtypeStruct((T, D), F32),
        scratch_shapes=[pltpu.VMEM((tm, D), BF16)],
        compiler_params=_params("parallel", "arbitrary"),
        name="ffn",
    )(x, norm, wg, wu, wd)


def _proj_kernel(x_ref, g_ref, w_ref, o_ref, h_ref):
    @pl.when(pl.program_id(1) == 0)
    def _():
        h_ref[...] = _rms(x_ref[...], g_ref[...]).astype(BF16)

    o_ref[...] = jnp.dot(h_ref[...], w_ref[...], preferred_element_type=F32).astype(o_ref.dtype)


def _proj(x, norm, w, out_dtype, *, tm, tn, col0, ncols):
    T, D = x.shape
    blk0 = col0 // tn
    assert blk0 * tn == col0 and ncols % tn == 0
    return pl.pallas_call(
        _proj_kernel,
        grid=(T // tm, ncols // tn),
        in_specs=[
            pl.BlockSpec((tm, D), lambda i, j: (i, 0)),
            pl.BlockSpec((1, D), lambda i, j: (0, 0)),
            pl.BlockSpec((D, tn), lambda i, j: (0, blk0 + j)),
        ],
        out_specs=pl.BlockSpec((tm, tn), lambda i, j: (i, j)),
        out_shape=jax.ShapeDtypeStruct((T, ncols), out_dtype),
        scratch_shapes=[pltpu.VMEM((tm, D), BF16)],
        compiler_params=_params("parallel", "arbitrary"),
        name="proj",
    )(x, norm, w)


def _gmlp_kernel(u_ref, v_ref, vn_ref, ws_ref, bs_ref, o_ref, *, tr):
    gw = A_WIDTH // A_GROUPS
    v = jax.nn.gelu(v_ref[...].astype(F32))
    vn = _rms(v, vn_ref[...]).astype(BF16)
    row = lax.broadcasted_iota(jnp.int32, (A_CHUNK, A_CHUNK), 0)
    col = lax.broadcasted_iota(jnp.int32, (A_CHUNK, A_CHUNK), 1)
    for g in range(A_GROUPS):
        w = jnp.where(col <= row, ws_ref[g], 0.0).astype(BF16)
        bias = bs_ref[:, g:g + 1]
        for c in range(tr // A_CHUNK):
            rows = slice(c * A_CHUNK, (c + 1) * A_CHUNK)
            cols = slice(g * gw, (g + 1) * gw)
            mixed = jnp.dot(w, vn[rows, cols], preferred_element_type=F32) + bias
            u = jax.nn.gelu(u_ref[rows, cols].astype(F32))
            o_ref[rows, cols] = (u * mixed).astype(o_ref.dtype)


def _gmlp(proj_a, a_v_norm, w_s, b_s_t, *, tr=POINTWISE_ROWS):
    T = proj_a.shape[0]
    return pl.pallas_call(
        functools.partial(_gmlp_kernel, tr=tr),
        grid=(T // tr,),
        in_specs=[
            pl.BlockSpec((tr, A_WIDTH), lambda i: (i, 0)),
            pl.BlockSpec((tr, A_WIDTH), lambda i: (i, 1)),
            pl.BlockSpec((1, A_WIDTH), lambda i: (0, 0)),
            pl.BlockSpec((A_GROUPS, A_CHUNK, A_CHUNK), lambda i: (0, 0, 0)),
            pl.BlockSpec((A_CHUNK, A_GROUPS), lambda i: (0, 0)),
        ],
        out_specs=pl.BlockSpec((tr, A_WIDTH), lambda i: (i, 0)),
        out_shape=jax.ShapeDtypeStruct((T, A_WIDTH), BF16),
        compiler_params=_params("parallel"),
        name="gmlp",
    )(proj_a, proj_a, a_v_norm, w_s, b_s_t)


def _rope(y, c, su, sd):
    half = ROPE_DIM // 2
    return y * c + pltpu.roll(y, half, 1) * su + pltpu.roll(y, HEAD_DIM - half, 1) * sd


def _norm_rope_mxu(x, g, c, s, w):
    xg = x * g
    lhs = jnp.concatenate([(x * x).astype(BF16), xg.astype(BF16)], axis=1)
    res = jnp.dot(lhs, w, preferred_element_type=F32)
    r = lax.rsqrt(res[:, :HEAD_DIM] * (1.0 / HEAD_DIM) + EPS)
    return (xg * c + res[:, HEAD_DIM:] * s) * r


def _prep_kernel(q_ref, ks_ref, vs_ref, kw_ref, vw_ref, qg_ref, ksg_ref, kwg_ref,
                 cq_ref, sq_ref, ck_ref, sk_ref, w_ref,
                 qn_ref, ksa_ref, vso_ref, kwn_ref, vwo_ref, *, tr, seq):
    cq, sq, ck, sk, w = cq_ref[...], sq_ref[...], ck_ref[...], sk_ref[...], w_ref[...]
    for h in range(N_HEADS):
        cols = slice(h * HEAD_DIM, (h + 1) * HEAD_DIM)
        qn_ref[:, cols] = _norm_rope_mxu(q_ref[:, cols], qg_ref[...], cq, sq, w).astype(BF16)
    t0 = (pl.program_id(0) * tr) % seq
    t = t0 + lax.broadcasted_iota(jnp.int32, (tr, LANES), 0)
    lane = lax.broadcasted_iota(jnp.int32, (tr, LANES), 1)
    onehot = jnp.where((t // SEL_BLOCK) == lane, 1.0, 0.0).astype(BF16)
    ones = jnp.ones((tr, HEAD_DIM), BF16)
    for g in range(N_KV_GROUPS):
        cols = slice(g * HEAD_DIM, (g + 1) * HEAD_DIM)
        lo = slice(2 * g * HEAD_DIM, (2 * g + 1) * HEAD_DIM)
        hi = slice((2 * g + 1) * HEAD_DIM, (2 * g + 2) * HEAD_DIM)
        ksa_ref[:, lo] = _norm_rope_mxu(ks_ref[:, cols], ksg_ref[...], ck, sk, w).astype(BF16)
        ksa_ref[:, hi] = onehot
        kwn_ref[:, cols] = _norm_rope_mxu(kw_ref[:, cols], kwg_ref[...], ck, sk, w).astype(BF16)
        vso_ref[:, lo] = vs_ref[:, cols].astype(BF16)
        vso_ref[:, hi] = ones
        vwo_ref[:, lo] = vw_ref[:, cols].astype(BF16)
        vwo_ref[:, hi] = ones


def _rope_matrix():
    half = ROPE_DIM // 2
    w = np.zeros((2 * HEAD_DIM, 2 * HEAD_DIM), np.float32)
    w[:HEAD_DIM, :HEAD_DIM] = 1.0
    for j in range(half):
        w[HEAD_DIM + j + half, HEAD_DIM + j] = 1.0
        w[HEAD_DIM + j, HEAD_DIM + j + half] = 1.0
    return jnp.asarray(w, dtype=BF16)


def _prep(proj_b, q_norm, k_slc_norm, k_win_norm, rope_c, rope_su, rope_sd, *, seq,
          tr=POINTWISE_ROWS):
    T = proj_b.shape[0]
    kvb = B_WIDTH // KV_WIDTH
    nrt = seq // tr
    q_scale = HEAD_DIM ** -0.5 * LOG2_E
    rope_s = rope_su + rope_sd
    w = _rope_matrix()
    row = lambda i: (i, 0)
    tab = pl.BlockSpec((tr, HEAD_DIM), lambda i: (i % nrt, 0))
    gain = pl.BlockSpec((1, HEAD_DIM), lambda i: (0, 0))
    kv_in = lambda k: pl.BlockSpec((tr, KV_WIDTH), lambda i: (i, kvb + k))
    return pl.pallas_call(
        functools.partial(_prep_kernel, tr=tr, seq=seq),
        grid=(T // tr,),
        in_specs=[pl.BlockSpec((tr, B_WIDTH), row), kv_in(2), kv_in(3), kv_in(4), kv_in(5),
                  gain, gain, gain, tab, tab, tab, tab, pl.BlockSpec(w.shape, lambda i: (0, 0))],
        out_specs=[pl.BlockSpec((tr, B_WIDTH), row), pl.BlockSpec((tr, 2 * KV_WIDTH), row),
                   pl.BlockSpec((tr, 2 * KV_WIDTH), row), pl.BlockSpec((tr, KV_WIDTH), row),
                   pl.BlockSpec((tr, 2 * KV_WIDTH), row)],
        out_shape=[jax.ShapeDtypeStruct((T, B_WIDTH), BF16),
                   jax.ShapeDtypeStruct((T, 2 * KV_WIDTH), BF16),
                   jax.ShapeDtypeStruct((T, 2 * KV_WIDTH), BF16),
                   jax.ShapeDtypeStruct((T, KV_WIDTH), BF16),
                   jax.ShapeDtypeStruct((T, 2 * KV_WIDTH), BF16)],
        compiler_params=_params("parallel"),
        name="qk_prep",
    )(proj_b, proj_b, proj_b, proj_b, proj_b, q_norm, k_slc_norm, k_win_norm,
      rope_c * q_scale, rope_s * q_scale, rope_c, rope_s, w)


def _compress_kernel(zk_ref, zv_ref, pek_ref, pev_ref, w1k_ref, w2k_ref, w1v_ref, w2v_ref,
                     kg_ref, c_ref, su_ref, sd_ref, kc_ref, vc_ref, zr_ref):
    half = CMP_STRIDE * HEAD_DIM
    nb = zr_ref.shape[0]

    def comp(z_ref, pe_ref, w1_ref, w2_ref):
        for l in range(CMP_STRIDE):
            zr_ref[:, l * HEAD_DIM:(l + 1) * HEAD_DIM] = z_ref[pl.ds(l, nb, stride=CMP_STRIDE), :]
        z = zr_ref[...]
        top = jnp.dot((z + pe_ref[:, :half]).astype(BF16), w1_ref[:half, :], preferred_element_type=F32)
        bot = jnp.dot((z + pe_ref[:, half:]).astype(BF16), w1_ref[half:, :], preferred_element_type=F32)
        hid = top + pltpu.roll(bot, nb - 1, 0)
        return jnp.dot(jax.nn.gelu(hid).astype(BF16), w2_ref[...], preferred_element_type=F32)

    k = comp(zk_ref, pek_ref, w1k_ref, w2k_ref)
    kc_ref[...] = _rope(_rms(k, kg_ref[...]), c_ref[...], su_ref[...], sd_ref[...]).astype(BF16)
    vc_ref[...] = comp(zv_ref, pev_ref, w1v_ref, w2v_ref).astype(BF16)


def _compress(proj_b, pek, pev, w1k, w2k, w1v, w2v, k_cmp_norm, cmp_c, cmp_su, cmp_sd, *, batch, seq):
    G = N_KV_GROUPS
    nb = seq // CMP_STRIDE
    kc_blk = B_WIDTH // HEAD_DIM
    vc_blk = kc_blk + G
    ospec = pl.BlockSpec((None, None, nb, HEAD_DIM), lambda b, g: (b, g, 0, 0))
    full = lambda a: pl.BlockSpec(a.shape, lambda b, g: (0,) * a.ndim)
    consts = (pek, pev, w1k, w2k, w1v, w2v, k_cmp_norm, cmp_c, cmp_su, cmp_sd)
    return pl.pallas_call(
        _compress_kernel,
        grid=(batch, G),
        in_specs=[pl.BlockSpec((seq, HEAD_DIM), lambda b, g: (b, kc_blk + g)),
                  pl.BlockSpec((seq, HEAD_DIM), lambda b, g: (b, vc_blk + g))]
                 + [full(a) for a in consts],
        out_specs=[ospec, ospec],
        out_shape=[jax.ShapeDtypeStruct((batch, G, nb, HEAD_DIM), BF16)] * 2,
        scratch_shapes=[pltpu.VMEM((nb, CMP_STRIDE * HEAD_DIM), F32)],
        compiler_params=_params("parallel", "parallel"),
        name="compress",
    )(proj_b, proj_b, *consts)


def _cmp_attn_kernel(q_ref, kc_ref, vc_ref, kw_ref, vw_ref, gate_ref, ovt_ref, oc_ref, bias_ref,
                     qs_ref, *, tq):
    hpg = HEADS_PER_GROUP
    M = hpg * tq
    nb = kc_ref.shape[0]
    n_sel = ovt_ref.shape[0]
    t0 = pl.program_id(2) * tq
    for h in range(hpg):
        qs_ref[h * tq:(h + 1) * tq, :] = q_ref[:, h * HEAD_DIM:(h + 1) * HEAD_DIM]

    wlen = WINDOW + tq
    kstart = pl.multiple_of(jnp.maximum(t0 - WINDOW, 0), tq)
    sw = lax.dot_general(qs_ref[...], kw_ref[pl.ds(kstart, wlen), :], NT_DIMS,
                         preferred_element_type=F32)
    tw = t0 + lax.broadcasted_iota(jnp.int32, (1, tq, wlen), 1)
    kp = kstart + lax.broadcasted_iota(jnp.int32, (1, tq, wlen), 2)
    keep = (tw - kp).astype(jnp.uint32) < WINDOW
    sw = jnp.where(keep, sw.reshape(hpg, tq, wlen), NEG).reshape(M, wlen)
    pw = jnp.exp2(sw - jnp.max(sw, axis=-1, keepdims=True))
    aw = jnp.dot(pw.astype(BF16), vw_ref[pl.ds(kstart, wlen), :], preferred_element_type=F32)
    o_w = aw[:, :HEAD_DIM] * (1.0 / aw[:, HEAD_DIM:])

    s = lax.dot_general(qs_ref[...], kc_ref[...], NT_DIMS, preferred_element_type=F32)
    s = s.reshape(hpg, tq, nb)
    t = t0 + lax.broadcasted_iota(jnp.int32, (1, tq, nb), 1)
    n = lax.broadcasted_iota(jnp.int32, (1, tq, nb), 2)
    mask = n * CMP_STRIDE + (CMP_BLOCK - 1) <= t
    s = jnp.where(mask, s, NEG)
    e = jnp.exp2(s - jnp.max(s, axis=-1, keepdims=True))
    inv = 1.0 / jnp.sum(e, axis=-1, keepdims=True)
    p = jnp.where(mask, e * inv, 0.0)
    o = jnp.dot(p.reshape(hpg * tq, nb).astype(BF16), vc_ref[...], preferred_element_type=F32)
    sig = jax.nn.sigmoid(gate_ref[...])
    for h in range(hpg):
        rows = slice(h * tq, (h + 1) * tq)
        oc_ref[:, h * HEAD_DIM:(h + 1) * HEAD_DIM] = (
            o[rows] * sig[:, h:h + 1]
            + o_w[rows] * sig[:, 2 * hpg + h:2 * hpg + h + 1]).astype(oc_ref.dtype)

    psum = jnp.sum(p, axis=0)
    ovt = ovt_ref[...]
    hi = psum.astype(BF16)
    r1 = psum - hi.astype(F32)
    mid = r1.astype(BF16)
    lo = (r1 - mid.astype(F32)).astype(BF16)
    imp = (lax.dot_general(ovt, hi, NT_DIMS, preferred_element_type=F32)
           + lax.dot_general(ovt, mid, NT_DIMS, preferred_element_type=F32)
           + lax.dot_general(ovt, lo, NT_DIMS, preferred_element_type=F32))
    j = lax.broadcasted_iota(jnp.int32, (n_sel, tq), 0)
    tt = t0 + lax.broadcasted_iota(jnp.int32, (n_sel, tq), 1)
    cur = tt // SEL_BLOCK
    forced = jnp.where((j == 0) | (j == cur) | (j == cur - 1), FORCE_BONUS, 0.0)
    score = jnp.where(j * SEL_BLOCK <= tt, imp + forced, NEG)
    jrow = j.astype(F32)
    below_neg = 3.0 * NEG
    chosen = jnp.zeros((n_sel, tq), F32)
    for _ in range(SEL_TOP):
        top = jnp.max(score, axis=0, keepdims=True)
        idx = jnp.min(jnp.where(score == top, jrow, float(n_sel)), axis=0, keepdims=True)
        hit = jrow == idx
        chosen = jnp.where(hit, 1.0, chosen)
        score = jnp.where(hit, below_neg, score)
    bias = jnp.where(chosen > 0.5, 0.0, NEG)
    bias = jnp.concatenate([bias, jnp.zeros((LANES - n_sel, tq), F32)], axis=0)
    bias_ref[...] = bias.T.astype(bias_ref.dtype)


def _cmp_attn(qn, kc, vc, kwn, vw, proj_b, ovt, *, batch, seq, tq=ATTN_Q_TILE):
    assert WINDOW % tq == 0
    T = qn.shape[0]
    G = N_KV_GROUPS
    nq = seq // tq
    gw = HEADS_PER_GROUP * HEAD_DIM
    nb = kc.shape[2]
    rowg = lambda b, g, i: (b * nq + i, g)
    seqg = lambda b, g, i: (b, g)
    return pl.pallas_call(
        functools.partial(_cmp_attn_kernel, tq=tq),
        grid=(batch, G, nq),
        in_specs=[
            pl.BlockSpec((tq, gw), rowg),
            pl.BlockSpec((None, None, nb, HEAD_DIM), lambda b, g, i: (b, g, 0, 0)),
            pl.BlockSpec((None, None, nb, HEAD_DIM), lambda b, g, i: (b, g, 0, 0)),
            pl.BlockSpec((seq, HEAD_DIM), seqg),
            pl.BlockSpec((seq, 2 * HEAD_DIM), seqg),
            pl.BlockSpec((tq, LANES), lambda b, g, i: (b * nq + i, HEAD_GATE_BLK + g)),
            pl.BlockSpec(ovt.shape, lambda b, g, i: (0, 0)),
        ],
        out_specs=[pl.BlockSpec((tq, gw), rowg), pl.BlockSpec((tq, LANES), rowg)],
        out_shape=[jax.ShapeDtypeStruct((T, B_WIDTH), BF16),
                   jax.ShapeDtypeStruct((T, G * LANES), BF16)],
        scratch_shapes=[pltpu.VMEM((HEADS_PER_GROUP * tq, HEAD_DIM), BF16)],
        compiler_params=_params("parallel", "parallel", "parallel"),
        name="cmp_win_attn",
    )(qn, kc, vc, kwn, vw, proj_b, ovt)


def _sw_attn_kernel(q_ref, bias_ref, ksa_ref, vs_ref, oc_ref, gate_ref, o_ref,
                    qs_ref, m_ref, acc_ref, ss0_ref, ss1_ref, *, tq, tks):
    hpg = HEADS_PER_GROUP
    M = hpg * tq
    ss_ref = (ss0_ref, ss1_ref)
    t0 = pl.program_id(2) * tq
    bias = bias_ref[...]
    for h in range(hpg):
        qs_ref[h * tq:(h + 1) * tq, :HEAD_DIM] = q_ref[:, h * HEAD_DIM:(h + 1) * HEAD_DIM]
        qs_ref[h * tq:(h + 1) * tq, HEAD_DIM:] = bias

    def reset():
        m_ref[...] = jnp.full_like(m_ref, NEG)
        acc_ref[...] = jnp.zeros_like(acc_ref)

    def step(s, v):
        m_old = m_ref[...]
        m_new = jnp.maximum(m_old, jnp.max(s, axis=-1, keepdims=True))
        alpha = jnp.exp2(m_old - m_new)
        p = jnp.exp2(s - jnp.tile(m_new, (1, s.shape[1] // LANES)))
        acc_ref[...] = (jnp.tile(alpha, (1, 2)) * acc_ref[...]
                        + jnp.dot(p.astype(BF16), v, preferred_element_type=F32))
        m_ref[...] = m_new

    def result():
        return acc_ref[:, :HEAD_DIM] * (1.0 / acc_ref[:, HEAD_DIM:])

    def masked(s, keep):
        tk = s.shape[1]
        return jnp.where(keep, s.reshape(hpg, tq, tk), NEG).reshape(M, tk)

    def positions(tk):
        t = t0 + lax.broadcasted_iota(jnp.int32, (1, tq, tk), 1)
        kpos = lax.broadcasted_iota(jnp.int32, (1, tq, tk), 2)
        return t, kpos

    def sel_base(j):
        return pl.multiple_of(j * tks, tks)

    def sel_scores(j):
        base = sel_base(j)
        k = ksa_ref[pl.ds(base, tks), :]
        s = lax.dot_general(qs_ref[...], k, NT_DIMS, preferred_element_type=F32)
        t, kpos = positions(tks)
        return masked(s, kpos + base <= t)

    sel_last = (t0 + tq - 1) // tks

    def by_parity(n, fn):
        @pl.when(n % 2 == 0)
        def _():
            fn(0, 1)

        @pl.when(n % 2 == 1)
        def _():
            fn(1, 0)

    reset()
    ss_ref[0][...] = sel_scores(0)

    def sel_body(j, carry):
        def run(cur, nxt):
            ss_ref[nxt][...] = sel_scores(j + 1)
            step(ss_ref[cur][...], vs_ref[pl.ds(sel_base(j), tks), :])

        by_parity(j, run)
        return carry

    lax.fori_loop(0, sel_last, sel_body, 0)
    by_parity(sel_last,
              lambda cur, nxt: step(ss_ref[cur][...], vs_ref[pl.ds(sel_base(sel_last), tks), :]))
    o_s = result()

    sig = jax.nn.sigmoid(gate_ref[...])
    for h in range(hpg):
        rows = slice(h * tq, (h + 1) * tq)
        cols = slice(h * HEAD_DIM, (h + 1) * HEAD_DIM)
        o = oc_ref[:, cols].astype(F32) + sig[:, hpg + h:hpg + h + 1] * o_s[rows]
        o_ref[:, cols] = o.astype(o_ref.dtype)


def _sw_attn(qn, bias, ksa, vs, o_c, proj_b, *, batch, seq, tq=ATTN_Q_TILE,
             tks=SEL_KEY_TILE):
    T = qn.shape[0]
    G = N_KV_GROUPS
    nq = seq // tq
    gw = HEADS_PER_GROUP * HEAD_DIM
    M = HEADS_PER_GROUP * tq
    rowg = lambda b, g, i: (b * nq + i, g)
    seqg = lambda b, g, i: (b, g)
    return pl.pallas_call(
        functools.partial(_sw_attn_kernel, tq=tq, tks=tks),
        grid=(batch, G, nq),
        in_specs=[
            pl.BlockSpec((tq, gw), rowg),
            pl.BlockSpec((tq, LANES), rowg),
            pl.BlockSpec((seq, 2 * HEAD_DIM), seqg),
            pl.BlockSpec((seq, 2 * HEAD_DIM), seqg),
            pl.BlockSpec((tq, gw), rowg),
            pl.BlockSpec((tq, LANES), lambda b, g, i: (b * nq + i, HEAD_GATE_BLK + g)),
        ],
        out_specs=pl.BlockSpec((tq, gw), rowg),
        out_shape=jax.ShapeDtypeStruct((T, B_WIDTH), BF16),
        scratch_shapes=[pltpu.VMEM((M, 2 * HEAD_DIM), BF16), pltpu.VMEM((M, LANES), F32),
                        pltpu.VMEM((M, 2 * HEAD_DIM), F32),
                        pltpu.VMEM((M, tks), F32), pltpu.VMEM((M, tks), F32)],
        compiler_params=_params("parallel", "parallel", "parallel"),
        name="sel_attn",
    )(qn, bias, ksa, vs, o_c, proj_b)


def _merge_kernel(oa_ref, ob_ref, ga_ref, gb_ref, wa_ref, wb_ref, o_ref):
    a = jnp.dot(oa_ref[...], wa_ref[...], preferred_element_type=F32)
    b = jnp.dot(ob_ref[...], wb_ref[...], preferred_element_type=F32)
    ga = jax.nn.sigmoid(ga_ref[...].astype(F32))
    gb = jax.nn.sigmoid(gb_ref[...].astype(F32))
    o_ref[...] = (ga * a + gb * b).astype(o_ref.dtype)


def _merge(o_a, o_b, proj_a, wa, wb, *, tm=ROW_TILE, tn=MERGE_TILE):
    T, D = o_a.shape
    N = wa.shape[1]
    nn = N // tn
    ga_blk = 2 * A_WIDTH // tn
    return pl.pallas_call(
        _merge_kernel,
        grid=(T // tm, nn),
        in_specs=[
            pl.BlockSpec((tm, D), lambda i, j: (i, 0)),
            pl.BlockSpec((tm, D), lambda i, j: (i, 0)),
            pl.BlockSpec((tm, tn), lambda i, j: (i, ga_blk + j)),
            pl.BlockSpec((tm, tn), lambda i, j: (i, ga_blk + nn + j)),
            pl.BlockSpec((D, tn), lambda i, j: (0, j)),
            pl.BlockSpec((D, tn), lambda i, j: (0, j)),
        ],
        out_specs=pl.BlockSpec((tm, tn), lambda i, j: (i, j)),
        out_shape=jax.ShapeDtypeStruct((T, N), BF16),
        compiler_params=_params("parallel", "arbitrary"),
        name="merge",
    )(o_a, o_b, proj_a, proj_a, wa, wb)


def _out_kernel(x_ref, m_ref, w_ref, o_ref):
    o_ref[...] = x_ref[...] + jnp.dot(m_ref[...], w_ref[...], preferred_element_type=F32)


def _out_proj(x, merged, w, *, tm=ROW_TILE, tn=MERGE_TILE):
    T, D = merged.shape
    N = w.shape[1]
    return pl.pallas_call(
        _out_kernel,
        grid=(T // tm, N // tn),
        in_specs=[
            pl.BlockSpec((tm, tn), lambda i, j: (i, j)),
            pl.BlockSpec((tm, D), lambda i, j: (i, 0)),
            pl.BlockSpec((D, tn), lambda i, j: (0, j)),
        ],
        out_specs=pl.BlockSpec((tm, tn), lambda i, j: (i, j)),
        out_shape=jax.ShapeDtypeStruct((T, N), F32),
        compiler_params=_params("parallel", "arbitrary"),
        name="out_proj",
    )(x, merged, w)


def _rope_tables(pos):
    half = ROPE_DIM // 2
    inv = ROPE_THETA ** (-2.0 * jnp.arange(half, dtype=F32) / ROPE_DIM)
    ang = pos.astype(F32)[:, None] * inv
    cos, sin = jnp.cos(ang), jnp.sin(ang)
    n = pos.shape[0]
    z16 = jnp.zeros((n, half), F32)
    rest = HEAD_DIM - ROPE_DIM
    c = jnp.concatenate([cos, cos, jnp.ones((n, rest), F32)], axis=1)
    su = jnp.concatenate([z16, sin, jnp.zeros((n, rest), F32)], axis=1)
    sd = jnp.concatenate([-sin, z16, jnp.zeros((n, rest), F32)], axis=1)
    return c, su, sd


def _regroup_w_in(w_in):
    sizes = [A_WIDTH, A_WIDTH, B_WIDTH] + [KV_WIDTH] * 6 + [3 * N_HEADS, D_MODEL, D_MODEL]
    offs = np.concatenate([[0], np.cumsum(sizes)])
    w = w_in.astype(BF16)
    zeros = lambda n: jnp.zeros((w.shape[0], n), w.dtype)
    gates = w[:, offs[9]:offs[10]]
    blocks = []
    for g in range(N_KV_GROUPS):
        cols = [gates[:, br * N_HEADS + g * HEADS_PER_GROUP:br * N_HEADS + (g + 1) * HEADS_PER_GROUP]
                for br in range(3)]
        blocks.extend(cols + [zeros(LANES - 3 * HEADS_PER_GROUP)])
    return jnp.concatenate([w[:, offs[2]:offs[9]]] + blocks + [zeros(PROJ_A_COL0 - PROJ_B_COLS)]
                           + [w[:, :offs[2]], w[:, offs[10]:]], axis=1)


def _overlap_t(nb_pad, n_sel):
    cmp_start = np.arange(nb_pad) * CMP_STRIDE
    sel_start = np.arange(n_sel) * SEL_BLOCK
    ov = ((cmp_start[None, :] < sel_start[:, None] + SEL_BLOCK)
          & (cmp_start[None, :] + CMP_BLOCK > sel_start[:, None]))
    return jnp.asarray(ov, dtype=BF16)


def _layer(x, ffn1_norm, ffn1_w_gate, ffn1_w_up, ffn1_w_down, mix_norm, w_in,
           a_v_norm, a_w_s, a_b_s, q_norm, k_cmp_norm, k_slc_norm, k_win_norm,
           cmp_k_pe, cmp_k_w1, cmp_k_w2, cmp_v_pe, cmp_v_w1, cmp_v_w2,
           w_branch_a, w_branch_b, w_out, ffn2_norm, ffn2_w_gate, ffn2_w_up, ffn2_w_down):
    B, S, D = x.shape
    T = B * S
    G = N_KV_GROUPS
    row = lambda v: v.reshape(1, -1)
    x0 = x.reshape(T, D)

    x1 = _ffn(x0, row(ffn1_norm), ffn1_w_gate, ffn1_w_up, ffn1_w_down)

    w_all = _regroup_w_in(w_in)
    proj_a = _proj(x1, row(mix_norm), w_all, BF16, tm=ROW_TILE, tn=PROJ_A_TILE,
                   col0=PROJ_A_COL0, ncols=2 * A_WIDTH + 2 * D_MODEL)
    proj_b = _proj(x1, row(mix_norm), w_all, F32, tm=ROW_TILE, tn=PROJ_B_TILE,
                   col0=0, ncols=PROJ_B_COLS)

    o_a = _gmlp(proj_a, row(a_v_norm), a_w_s, a_b_s.T)

    pos = jnp.arange(S)
    qn, ksa, vs, kwn, vw = _prep(proj_b, row(q_norm), row(k_slc_norm), row(k_win_norm),
                                 *_rope_tables(pos), seq=S)

    nb_pad = S // CMP_STRIDE
    cmp_end = jnp.arange(nb_pad) * CMP_STRIDE + (CMP_BLOCK - 1)
    kc, vc = _compress(proj_b, cmp_k_pe.reshape(1, -1), cmp_v_pe.reshape(1, -1),
                       cmp_k_w1.astype(BF16), cmp_k_w2.astype(BF16),
                       cmp_v_w1.astype(BF16), cmp_v_w2.astype(BF16),
                       row(k_cmp_norm), *_rope_tables(cmp_end), batch=B, seq=S)

    o_cw, bias = _cmp_attn(qn, kc, vc, kwn, vw, proj_b, _overlap_t(nb_pad, S // SEL_BLOCK),
                           batch=B, seq=S)
    o_b = _sw_attn(qn, bias, ksa, vs, o_cw, proj_b, batch=B, seq=S)

    merged = _merge(o_a, o_b, proj_a, w_branch_a.astype(BF16), w_branch_b.astype(BF16))
    x2 = _out_proj(x1, merged, w_out.astype(BF16))

    x3 = _ffn(x2, row(ffn2_norm), ffn2_w_gate, ffn2_w_up, ffn2_w_down)
    return x3.reshape(B, S, D)


def kernel(x, ffn1_norm, ffn1_w_gate, ffn1_w_up, ffn1_w_down, mix_norm, w_in, a_v_norm, a_w_s, a_b_s, q_norm, k_cmp_norm, k_slc_norm, k_win_norm, cmp_k_pe, cmp_k_w1, cmp_k_w2, cmp_v_pe, cmp_v_w1, cmp_v_w2, w_branch_a, w_branch_b, w_out, ffn2_norm, ffn2_w_gate, ffn2_w_up, ffn2_w_down):
    params = (ffn1_norm, ffn1_w_gate, ffn1_w_up, ffn1_w_down, mix_norm, w_in,
              a_v_norm, a_w_s, a_b_s, q_norm, k_cmp_norm, k_slc_norm, k_win_norm,
              cmp_k_pe, cmp_k_w1, cmp_k_w2, cmp_v_pe, cmp_v_w1, cmp_v_w2,
              w_branch_a, w_branch_b, w_out, ffn2_norm, ffn2_w_gate, ffn2_w_up, ffn2_w_down)
    for l in range(params[0].shape[0]):
        x = _layer(x, *[p[l] for p in params])
    return x
```

```python
import functools

import numpy as np
import jax
import jax.numpy as jnp
from jax import lax
from jax.experimental import pallas as pl
from jax.experimental.pallas import tpu as pltpu

D_MODEL = 2048
D_FF = 5504
A_WIDTH = 2048
A_GROUPS = 8
A_CHUNK = 128
N_HEADS = 16
HEAD_DIM = 128
N_KV_GROUPS = 2
HEADS_PER_GROUP = N_HEADS // N_KV_GROUPS
CMP_BLOCK = 32
CMP_STRIDE = 16
CMP_HIDDEN = 256
SEL_BLOCK = 64
SEL_TOP = 16
WINDOW = 512
ROPE_THETA = 500000.0
ROPE_DIM = HEAD_DIM // 4
EPS = 1e-6
NEG = -1e30
FORCE_BONUS = 1e4
LOG2_E = 1.4426950408889634
B_WIDTH = N_HEADS * HEAD_DIM
KV_WIDTH = N_KV_GROUPS * HEAD_DIM

LANES = 128
HEAD_GATE_BLK = (B_WIDTH + 6 * KV_WIDTH) // LANES
VMEM_LIMIT = 56 * 1024 * 1024
FFN_VMEM_LIMIT = 59904 * 1024

ROW_TILE = 1024
FFN_HIDDEN_TILE = 256
PROJ_A_TILE = 2048
PROJ_B_TILE = 1920
MERGE_TILE = 1024
POINTWISE_ROWS = 512
ATTN_Q_TILE = 256
SEL_KEY_TILE = 512

F32 = jnp.float32
BF16 = jnp.bfloat16
NT_DIMS = (((1,), (1,)), ((), ()))


def _params(*sem, vmem_limit=VMEM_LIMIT):
    return pltpu.CompilerParams(dimension_semantics=sem, vmem_limit_bytes=vmem_limit)


def _rms(x, g):
    ms = jnp.mean(x * x, axis=-1, keepdims=True)
    return x * lax.rsqrt(ms + EPS) * g


def _ffn_kernel(x_ref, g_ref, wg_ref, wu_ref, wd_ref, *rest, nf, tf, f_last, emit_norm):
    if emit_norm:
        gn_ref, o_ref, hn_ref, h_ref = rest
    else:
        o_ref, h_ref = rest
    j = pl.program_id(1)

    def partial(width):
        h = h_ref[...]
        a = jnp.dot(h, wg_ref[:, :width].astype(BF16), preferred_element_type=F32)
        b = jnp.dot(h, wu_ref[:, :width].astype(BF16), preferred_element_type=F32)
        act = (a * jax.nn.sigmoid(a) * b).astype(BF16)
        return jnp.dot(act, wd_ref[:width, :].astype(BF16), preferred_element_type=F32)

    @pl.when(j == 0)
    def _():
        h_ref[...] = _rms(x_ref[...], g_ref[...]).astype(BF16)
        o_ref[...] = partial(f_last)

    @pl.when((j > 0) & (j < nf - 1))
    def _():
        o_ref[...] += partial(tf)

    @pl.when(j == nf - 1)
    def _():
        y = x_ref[...] + 0.5 * (o_ref[...] + partial(tf))
        o_ref[...] = y
        if emit_norm:
            hn_ref[...] = _rms(y, gn_ref[...]).astype(BF16)


def _ffn(x, norm, wg, wu, wd, next_norm=None, *, tm=ROW_TILE, tf=FFN_HIDDEN_TILE):
    T, D = x.shape
    F = wg.shape[1]
    nf = pl.cdiv(F, tf)
    f_last = F - (nf - 1) * tf
    emit_norm = next_norm is not None
    hid = lambda j: (j + nf - 1) % nf
    row_tile = pl.BlockSpec((tm, D), lambda i, j: (i, 0))
    gain = pl.BlockSpec((1, D), lambda i, j: (0, 0))
    return pl.pallas_call(
        functools.partial(_ffn_kernel, nf=nf, tf=tf, f_last=f_last, emit_norm=emit_norm),
        grid=(T // tm, nf),
        in_specs=[
            row_tile,
            gain,
            pl.BlockSpec((D, tf), lambda i, j: (0, hid(j))),
            pl.BlockSpec((D, tf), lambda i, j: (0, hid(j))),
            pl.BlockSpec((tf, D), lambda i, j: (hid(j), 0)),
        ] + ([gain] if emit_norm else []),
        out_specs=[row_tile, row_tile] if emit_norm else row_tile,
        out_shape=([jax.ShapeDtypeStruct((T, D), F32), jax.ShapeDtypeStruct((T, D), BF16)]
                   if emit_norm else jax.ShapeDtypeStruct((T, D), F32)),
        scratch_shapes=[pltpu.VMEM((tm, D), BF16)],
        compiler_params=_params("parallel", "arbitrary", vmem_limit=FFN_VMEM_LIMIT),
        name="ffn",
    )(x, norm, wg, wu, wd, *([next_norm] if emit_norm else []))


def _proj_kernel(h_ref, w_ref, o_ref):
    o_ref[...] = jnp.dot(h_ref[...], w_ref[...], preferred_element_type=F32).astype(o_ref.dtype)


def _proj(h, w, out_dtype, *, tm, tn):
    T, D = h.shape
    N = w.shape[1]
    return pl.pallas_call(
        _proj_kernel,
        grid=(T // tm, N // tn),
        in_specs=[
            pl.BlockSpec((tm, D), lambda i, j: (i, 0)),
            pl.BlockSpec((D, tn), lambda i, j: (0, j)),
        ],
        out_specs=pl.BlockSpec((tm, tn), lambda i, j: (i, j)),
        out_shape=jax.ShapeDtypeStruct((T, N), out_dtype),
        compiler_params=_params("parallel", "arbitrary"),
        name="proj",
    )(h, w)


def _gmlp_kernel(u_ref, v_ref, vn_ref, ws_ref, bs_ref, o_ref, *, tr):
    gw = A_WIDTH // A_GROUPS
    v = jax.nn.gelu(v_ref[...].astype(F32))
    vn = _rms(v, vn_ref[...]).astype(BF16)
    row = lax.broadcasted_iota(jnp.int32, (A_CHUNK, A_CHUNK), 0)
    col = lax.broadcasted_iota(jnp.int32, (A_CHUNK, A_CHUNK), 1)
    for g in range(A_GROUPS):
        w = jnp.where(col <= row, ws_ref[g], 0.0).astype(BF16)
        bias = bs_ref[:, g:g + 1]
        for c in range(tr // A_CHUNK):
            rows = slice(c * A_CHUNK, (c + 1) * A_CHUNK)
            cols = slice(g * gw, (g + 1) * gw)
            mixed = jnp.dot(w, vn[rows, cols], preferred_element_type=F32) + bias
            u = jax.nn.gelu(u_ref[rows, cols].astype(F32))
            o_ref[rows, cols] = (u * mixed).astype(o_ref.dtype)


def _gmlp(proj_a, a_v_norm, w_s, b_s_t, *, tr=POINTWISE_ROWS):
    T = proj_a.shape[0]
    return pl.pallas_call(
        functools.partial(_gmlp_kernel, tr=tr),
        grid=(T // tr,),
        in_specs=[
            pl.BlockSpec((tr, A_WIDTH), lambda i: (i, 0)),
            pl.BlockSpec((tr, A_WIDTH), lambda i: (i, 1)),
            pl.BlockSpec((1, A_WIDTH), lambda i: (0, 0)),
            pl.BlockSpec((A_GROUPS, A_CHUNK, A_CHUNK), lambda i: (0, 0, 0)),
            pl.BlockSpec((A_CHUNK, A_GROUPS), lambda i: (0, 0)),
        ],
        out_specs=pl.BlockSpec((tr, A_WIDTH), lambda i: (i, 0)),
        out_shape=jax.ShapeDtypeStruct((T, A_WIDTH), BF16),
        compiler_params=_params("parallel"),
        name="gmlp",
    )(proj_a, proj_a, a_v_norm, w_s, b_s_t)


def _rope(y, c, su, sd):
    half = ROPE_DIM // 2
    return y * c + pltpu.roll(y, half, 1) * su + pltpu.roll(y, HEAD_DIM - half, 1) * sd


def _norm_rope_mxu(x, g, c, s, w):
    xg = x * g
    lhs = jnp.concatenate([(x * x).astype(BF16), xg.astype(BF16)], axis=1)
    res = jnp.dot(lhs, w, preferred_element_type=F32)
    r = lax.rsqrt(res[:, :HEAD_DIM] * (1.0 / HEAD_DIM) + EPS)
    return (xg * c + res[:, HEAD_DIM:] * s) * r


def _prep_kernel(q_ref, ks_ref, vs_ref, kw_ref, vw_ref, qg_ref, ksg_ref, kwg_ref,
                 cq_ref, sq_ref, ck_ref, sk_ref, w_ref,
                 qn_ref, ksa_ref, vso_ref, kwn_ref, vwo_ref, *, tr, seq):
    cq, sq, ck, sk, w = cq_ref[...], sq_ref[...], ck_ref[...], sk_ref[...], w_ref[...]
    for h in range(N_HEADS):
        cols = slice(h * HEAD_DIM, (h + 1) * HEAD_DIM)
        qn_ref[:, cols] = _norm_rope_mxu(q_ref[:, cols], qg_ref[...], cq, sq, w).astype(BF16)
    t0 = (pl.program_id(0) * tr) % seq
    t = t0 + lax.broadcasted_iota(jnp.int32, (tr, LANES), 0)
    lane = lax.broadcasted_iota(jnp.int32, (tr, LANES), 1)
    onehot = jnp.where((t // SEL_BLOCK) == lane, 1.0, 0.0).astype(BF16)
    ones = jnp.ones((tr, HEAD_DIM), BF16)
    for g in range(N_KV_GROUPS):
        cols = slice(g * HEAD_DIM, (g + 1) * HEAD_DIM)
        lo = slice(2 * g * HEAD_DIM, (2 * g + 1) * HEAD_DIM)
        hi = slice((2 * g + 1) * HEAD_DIM, (2 * g + 2) * HEAD_DIM)
        ksa_ref[:, lo] = _norm_rope_mxu(ks_ref[:, cols], ksg_ref[...], ck, sk, w).astype(BF16)
        ksa_ref[:, hi] = onehot
        kwn_ref[:, cols] = _norm_rope_mxu(kw_ref[:, cols], kwg_ref[...], ck, sk, w).astype(BF16)
        vso_ref[:, lo] = vs_ref[:, cols].astype(BF16)
        vso_ref[:, hi] = ones
        vwo_ref[:, lo] = vw_ref[:, cols].astype(BF16)
        vwo_ref[:, hi] = ones


def _rope_matrix():
    half = ROPE_DIM // 2
    w = np.zeros((2 * HEAD_DIM, 2 * HEAD_DIM), np.float32)
    w[:HEAD_DIM, :HEAD_DIM] = 1.0
    for j in range(half):
        w[HEAD_DIM + j + half, HEAD_DIM + j] = 1.0
        w[HEAD_DIM + j, HEAD_DIM + j + half] = 1.0
    return jnp.asarray(w, dtype=BF16)


def _prep(proj_b, q_norm, k_slc_norm, k_win_norm, rope_c, rope_su, rope_sd, *, seq,
          tr=POINTWISE_ROWS):
    T = proj_b.shape[0]
    kvb = B_WIDTH // KV_WIDTH
    nrt = seq // tr
    q_scale = HEAD_DIM ** -0.5 * LOG2_E
    rope_s = rope_su + rope_sd
    w = _rope_matrix()
    row = lambda i: (i, 0)
    tab = pl.BlockSpec((tr, HEAD_DIM), lambda i: (i % nrt, 0))
    gain = pl.BlockSpec((1, HEAD_DIM), lambda i: (0, 0))
    kv_in = lambda k: pl.BlockSpec((tr, KV_WIDTH), lambda i: (i, kvb + k))
    return pl.pallas_call(
        functools.partial(_prep_kernel, tr=tr, seq=seq),
        grid=(T // tr,),
        in_specs=[pl.BlockSpec((tr, B_WIDTH), row), kv_in(2), kv_in(3), kv_in(4), kv_in(5),
                  gain, gain, gain, tab, tab, tab, tab, pl.BlockSpec(w.shape, lambda i: (0, 0))],
        out_specs=[pl.BlockSpec((tr, B_WIDTH), row), pl.BlockSpec((tr, 2 * KV_WIDTH), row),
                   pl.BlockSpec((tr, 2 * KV_WIDTH), row), pl.BlockSpec((tr, KV_WIDTH), row),
                   pl.BlockSpec((tr, 2 * KV_WIDTH), row)],
        out_shape=[jax.ShapeDtypeStruct((T, B_WIDTH), BF16),
                   jax.ShapeDtypeStruct((T, 2 * KV_WIDTH), BF16),
                   jax.ShapeDtypeStruct((T, 2 * KV_WIDTH), BF16),
                   jax.ShapeDtypeStruct((T, KV_WIDTH), BF16),
                   jax.ShapeDtypeStruct((T, 2 * KV_WIDTH), BF16)],
        compiler_params=_params("parallel"),
        name="qk_prep",
    )(proj_b, proj_b, proj_b, proj_b, proj_b, q_norm, k_slc_norm, k_win_norm,
      rope_c * q_scale, rope_s * q_scale, rope_c, rope_s, w)


def _compress_kernel(zk_ref, zv_ref, pek_ref, pev_ref, w1k_ref, w2k_ref, w1v_ref, w2v_ref,
                     kg_ref, c_ref, su_ref, sd_ref, kc_ref, vc_ref, zr_ref):
    half = CMP_STRIDE * HEAD_DIM
    nb = zr_ref.shape[0]

    def comp(z_ref, pe_ref, w1_ref, w2_ref):
        for l in range(CMP_STRIDE):
            zr_ref[:, l * HEAD_DIM:(l + 1) * HEAD_DIM] = z_ref[pl.ds(l, nb, stride=CMP_STRIDE), :]
        z = zr_ref[...]
        top = jnp.dot((z + pe_ref[:, :half]).astype(BF16), w1_ref[:half, :], preferred_element_type=F32)
        bot = jnp.dot((z + pe_ref[:, half:]).astype(BF16), w1_ref[half:, :], preferred_element_type=F32)
        hid = top + pltpu.roll(bot, nb - 1, 0)
        return jnp.dot(jax.nn.gelu(hid).astype(BF16), w2_ref[...], preferred_element_type=F32)

    k = comp(zk_ref, pek_ref, w1k_ref, w2k_ref)
    kc_ref[...] = _rope(_rms(k, kg_ref[...]), c_ref[...], su_ref[...], sd_ref[...]).astype(BF16)
    vc_ref[...] = comp(zv_ref, pev_ref, w1v_ref, w2v_ref).astype(BF16)


def _compress(proj_b, pek, pev, w1k, w2k, w1v, w2v, k_cmp_norm, cmp_c, cmp_su, cmp_sd, *, batch, seq):
    G = N_KV_GROUPS
    nb = seq // CMP_STRIDE
    kc_blk = B_WIDTH // HEAD_DIM
    vc_blk = kc_blk + G
    ospec = pl.BlockSpec((None, None, nb, HEAD_DIM), lambda b, g: (b, g, 0, 0))
    full = lambda a: pl.BlockSpec(a.shape, lambda b, g: (0,) * a.ndim)
    consts = (pek, pev, w1k, w2k, w1v, w2v, k_cmp_norm, cmp_c, cmp_su, cmp_sd)
    return pl.pallas_call(
        _compress_kernel,
        grid=(batch, G),
        in_specs=[pl.BlockSpec((seq, HEAD_DIM), lambda b, g: (b, kc_blk + g)),
                  pl.BlockSpec((seq, HEAD_DIM), lambda b, g: (b, vc_blk + g))]
                 + [full(a) for a in consts],
        out_specs=[ospec, ospec],
        out_shape=[jax.ShapeDtypeStruct((batch, G, nb, HEAD_DIM), BF16)] * 2,
        scratch_shapes=[pltpu.VMEM((nb, CMP_STRIDE * HEAD_DIM), F32)],
        compiler_params=_params("parallel", "parallel"),
        name="compress",
    )(proj_b, proj_b, *consts)


def _cmp_attn_kernel(q_ref, kc_ref, vc_ref, kw_ref, vw_ref, gate_ref, ovt_ref, oc_ref, bias_ref,
                     qs_ref, *, tq):
    hpg = HEADS_PER_GROUP
    M = hpg * tq
    nb = kc_ref.shape[0]
    n_sel = ovt_ref.shape[0]
    t0 = pl.program_id(2) * tq
    for h in range(hpg):
        qs_ref[h * tq:(h + 1) * tq, :] = q_ref[:, h * HEAD_DIM:(h + 1) * HEAD_DIM]

    wlen = WINDOW + tq
    kstart = pl.multiple_of(jnp.maximum(t0 - WINDOW, 0), tq)
    sw = lax.dot_general(qs_ref[...], kw_ref[pl.ds(kstart, wlen), :], NT_DIMS,
                         preferred_element_type=F32)
    tw = t0 + lax.broadcasted_iota(jnp.int32, (1, tq, wlen), 1)
    kp = kstart + lax.broadcasted_iota(jnp.int32, (1, tq, wlen), 2)
    keep = (tw - kp).astype(jnp.uint32) < WINDOW
    sw = jnp.where(keep, sw.reshape(hpg, tq, wlen), NEG).reshape(M, wlen)
    pw = jnp.exp2(sw - jnp.max(sw, axis=-1, keepdims=True))
    aw = jnp.dot(pw.astype(BF16), vw_ref[pl.ds(kstart, wlen), :], preferred_element_type=F32)
    o_w = aw[:, :HEAD_DIM] * (1.0 / aw[:, HEAD_DIM:])

    s = lax.dot_general(qs_ref[...], kc_ref[...], NT_DIMS, preferred_element_type=F32)
    s = s.reshape(hpg, tq, nb)
    t = t0 + lax.broadcasted_iota(jnp.int32, (1, tq, nb), 1)
    n = lax.broadcasted_iota(jnp.int32, (1, tq, nb), 2)
    mask = n * CMP_STRIDE + (CMP_BLOCK - 1) <= t
    s = jnp.where(mask, s, NEG)
    e = jnp.exp2(s - jnp.max(s, axis=-1, keepdims=True))
    inv = 1.0 / jnp.sum(e, axis=-1, keepdims=True)
    p = jnp.where(mask, e * inv, 0.0)
    o = jnp.dot(p.reshape(hpg * tq, nb).astype(BF16), vc_ref[...], preferred_element_type=F32)
    sig = jax.nn.sigmoid(gate_ref[...])
    for h in range(hpg):
        rows = slice(h * tq, (h + 1) * tq)
        oc_ref[:, h * HEAD_DIM:(h + 1) * HEAD_DIM] = (
            o[rows] * sig[:, h:h + 1]
            + o_w[rows] * sig[:, 2 * hpg + h:2 * hpg + h + 1]).astype(oc_ref.dtype)

    psum = jnp.sum(p, axis=0)
    ovt = ovt_ref[...]
    hi = psum.astype(BF16)
    r1 = psum - hi.astype(F32)
    mid = r1.astype(BF16)
    lo = (r1 - mid.astype(F32)).astype(BF16)
    imp = (lax.dot_general(ovt, hi, NT_DIMS, preferred_element_type=F32)
           + lax.dot_general(ovt, mid, NT_DIMS, preferred_element_type=F32)
           + lax.dot_general(ovt, lo, NT_DIMS, preferred_element_type=F32))
    j = lax.broadcasted_iota(jnp.int32, (n_sel, tq), 0)
    tt = t0 + lax.broadcasted_iota(jnp.int32, (n_sel, tq), 1)
    cur = tt // SEL_BLOCK
    forced = jnp.where((j == 0) | (j == cur) | (j == cur - 1), FORCE_BONUS, 0.0)
    score = jnp.where(j * SEL_BLOCK <= tt, imp + forced, NEG)
    jrow = j.astype(F32)
    below_neg = 3.0 * NEG
    chosen = jnp.zeros((n_sel, tq), F32)
    for _ in range(SEL_TOP):
        top = jnp.max(score, axis=0, keepdims=True)
        idx = jnp.min(jnp.where(score == top, jrow, float(n_sel)), axis=0, keepdims=True)
        hit = jrow == idx
        chosen = jnp.where(hit, 1.0, chosen)
        score = jnp.where(hit, below_neg, score)
    bias = jnp.where(chosen > 0.5, 0.0, NEG)
    bias = jnp.concatenate([bias, jnp.zeros((LANES - n_sel, tq), F32)], axis=0)
    bias_ref[...] = bias.T.astype(bias_ref.dtype)


def _cmp_attn(qn, kc, vc, kwn, vw, proj_b, ovt, *, batch, seq, tq=ATTN_Q_TILE):
    assert WINDOW % tq == 0
    T = qn.shape[0]
    G = N_KV_GROUPS
    nq = seq // tq
    gw = HEADS_PER_GROUP * HEAD_DIM
    nb = kc.shape[2]
    rowg = lambda b, g, i: (b * nq + i, g)
    seqg = lambda b, g, i: (b, g)
    return pl.pallas_call(
        functools.partial(_cmp_attn_kernel, tq=tq),
        grid=(batch, G, nq),
        in_specs=[
            pl.BlockSpec((tq, gw), rowg),
            pl.BlockSpec((None, None, nb, HEAD_DIM), lambda b, g, i: (b, g, 0, 0)),
            pl.BlockSpec((None, None, nb, HEAD_DIM), lambda b, g, i: (b, g, 0, 0)),
            pl.BlockSpec((seq, HEAD_DIM), seqg),
            pl.BlockSpec((seq, 2 * HEAD_DIM), seqg),
            pl.BlockSpec((tq, LANES), lambda b, g, i: (b * nq + i, HEAD_GATE_BLK + g)),
            pl.BlockSpec(ovt.shape, lambda b, g, i: (0, 0)),
        ],
        out_specs=[pl.BlockSpec((tq, gw), rowg), pl.BlockSpec((tq, LANES), rowg)],
        out_shape=[jax.ShapeDtypeStruct((T, B_WIDTH), BF16),
                   jax.ShapeDtypeStruct((T, G * LANES), BF16)],
        scratch_shapes=[pltpu.VMEM((HEADS_PER_GROUP * tq, HEAD_DIM), BF16)],
        compiler_params=_params("parallel", "parallel", "parallel"),
        name="cmp_win_attn",
    )(qn, kc, vc, kwn, vw, proj_b, ovt)


def _sw_attn_kernel(q_ref, bias_ref, ksa_ref, vs_ref, oc_ref, gate_ref, o_ref,
                    qs_ref, m_ref, acc_ref, ss0_ref, ss1_ref, *, tq, tks):
    hpg = HEADS_PER_GROUP
    M = hpg * tq
    ss_ref = (ss0_ref, ss1_ref)
    t0 = pl.program_id(2) * tq
    bias = bias_ref[...]
    for h in range(hpg):
        qs_ref[h * tq:(h + 1) * tq, :HEAD_DIM] = q_ref[:, h * HEAD_DIM:(h + 1) * HEAD_DIM]
        qs_ref[h * tq:(h + 1) * tq, HEAD_DIM:] = bias

    def reset():
        m_ref[...] = jnp.full_like(m_ref, NEG)
        acc_ref[...] = jnp.zeros_like(acc_ref)

    def step(s, v):
        m_old = m_ref[...]
        m_new = jnp.maximum(m_old, jnp.max(s, axis=-1, keepdims=True))
        alpha = jnp.exp2(m_old - m_new)
        p = jnp.exp2(s - jnp.tile(m_new, (1, s.shape[1] // LANES)))
        acc_ref[...] = (jnp.tile(alpha, (1, 2)) * acc_ref[...]
                        + jnp.dot(p.astype(BF16), v, preferred_element_type=F32))
        m_ref[...] = m_new

    def result():
        return acc_ref[:, :HEAD_DIM] * (1.0 / acc_ref[:, HEAD_DIM:])

    def masked(s, keep):
        tk = s.shape[1]
        return jnp.where(keep, s.reshape(hpg, tq, tk), NEG).reshape(M, tk)

    def positions(tk):
        t = t0 + lax.broadcasted_iota(jnp.int32, (1, tq, tk), 1)
        kpos = lax.broadcasted_iota(jnp.int32, (1, tq, tk), 2)
        return t, kpos

    def sel_base(j):
        return pl.multiple_of(j * tks, tks)

    def sel_scores(j):
        base = sel_base(j)
        k = ksa_ref[pl.ds(base, tks), :]
        s = lax.dot_general(qs_ref[...], k, NT_DIMS, preferred_element_type=F32)
        t, kpos = positions(tks)
        return masked(s, kpos + base <= t)

    sel_last = (t0 + tq - 1) // tks

    def by_parity(n, fn):
        @pl.when(n % 2 == 0)
        def _():
            fn(0, 1)

        @pl.when(n % 2 == 1)
        def _():
            fn(1, 0)

    reset()
    ss_ref[0][...] = sel_scores(0)

    def sel_body(j, carry):
        def run(cur, nxt):
            ss_ref[nxt][...] = sel_scores(j + 1)
            step(ss_ref[cur][...], vs_ref[pl.ds(sel_base(j), tks), :])

        by_parity(j, run)
        return carry

    lax.fori_loop(0, sel_last, sel_body, 0)
    by_parity(sel_last,
              lambda cur, nxt: step(ss_ref[cur][...], vs_ref[pl.ds(sel_base(sel_last), tks), :]))
    o_s = result()

    sig = jax.nn.sigmoid(gate_ref[...])
    for h in range(hpg):
        rows = slice(h * tq, (h + 1) * tq)
        cols = slice(h * HEAD_DIM, (h + 1) * HEAD_DIM)
        o = oc_ref[:, cols].astype(F32) + sig[:, hpg + h:hpg + h + 1] * o_s[rows]
        o_ref[:, cols] = o.astype(o_ref.dtype)


def _sw_attn(qn, bias, ksa, vs, o_c, proj_b, *, batch, seq, tq=ATTN_Q_TILE,
             tks=SEL_KEY_TILE):
    T = qn.shape[0]
    G = N_KV_GROUPS
    nq = seq // tq
    gw = HEADS_PER_GROUP * HEAD_DIM
    M = HEADS_PER_GROUP * tq
    rowg = lambda b, g, i: (b * nq + i, g)
    seqg = lambda b, g, i: (b, g)
    return pl.pallas_call(
        functools.partial(_sw_attn_kernel, tq=tq, tks=tks),
        grid=(batch, G, nq),
        in_specs=[
            pl.BlockSpec((tq, gw), rowg),
            pl.BlockSpec((tq, LANES), rowg),
            pl.BlockSpec((seq, 2 * HEAD_DIM), seqg),
            pl.BlockSpec((seq, 2 * HEAD_DIM), seqg),
            pl.BlockSpec((tq, gw), rowg),
            pl.BlockSpec((tq, LANES), lambda b, g, i: (b * nq + i, HEAD_GATE_BLK + g)),
        ],
        out_specs=pl.BlockSpec((tq, gw), rowg),
        out_shape=jax.ShapeDtypeStruct((T, B_WIDTH), BF16),
        scratch_shapes=[pltpu.VMEM((M, 2 * HEAD_DIM), BF16), pltpu.VMEM((M, LANES), F32),
                        pltpu.VMEM((M, 2 * HEAD_DIM), F32),
                        pltpu.VMEM((M, tks), F32), pltpu.VMEM((M, tks), F32)],
        compiler_params=_params("parallel", "parallel", "parallel"),
        name="sel_attn",
    )(qn, bias, ksa, vs, o_c, proj_b)


def _merge_kernel(oa_ref, ob_ref, ga_ref, gb_ref, wa_ref, wb_ref, o_ref):
    a = jnp.dot(oa_ref[...], wa_ref[...], preferred_element_type=F32)
    b = jnp.dot(ob_ref[...], wb_ref[...], preferred_element_type=F32)
    ga = jax.nn.sigmoid(ga_ref[...].astype(F32))
    gb = jax.nn.sigmoid(gb_ref[...].astype(F32))
    o_ref[...] = (ga * a + gb * b).astype(o_ref.dtype)


def _merge(o_a, o_b, proj_a, wa, wb, *, tm=ROW_TILE, tn=MERGE_TILE):
    T, D = o_a.shape
    N = wa.shape[1]
    nn = N // tn
    ga_blk = 2 * A_WIDTH // tn
    return pl.pallas_call(
        _merge_kernel,
        grid=(T // tm, nn),
        in_specs=[
            pl.BlockSpec((tm, D), lambda i, j: (i, 0)),
            pl.BlockSpec((tm, D), lambda i, j: (i, 0)),
            pl.BlockSpec((tm, tn), lambda i, j: (i, ga_blk + j)),
            pl.BlockSpec((tm, tn), lambda i, j: (i, ga_blk + nn + j)),
            pl.BlockSpec((D, tn), lambda i, j: (0, j)),
            pl.BlockSpec((D, tn), lambda i, j: (0, j)),
        ],
        out_specs=pl.BlockSpec((tm, tn), lambda i, j: (i, j)),
        out_shape=jax.ShapeDtypeStruct((T, N), BF16),
        compiler_params=_params("parallel", "arbitrary"),
        name="merge",
    )(o_a, o_b, proj_a, proj_a, wa, wb)


def _out_kernel(x_ref, m_ref, w_ref, o_ref):
    o_ref[...] = x_ref[...] + jnp.dot(m_ref[...], w_ref[...], preferred_element_type=F32)


def _out_proj(x, merged, w, *, tm=ROW_TILE, tn=MERGE_TILE):
    T, D = merged.shape
    N = w.shape[1]
    return pl.pallas_call(
        _out_kernel,
        grid=(T // tm, N // tn),
        in_specs=[
            pl.BlockSpec((tm, tn), lambda i, j: (i, j)),
            pl.BlockSpec((tm, D), lambda i, j: (i, 0)),
            pl.BlockSpec((D, tn), lambda i, j: (0, j)),
        ],
        out_specs=pl.BlockSpec((tm, tn), lambda i, j: (i, j)),
        out_shape=jax.ShapeDtypeStruct((T, N), F32),
        compiler_params=_params("parallel", "arbitrary"),
        name="out_proj",
    )(x, merged, w)


def _rope_tables(pos):
    half = ROPE_DIM // 2
    inv = ROPE_THETA ** (-2.0 * jnp.arange(half, dtype=F32) / ROPE_DIM)
    ang = pos.astype(F32)[:, None] * inv
    cos, sin = jnp.cos(ang), jnp.sin(ang)
    n = pos.shape[0]
    z16 = jnp.zeros((n, half), F32)
    rest = HEAD_DIM - ROPE_DIM
    c = jnp.concatenate([cos, cos, jnp.ones((n, rest), F32)], axis=1)
    su = jnp.concatenate([z16, sin, jnp.zeros((n, rest), F32)], axis=1)
    sd = jnp.concatenate([-sin, z16, jnp.zeros((n, rest), F32)], axis=1)
    return c, su, sd


def _split_w_in(w_in):
    sizes = [A_WIDTH, A_WIDTH, B_WIDTH] + [KV_WIDTH] * 6 + [3 * N_HEADS, D_MODEL, D_MODEL]
    offs = np.concatenate([[0], np.cumsum(sizes)])
    w = w_in.astype(BF16)
    w_a = jnp.concatenate([w[:, :offs[2]], w[:, offs[10]:]], axis=1)
    gates = w[:, offs[9]:offs[10]]
    blocks = []
    for g in range(N_KV_GROUPS):
        cols = [gates[:, br * N_HEADS + g * HEADS_PER_GROUP:br * N_HEADS + (g + 1) * HEADS_PER_GROUP]
                for br in range(3)]
        pad = jnp.zeros((w.shape[0], LANES - 3 * HEADS_PER_GROUP), w.dtype)
        blocks.extend(cols + [pad])
    w_b = jnp.concatenate([w[:, offs[2]:offs[9]]] + blocks, axis=1)
    return w_a, w_b


def _overlap_t(nb_pad, n_sel):
    cmp_start = np.arange(nb_pad) * CMP_STRIDE
    sel_start = np.arange(n_sel) * SEL_BLOCK
    ov = ((cmp_start[None, :] < sel_start[:, None] + SEL_BLOCK)
          & (cmp_start[None, :] + CMP_BLOCK > sel_start[:, None]))
    return jnp.asarray(ov, dtype=BF16)


def _layer(x, ffn1_norm, ffn1_w_gate, ffn1_w_up, ffn1_w_down, mix_norm, w_in,
           a_v_norm, a_w_s, a_b_s, q_norm, k_cmp_norm, k_slc_norm, k_win_norm,
           cmp_k_pe, cmp_k_w1, cmp_k_w2, cmp_v_pe, cmp_v_w1, cmp_v_w2,
           w_branch_a, w_branch_b, w_out, ffn2_norm, ffn2_w_gate, ffn2_w_up, ffn2_w_down):
    B, S, D = x.shape
    T = B * S
    G = N_KV_GROUPS
    row = lambda v: v.reshape(1, -1)
    x0 = x.reshape(T, D)

    x1, h1 = _ffn(x0, row(ffn1_norm), ffn1_w_gate, ffn1_w_up, ffn1_w_down, row(mix_norm))

    w_a, w_b = _split_w_in(w_in)
    proj_a = _proj(h1, w_a, BF16, tm=ROW_TILE, tn=PROJ_A_TILE)
    proj_b = _proj(h1, w_b, F32, tm=ROW_TILE, tn=PROJ_B_TILE)

    o_a = _gmlp(proj_a, row(a_v_norm), a_w_s, a_b_s.T)

    pos = jnp.arange(S)
    qn, ksa, vs, kwn, vw = _prep(proj_b, row(q_norm), row(k_slc_norm), row(k_win_norm),
                                 *_rope_tables(pos), seq=S)

    nb_pad = S // CMP_STRIDE
    cmp_end = jnp.arange(nb_pad) * CMP_STRIDE + (CMP_BLOCK - 1)
    kc, vc = _compress(proj_b, cmp_k_pe.reshape(1, -1), cmp_v_pe.reshape(1, -1),
                       cmp_k_w1.astype(BF16), cmp_k_w2.astype(BF16),
                       cmp_v_w1.astype(BF16), cmp_v_w2.astype(BF16),
                       row(k_cmp_norm), *_rope_tables(cmp_end), batch=B, seq=S)

    o_cw, bias = _cmp_attn(qn, kc, vc, kwn, vw, proj_b, _overlap_t(nb_pad, S // SEL_BLOCK),
                           batch=B, seq=S)
    o_b = _sw_attn(qn, bias, ksa, vs, o_cw, proj_b, batch=B, seq=S)

    merged = _merge(o_a, o_b, proj_a, w_branch_a.astype(BF16), w_branch_b.astype(BF16))
    x2 = _out_proj(x1, merged, w_out.astype(BF16))

    x3 = _ffn(x2, row(ffn2_norm), ffn2_w_gate, ffn2_w_up, ffn2_w_down)
    return x3.reshape(B, S, D)


def kernel(x, ffn1_norm, ffn1_w_gate, ffn1_w_up, ffn1_w_down, mix_norm, w_in, a_v_norm, a_w_s, a_b_s, q_norm, k_cmp_norm, k_slc_norm, k_win_norm, cmp_k_pe, cmp_k_w1, cmp_k_w2, cmp_v_pe, cmp_v_w1, cmp_v_w2, w_branch_a, w_branch_b, w_out, ffn2_norm, ffn2_w_gate, ffn2_w_up, ffn2_w_down):
    params = (ffn1_norm, ffn1_w_gate, ffn1_w_up, ffn1_w_down, mix_norm, w_in,
              a_v_norm, a_w_s, a_b_s, q_norm, k_cmp_norm, k_slc_norm, k_win_norm,
              cmp_k_pe, cmp_k_w1, cmp_k_w2, cmp_v_pe, cmp_v_w1, cmp_v_w2,
              w_branch_a, w_branch_b, w_out, ffn2_norm, ffn2_w_gate, ffn2_w_up, ffn2_w_down)
    for l in range(params[0].shape[0]):
        x = _layer(x, *[p[l] for p in params])
    return x
```
